```python
import jax, jax.numpy as jnp
from jax import lax
import numpy as np

D_MODEL = 1024
BATCH = 16
SEQ = 2048
DEPTH = 2

CHUNK = 128
E_A = 2 * D_MODEL
N_GROUPS_A = 16
GROUP_A = E_A // N_GROUPS_A
E_B = D_MODEL
CONV_W = 3
D_FF = ((8 * D_MODEL // 3 + 255) // 256) * 256
N_A = (DEPTH + 1) // 2
N_B = DEPTH // 2
EPS = 1e-6

kernel_name = "hybrid_sgu_shortconv_swiglu"


def rms_norm(x, g):
    xf = x.astype(jnp.float32)
    y = xf * lax.rsqrt(jnp.mean(xf * xf, axis=-1, keepdims=True) + EPS)
    return (y * g.astype(jnp.float32)).astype(x.dtype)


def layer_norm(x, g, b):
    xf = x.astype(jnp.float32)
    mu = jnp.mean(xf, axis=-1, keepdims=True)
    xc = xf - mu
    y = xc * lax.rsqrt(jnp.mean(xc * xc, axis=-1, keepdims=True) + EPS)
    return (y * g.astype(jnp.float32) + b.astype(jnp.float32)).astype(x.dtype)


def spatial_gating_mixer(h, w_in, v_gain, v_bias, w_s, b_s, w_out):
    bsz, seq, _ = h.shape
    z = jax.nn.gelu(jnp.einsum('bsd,de->bse', h, w_in))
    u, v = jnp.split(z, 2, axis=-1)
    v = layer_norm(v, v_gain, v_bias)
    v = v.reshape(bsz, seq // CHUNK, CHUNK, N_GROUPS_A, GROUP_A)
    causal = jnp.tril(jnp.ones((CHUNK, CHUNK), dtype=bool))
    w = jnp.where(causal[None], w_s, jnp.zeros((), w_s.dtype))
    sv = jnp.einsum('hts,bnshc->bnthc', w, v) + b_s.T[None, None, :, :, None]
    y = u * sv.reshape(bsz, seq, E_A)
    return jnp.einsum('bse,ed->bsd', y, w_out)


def short_conv_mixer(h, w_in, conv_w, w_out):
    seq = h.shape[1]
    p = jnp.einsum('bsd,de->bse', h, w_in)
    b_gate, c_gate, hx = jnp.split(p, 3, axis=-1)
    z = c_gate * hx
    zp = jnp.pad(z, ((0, 0), (CONV_W - 1, 0), (0, 0)))
    conv = zp[:, 0:seq] * conv_w[0]
    for k in range(1, CONV_W):
        conv = conv + zp[:, k:k + seq] * conv_w[k]
    y = b_gate * conv
    return jnp.einsum('bse,ed->bsd', y, w_out)


def swiglu(h, w_gate, w_up, w_down):
    g = jnp.einsum('bsd,df->bsf', h, w_gate)
    u = jnp.einsum('bsd,df->bsf', h, w_up)
    return jnp.einsum('bsf,fd->bsd', jax.nn.silu(g) * u, w_down)


def setup_inputs(seed: int = 0) -> dict:
    key = jax.random.key(seed)
    ks = jax.random.split(key, 20)
    f32 = jnp.float32

    def nrm(k, shape, fan_in):
        return jax.random.normal(k, shape, f32) * (fan_in ** -0.5)

    def gain(k, shape):
        return 1.0 + 0.02 * jax.random.normal(k, shape, f32)

    return {
        "x": jax.random.normal(ks[0], (BATCH, SEQ, D_MODEL), f32),
        "mix_norm": gain(ks[1], (DEPTH, D_MODEL)),
        "ffn_norm": gain(ks[2], (DEPTH, D_MODEL)),
        "a_w_in": nrm(ks[3], (N_A, D_MODEL, 2 * E_A), D_MODEL),
        "a_v_gain": gain(ks[4], (N_A, E_A)),
        "a_v_bias": 0.02 * jax.random.normal(ks[5], (N_A, E_A), f32),
        "a_w_s": nrm(ks[6], (N_A, N_GROUPS_A, CHUNK, CHUNK), CHUNK),
        "a_b_s": gain(ks[7], (N_A, N_GROUPS_A, CHUNK)),
        "a_w_out": nrm(ks[8], (N_A, E_A, D_MODEL), E_A),
        "b_w_in": nrm(ks[9], (N_B, D_MODEL, 3 * E_B), D_MODEL),
        "b_conv_w": nrm(ks[10], (N_B, CONV_W, E_B), CONV_W),
        "b_w_out": nrm(ks[11], (N_B, E_B, D_MODEL), E_B),
        "ffn_w_gate": nrm(ks[12], (DEPTH, D_MODEL, D_FF), D_MODEL),
        "ffn_w_up": nrm(ks[13], (DEPTH, D_MODEL, D_FF), D_MODEL),
        "ffn_w_down": nrm(ks[14], (DEPTH, D_FF, D_MODEL), D_FF),
        "final_norm": gain(ks[15], (D_MODEL,)),
    }


def reference(x, mix_norm, ffn_norm, a_w_in, a_v_gain, a_v_bias, a_w_s, a_b_s, a_w_out,
              b_w_in, b_conv_w, b_w_out, ffn_w_gate, ffn_w_up, ffn_w_down, final_norm):
    for i in range(DEPTH):
        h = rms_norm(x, mix_norm[i])
        j = i // 2
        if i % 2 == 0:
            mix = spatial_gating_mixer(h, a_w_in[j], a_v_gain[j], a_v_bias[j],
                                       a_w_s[j], a_b_s[j], a_w_out[j])
        else:
            mix = short_conv_mixer(h, b_w_in[j], b_conv_w[j], b_w_out[j])
        x = x + mix
        x = x + swiglu(rms_norm(x, ffn_norm[i]), ffn_w_gate[i], ffn_w_up[i], ffn_w_down[i])
    return rms_norm(x, final_norm)
```

```python
import functools
import math

import jax
import jax.numpy as jnp
from jax import lax
from jax.experimental import pallas as pl
from jax.experimental.pallas import tpu as pltpu

EPS = 1e-6
CHUNK = 128
LANES = 128
MXU_N = 256
VMEM_LIMIT_BYTES = 56 * 1024 * 1024

BF16 = jnp.bfloat16
F32 = jnp.float32


def _dot(a, b):
    return jnp.dot(a, b, preferred_element_type=F32)


def _rms_norm(x, g):
    y = x * lax.rsqrt(jnp.mean(x * x, axis=-1, keepdims=True) + EPS)
    return y * g


def _gelu_tanh(x):
    c0 = math.sqrt(2.0 / math.pi)
    t = x * (c0 + (c0 * 0.044715) * (x * x))
    hx = 0.5 * x
    return hx + hx * jnp.tanh(t)


def _resident(shape):
    return pl.BlockSpec(shape, lambda *_: (0,) * len(shape), pipeline_mode=pl.Buffered(1))


def _ffn_kernel(x_ref, g_ref, wg_ref, wu_ref, wd_ref, fg_ref, o_ref, *, col_block, final_norm):
    x = x_ref[...]
    h = _rms_norm(x, g_ref[...]).astype(BF16)
    acc = x
    d_ff = wg_ref.shape[1]
    for c in range(0, d_ff, col_block):
        g = _dot(h, wg_ref[:, c:c + col_block])
        u = _dot(h, wu_ref[:, c:c + col_block])
        a = (g * (1.0 / (1.0 + jnp.exp(-g))) * u).astype(BF16)
        acc = acc + _dot(a, wd_ref[c:c + col_block, :])
    if final_norm:
        acc = _rms_norm(acc, fg_ref[...])
    o_ref[...] = acc


def _ffn(x2, g, wg, wu, wd, fg, *, tm, final_norm):
    m, d = x2.shape
    d_ff = wg.shape[1]
    return pl.pallas_call(
        functools.partial(_ffn_kernel, col_block=MXU_N, final_norm=final_norm),
        out_shape=jax.ShapeDtypeStruct((m, d), F32),
        grid=(m // tm,),
        in_specs=[
            pl.BlockSpec((tm, d), lambda i: (i, 0)),
            _resident((1, d)),
            _resident((d, d_ff)),
            _resident((d, d_ff)),
            _resident((d_ff, d)),
            _resident((1, d)),
        ],
        out_specs=pl.BlockSpec((tm, d), lambda i: (i, 0)),
        compiler_params=pltpu.CompilerParams(
            dimension_semantics=("arbitrary",), vmem_limit_bytes=VMEM_LIMIT_BYTES),
        name="swiglu_final" if final_norm else "swiglu",
    )(x2, g, wg, wu, wd, fg)


def _sgu_kernel(x_ref, g_ref, win_ref, vg_ref, vb_ref, ws_ref, bs_ref, wout_ref, o_ref,
                wtri_ref, u_ref, v_ref, vn_ref, y_ref, *, col_block):
    tm = x_ref.shape[0]
    e_a = wout_ref.shape[0]
    n_groups = ws_ref.shape[0]
    n_chunks = tm // CHUNK

    @pl.when(pl.program_id(0) == 0)
    def _():
        row = lax.broadcasted_iota(jnp.int32, (CHUNK, CHUNK), 0)
        col = lax.broadcasted_iota(jnp.int32, (CHUNK, CHUNK), 1)
        for hg in range(n_groups):
            wtri_ref[hg] = jnp.where(col <= row, ws_ref[hg], 0.0).astype(BF16)

    x = x_ref[...]
    h = _rms_norm(x, g_ref[...]).astype(BF16)

    s1 = jnp.zeros((tm, 1), F32)
    s2 = jnp.zeros((tm, 1), F32)
    for c in range(0, e_a, col_block):
        zv = _gelu_tanh(_dot(h, win_ref[:, e_a + c:e_a + c + col_block]))
        v_ref[:, c:c + col_block] = zv
        s1 = s1 + jnp.sum(zv, axis=-1, keepdims=True)
        s2 = s2 + jnp.sum(zv * zv, axis=-1, keepdims=True)
    mu = s1 * (1.0 / e_a)
    rstd = lax.rsqrt(s2 * (1.0 / e_a) - mu * mu + EPS)

    for c in range(0, e_a, col_block):
        u_ref[:, c:c + col_block] = _gelu_tanh(_dot(h, win_ref[:, c:c + col_block]))
        vn = ((v_ref[:, c:c + col_block] - mu) * rstd * vg_ref[:, c:c + col_block]
              + vb_ref[:, c:c + col_block]).astype(BF16)
        for gi in range(col_block // LANES):
            hg = c // LANES + gi
            for n in range(n_chunks):
                vn_ref[:, (hg * n_chunks + n) * LANES:(hg * n_chunks + n + 1) * LANES] = (
                    vn[n * CHUNK:(n + 1) * CHUNK, gi * LANES:(gi + 1) * LANES])

    for hg in range(n_groups):
        sv = _dot(wtri_ref[hg], vn_ref[:, hg * n_chunks * LANES:(hg + 1) * n_chunks * LANES])
        bias = bs_ref[:, hg * LANES:(hg + 1) * LANES]
        for n in range(n_chunks):
            rows = slice(n * CHUNK, (n + 1) * CHUNK)
            cols = slice(hg * LANES, (hg + 1) * LANES)
            y_ref[rows, cols] = (u_ref[rows, cols] * (sv[:, n * LANES:(n + 1) * LANES] + bias)
                                 ).astype(BF16)

    o_ref[...] = x + _dot(y_ref[...], wout_ref[...])


def _sgu(x2, g, win, vg, vb, ws, bs_full, wout, *, tm):
    m, d = x2.shape
    e_a = wout.shape[0]
    n_groups = ws.shape[0]
    return pl.pallas_call(
        functools.partial(_sgu_kernel, col_block=MXU_N),
        out_shape=jax.ShapeDtypeStruct((m, d), F32),
        grid=(m // tm,),
        in_specs=[
            pl.BlockSpec((tm, d), lambda i: (i, 0)),
            _resident((1, d)),
            _resident((d, 2 * e_a)),
            _resident((1, e_a)),
            _resident((1, e_a)),
            _resident((n_groups, CHUNK, CHUNK)),
            _resident((CHUNK, e_a)),
            _resident((e_a, d)),
        ],
        out_specs=pl.BlockSpec((tm, d), lambda i: (i, 0)),
        scratch_shapes=[
            pltpu.VMEM((n_groups, CHUNK, CHUNK), BF16),
            pltpu.VMEM((tm, e_a), F32),
            pltpu.VMEM((tm, e_a), F32),
            pltpu.VMEM((CHUNK, (tm // CHUNK) * e_a), BF16),
            pltpu.VMEM((tm, e_a), BF16),
        ],
        compiler_params=pltpu.CompilerParams(
            dimension_semantics=("arbitrary",), vmem_limit_bytes=VMEM_LIMIT_BYTES),
        name="sgu_mixer",
    )(x2, g, win, vg, vb, ws, bs_full, wout)


def _conv_kernel(x_ref, g_ref, win_ref, cw_ref, wout_ref, o_ref, z_ref, *, halo):
    tm = x_ref.shape[1]
    e_b = wout_ref.shape[0]
    conv_w = cw_ref.shape[0]

    @pl.when(pl.program_id(1) == 0)
    def _():
        z_ref[0:halo, :] = jnp.zeros((halo, e_b), F32)

    @pl.when(pl.program_id(1) != 0)
    def _():
        z_ref[0:halo, :] = z_ref[tm:tm + halo, :]

    x = x_ref[0]
    h = _rms_norm(x, g_ref[...]).astype(BF16)
    z_ref[halo:halo + tm, :] = (_dot(h, win_ref[:, e_b:2 * e_b]) * _dot(h, win_ref[:, 2 * e_b:]))
    conv = z_ref[halo:halo + tm, :] * cw_ref[conv_w - 1:conv_w, :]
    for k in range(conv_w - 1):
        back = conv_w - 1 - k
        conv = conv + z_ref[halo - back:halo - back + tm, :] * cw_ref[k:k + 1, :]
    y = (_dot(h, win_ref[:, 0:e_b]) * conv).astype(BF16)
    o_ref[0] = x + _dot(y, wout_ref[...])


def _short_conv(x3, g, win, cw, wout, *, tm):
    b, s, d = x3.shape
    e_b = wout.shape[0]
    halo = 8
    return pl.pallas_call(
        functools.partial(_conv_kernel, halo=halo),
        out_shape=jax.ShapeDtypeStruct((b, s, d), F32),
        grid=(b, s // tm),
        in_specs=[
            pl.BlockSpec((1, tm, d), lambda i, j: (i, j, 0)),
            _resident((1, d)),
            _resident((d, 3 * e_b)),
            _resident(cw.shape),
            _resident((e_b, d)),
        ],
        out_specs=pl.BlockSpec((1, tm, d), lambda i, j: (i, j, 0)),
        scratch_shapes=[pltpu.VMEM((halo + tm, e_b), F32)],
        compiler_params=pltpu.CompilerParams(
            dimension_semantics=("arbitrary", "arbitrary"), vmem_limit_bytes=VMEM_LIMIT_BYTES),
        name="short_conv_mixer",
    )(x3, g, win, cw, wout)


def kernel(x, mix_norm, ffn_norm, a_w_in, a_v_gain, a_v_bias, a_w_s, a_b_s, a_w_out,
           b_w_in, b_conv_w, b_w_out, ffn_w_gate, ffn_w_up, ffn_w_down, final_norm):
    b, s, d = x.shape
    depth = mix_norm.shape[0]
    tm = 512
    assert s % tm == 0 and tm % CHUNK == 0 and a_w_s.shape[-1] == CHUNK
    assert a_w_s.shape[1] * LANES == a_w_out.shape[1]

    fg = final_norm.reshape(1, d)
    x2 = x.reshape(b * s, d)
    for i in range(depth):
        j = i // 2
        if i % 2 == 0:
            bs_full = jnp.repeat(a_b_s[j].T, LANES, axis=1)
            x2 = _sgu(x2, mix_norm[i].reshape(1, d), a_w_in[j].astype(BF16),
                      a_v_gain[j].reshape(1, -1), a_v_bias[j].reshape(1, -1), a_w_s[j], bs_full,
                      a_w_out[j].astype(BF16), tm=tm)
        else:
            x2 = _short_conv(x2.reshape(b, s, d), mix_norm[i].reshape(1, d),
                             b_w_in[j].astype(BF16), b_conv_w[j], b_w_out[j].astype(BF16),
                             tm=tm).reshape(b * s, d)
        x2 = _ffn(x2, ffn_norm[i].reshape(1, d), ffn_w_gate[i].astype(BF16),
                  ffn_w_up[i].astype(BF16), ffn_w_down[i].astype(BF16), fg, tm=tm,
                  final_norm=(i == depth - 1))
    return x2.reshape(b, s, d)
```

```python
import functools
import math

import jax
import jax.numpy as jnp
from jax import lax
from jax.experimental import pallas as pl
from jax.experimental.pallas import tpu as pltpu

EPS = 1e-6
CHUNK = 128
LANES = 128
MXU_N = 256
VMEM_LIMIT_BYTES = 56 * 1024 * 1024

BF16 = jnp.bfloat16
F32 = jnp.float32


def _dot(a, b):
    return jnp.dot(a, b, preferred_element_type=F32)


def _rms_norm(x, g):
    y = x * lax.rsqrt(jnp.mean(x * x, axis=-1, keepdims=True) + EPS)
    return y * g


def _gelu_tanh(x):
    c0 = math.sqrt(2.0 / math.pi)
    t = x * (c0 + (c0 * 0.044715) * (x * x))
    hx = 0.5 * x
    return hx + hx * jnp.tanh(t)


def _resident(shape):
    return pl.BlockSpec(shape, lambda *_: (0,) * len(shape), pipeline_mode=pl.Buffered(1))


def _ffn_kernel(x_ref, g_ref, wg_ref, wu_ref, wd_ref, fg_ref, o_ref, *, col_block, final_norm):
    x = x_ref[...]
    h = _rms_norm(x, g_ref[...]).astype(BF16)
    acc = x
    d_ff = wg_ref.shape[1]
    for c in range(0, d_ff, col_block):
        g = _dot(h, wg_ref[:, c:c + col_block])
        u = _dot(h, wu_ref[:, c:c + col_block])
        a = (g * (1.0 / (1.0 + jnp.exp(-g))) * u).astype(BF16)
        acc = acc + _dot(a, wd_ref[c:c + col_block, :])
    if final_norm:
        acc = _rms_norm(acc, fg_ref[...])
    o_ref[...] = acc


def _ffn(x2, g, wg, wu, wd, fg, *, tm, final_norm):
    m, d = x2.shape
    d_ff = wg.shape[1]
    return pl.pallas_call(
        functools.partial(_ffn_kernel, col_block=MXU_N, final_norm=final_norm),
        out_shape=jax.ShapeDtypeStruct((m, d), F32),
        grid=(m // tm,),
        in_specs=[
            pl.BlockSpec((tm, d), lambda i: (i, 0)),
            _resident((1, d)),
            _resident((d, d_ff)),
            _resident((d, d_ff)),
            _resident((d_ff, d)),
            _resident((1, d)),
        ],
        out_specs=pl.BlockSpec((tm, d), lambda i: (i, 0)),
        compiler_params=pltpu.CompilerParams(
            dimension_semantics=("arbitrary",), vmem_limit_bytes=VMEM_LIMIT_BYTES),
        name="swiglu_final" if final_norm else "swiglu",
    )(x2, g, wg, wu, wd, fg)


def _sgu_kernel(x_ref, g_ref, win_ref, vg_ref, vb_ref, ws_ref, bs_ref, wout_ref, o_ref,
                wtri_ref, v_ref, *, col_block):
    tm = x_ref.shape[0]
    e_a = wout_ref.shape[0]
    n_groups = ws_ref.shape[0]
    n_chunks = tm // CHUNK

    @pl.when(pl.program_id(0) == 0)
    def _():
        row = lax.broadcasted_iota(jnp.int32, (CHUNK, CHUNK), 0)
        col = lax.broadcasted_iota(jnp.int32, (CHUNK, CHUNK), 1)
        for hg in range(n_groups):
            wtri_ref[hg] = jnp.where(col <= row, ws_ref[hg], 0.0).astype(BF16)

    x = x_ref[...]
    h = _rms_norm(x, g_ref[...]).astype(BF16)

    n_blocks = e_a // col_block
    groups_per_block = col_block // LANES
    s1 = jnp.zeros((tm, 1), F32)
    s2 = jnp.zeros((tm, 1), F32)
    for b in range(n_blocks):
        c = b * col_block
        zv = _gelu_tanh(_dot(h, win_ref[:, e_a + c:e_a + c + col_block]))
        v_ref[b] = zv
        s1 = s1 + jnp.sum(zv, axis=-1, keepdims=True)
        s2 = s2 + jnp.sum(zv * zv, axis=-1, keepdims=True)

    def u_proj(b):
        return _dot(h, win_ref[:, b * col_block:(b + 1) * col_block])

    def layer_norm_v(b):
        cols = slice(b * col_block, (b + 1) * col_block)
        return ((v_ref[b] - mu) * rstd * vg_ref[:, cols] + vb_ref[:, cols]).astype(BF16)

    def spatial(b, vn):
        out = []
        for gi in range(groups_per_block):
            lanes = slice(gi * LANES, (gi + 1) * LANES)
            rhs = jnp.concatenate(
                [vn[n * CHUNK:(n + 1) * CHUNK, lanes] for n in range(n_chunks)], axis=1)
            out.append(_dot(wtri_ref[b * groups_per_block + gi], rhs))
        return out

    def gate(b, u, sv):
        y_cols = []
        for gi in range(groups_per_block):
            hg = b * groups_per_block + gi
            lanes = slice(gi * LANES, (gi + 1) * LANES)
            bias = bs_ref[:, hg * LANES:(hg + 1) * LANES]
            y_cols.append(jnp.concatenate(
                [u[n * CHUNK:(n + 1) * CHUNK, lanes]
                 * (sv[gi][:, n * LANES:(n + 1) * LANES] + bias) for n in range(n_chunks)],
                axis=0))
        return jnp.concatenate(y_cols, axis=1).astype(BF16)

    zu = u_proj(0)
    mu = s1 * (1.0 / e_a)
    rstd = lax.rsqrt(s2 * (1.0 / e_a) - mu * mu + EPS)
    vn = layer_norm_v(0)
    acc = x
    y_prev = None
    for b in range(n_blocks):
        sv = spatial(b, vn)
        if y_prev is not None:
            acc = acc + _dot(y_prev, wout_ref[(b - 1) * col_block:b * col_block, :])
        zu_next = u_proj(b + 1) if b + 1 < n_blocks else None
        u = _gelu_tanh(zu)
        y_prev = gate(b, u, sv)
        if b + 1 < n_blocks:
            vn = layer_norm_v(b + 1)
        zu = zu_next
    acc = acc + _dot(y_prev, wout_ref[(n_blocks - 1) * col_block:, :])
    o_ref[...] = acc


def _sgu(x2, g, win, vg, vb, ws, bs_full, wout, *, tm):
    m, d = x2.shape
    e_a = wout.shape[0]
    n_groups = ws.shape[0]
    return pl.pallas_call(
        functools.partial(_sgu_kernel, col_block=MXU_N),
        out_shape=jax.ShapeDtypeStruct((m, d), F32),
        grid=(m // tm,),
        in_specs=[
            pl.BlockSpec((tm, d), lambda i: (i, 0)),
            _resident((1, d)),
            _resident((d, 2 * e_a)),
            _resident((1, e_a)),
            _resident((1, e_a)),
            _resident((n_groups, CHUNK, CHUNK)),
            _resident((CHUNK, e_a)),
            _resident((e_a, d)),
        ],
        out_specs=pl.BlockSpec((tm, d), lambda i: (i, 0)),
        scratch_shapes=[
            pltpu.VMEM((n_groups, CHUNK, CHUNK), BF16),
            pltpu.VMEM((e_a // MXU_N, tm, MXU_N), F32),
        ],
        compiler_params=pltpu.CompilerParams(
            dimension_semantics=("arbitrary",), vmem_limit_bytes=VMEM_LIMIT_BYTES),
        name="sgu_mixer",
    )(x2, g, win, vg, vb, ws, bs_full, wout)


def _conv_kernel(x_ref, g_ref, win_ref, cw_ref, wout_ref, o_ref, z_ref, *, halo):
    tm = x_ref.shape[1]
    e_b = wout_ref.shape[0]
    conv_w = cw_ref.shape[0]

    @pl.when(pl.program_id(1) == 0)
    def _():
        z_ref[0:halo, :] = jnp.zeros((halo, e_b), F32)

    @pl.when(pl.program_id(1) != 0)
    def _():
        z_ref[0:halo, :] = z_ref[tm:tm + halo, :]

    x = x_ref[0]
    h = _rms_norm(x, g_ref[...]).astype(BF16)
    z_ref[halo:halo + tm, :] = (_dot(h, win_ref[:, e_b:2 * e_b]) * _dot(h, win_ref[:, 2 * e_b:]))
    conv = z_ref[halo:halo + tm, :] * cw_ref[conv_w - 1:conv_w, :]
    for k in range(conv_w - 1):
        back = conv_w - 1 - k
        conv = conv + z_ref[halo - back:halo - back + tm, :] * cw_ref[k:k + 1, :]
    y = (_dot(h, win_ref[:, 0:e_b]) * conv).astype(BF16)
    o_ref[0] = x + _dot(y, wout_ref[...])


def _short_conv(x3, g, win, cw, wout, *, tm):
    b, s, d = x3.shape
    e_b = wout.shape[0]
    halo = 8
    return pl.pallas_call(
        functools.partial(_conv_kernel, halo=halo),
        out_shape=jax.ShapeDtypeStruct((b, s, d), F32),
        grid=(b, s // tm),
        in_specs=[
            pl.BlockSpec((1, tm, d), lambda i, j: (i, j, 0)),
            _resident((1, d)),
            _resident((d, 3 * e_b)),
            _resident(cw.shape),
            _resident((e_b, d)),
        ],
        out_specs=pl.BlockSpec((1, tm, d), lambda i, j: (i, j, 0)),
        scratch_shapes=[pltpu.VMEM((halo + tm, e_b), F32)],
        compiler_params=pltpu.CompilerParams(
            dimension_semantics=("arbitrary", "arbitrary"), vmem_limit_bytes=VMEM_LIMIT_BYTES),
        name="short_conv_mixer",
    )(x3, g, win, cw, wout)


def kernel(x, mix_norm, ffn_norm, a_w_in, a_v_gain, a_v_bias, a_w_s, a_b_s, a_w_out,
           b_w_in, b_conv_w, b_w_out, ffn_w_gate, ffn_w_up, ffn_w_down, final_norm):
    b, s, d = x.shape
    depth = mix_norm.shape[0]
    tm = 1024
    tm_sgu = 512
    assert s % tm == 0 and s % tm_sgu == 0 and tm_sgu % CHUNK == 0 and a_w_s.shape[-1] == CHUNK
    assert a_w_s.shape[1] * LANES == a_w_out.shape[1]

    fg = final_norm.reshape(1, d)
    x2 = x.reshape(b * s, d)
    for i in range(depth):
        j = i // 2
        if i % 2 == 0:
            bs_full = jnp.repeat(a_b_s[j].T, LANES, axis=1)
            x2 = _sgu(x2, mix_norm[i].reshape(1, d), a_w_in[j].astype(BF16),
                      a_v_gain[j].reshape(1, -1), a_v_bias[j].reshape(1, -1), a_w_s[j], bs_full,
                      a_w_out[j].astype(BF16), tm=tm_sgu)
        else:
            x2 = _short_conv(x2.reshape(b, s, d), mix_norm[i].reshape(1, d),
                             b_w_in[j].astype(BF16), b_conv_w[j], b_w_out[j].astype(BF16),
                             tm=tm).reshape(b * s, d)
        x2 = _ffn(x2, ffn_norm[i].reshape(1, d), ffn_w_gate[i].astype(BF16),
                  ffn_w_up[i].astype(BF16), ffn_w_down[i].astype(BF16), fg, tm=tm,
                  final_norm=(i == depth - 1))
    return x2.reshape(b, s, d)
```

```python
import functools
import math

import jax
import jax.numpy as jnp
from jax import lax
from jax.experimental import pallas as pl
from jax.experimental.pallas import tpu as pltpu

EPS = 1e-6
CHUNK = 128
LANES = 128
SUBLANES = 8
MXU_N = 256
OUT_PROJ_BLOCKS = 2
VMEM_LIMIT_BYTES = 56 * 1024 * 1024

BF16 = jnp.bfloat16
F32 = jnp.float32


def _dot(a, b):
    return jnp.dot(a, b, preferred_element_type=F32)


def _rms_norm(x, g):
    y = x * lax.rsqrt(jnp.mean(x * x, axis=-1, keepdims=True) + EPS)
    return y * g


def _gelu_tanh(x):
    k = -2.0 * math.sqrt(2.0 / math.pi) * math.log2(math.e)
    return x / (1.0 + jnp.exp2(x * (k + (k * 0.044715) * (x * x))))


def _odd_tile_pitch(w):
    n = w.shape[-1]
    assert n % LANES == 0
    if (n // LANES) % 2 == 1:
        return w
    return jnp.pad(w, [(0, 0)] * (w.ndim - 1) + [(0, LANES)])


def _resident(shape):
    return pl.BlockSpec(shape, lambda *_: (0,) * len(shape), pipeline_mode=pl.Buffered(1))


def _resident_layer(stacked, layer):
    rest = stacked.shape[1:]
    return pl.BlockSpec((None,) + rest, lambda *_: (layer,) + (0,) * len(rest),
                        pipeline_mode=pl.Buffered(1))


def _compiler_params(n_grid_axes):
    return pltpu.CompilerParams(dimension_semantics=("arbitrary",) * n_grid_axes,
                                vmem_limit_bytes=VMEM_LIMIT_BYTES)


def _ffn_kernel(x_ref, g_ref, wg_ref, wu_ref, wd_ref, fg_ref, o_ref, *, col_block, final_norm):
    x = x_ref[...]
    d = x.shape[1]
    h = _rms_norm(x, g_ref[...]).astype(BF16)
    acc = x
    d_ff = wd_ref.shape[0]
    for c in range(0, d_ff, col_block):
        g = _dot(h, wg_ref[:, c:c + col_block])
        u = _dot(h, wu_ref[:, c:c + col_block])
        a = (g * (1.0 / (1.0 + jnp.exp(-g))) * u).astype(BF16)
        acc = acc + _dot(a, wd_ref[c:c + col_block, 0:d])
    if final_norm:
        acc = _rms_norm(acc, fg_ref[...])
    o_ref[...] = acc


def _ffn(x2, norms, wg, wu, wd, fg, *, layer, tm, final_norm):
    m, d = x2.shape
    return pl.pallas_call(
        functools.partial(_ffn_kernel, col_block=MXU_N, final_norm=final_norm),
        out_shape=jax.ShapeDtypeStruct((m, d), F32),
        grid=(m // tm,),
        in_specs=[
            pl.BlockSpec((tm, d), lambda i: (i, 0)),
            _resident_layer(norms, layer),
            _resident_layer(wg, layer),
            _resident_layer(wu, layer),
            _resident_layer(wd, layer),
            _resident(fg.shape),
        ],
        out_specs=pl.BlockSpec((tm, d), lambda i: (i, 0)),
        compiler_params=_compiler_params(1),
        name="swiglu_final" if final_norm else "swiglu",
    )(x2, norms, wg, wu, wd, fg)


def _sgu_kernel(x_ref, xp_ref, g_ref, win_ref, vg_ref, vb_ref, ws_ref, bs_ref, wout_ref, o_ref,
                wtri_ref, h_ref, v_ref, mu_ref, rstd_ref, *, col_block):
    tm, d = x_ref.shape
    e_a = wout_ref.shape[0]
    n_groups = ws_ref.shape[0]
    n_chunks = tm // CHUNK
    n_blocks = e_a // col_block
    groups_per_block = col_block // LANES
    assert n_blocks % OUT_PROJ_BLOCKS == 0

    @pl.when(pl.program_id(0) == 0)
    def _():
        row = lax.broadcasted_iota(jnp.int32, (CHUNK, CHUNK), 0)
        col = lax.broadcasted_iota(jnp.int32, (CHUNK, CHUNK), 1)
        for hg in range(n_groups):
            wtri_ref[hg] = jnp.where(col <= row, ws_ref[hg], 0.0).astype(BF16)
        h_ref[...] = jnp.zeros(h_ref.shape, BF16)
        v_ref[...] = jnp.zeros(v_ref.shape, F32)
        mu_ref[...] = jnp.zeros(mu_ref.shape, F32)
        rstd_ref[...] = jnp.zeros(rstd_ref.shape, F32)

    def v_proj(b):
        zv = _gelu_tanh(_dot(h, win_ref[:, e_a + b * col_block:e_a + (b + 1) * col_block]))
        v_ref[b] = zv
        return jnp.sum(zv, axis=-1, keepdims=True), jnp.sum(zv * zv, axis=-1, keepdims=True)

    def u_proj(b):
        return _dot(h_prev, win_ref[:, b * col_block:(b + 1) * col_block])

    def layer_norm_v(b):
        cols = slice(b * col_block, (b + 1) * col_block)
        return ((v_ref[b] - mu) * rstd * vg_ref[:, cols] + vb_ref[:, cols]).astype(BF16)

    def spatial(b, vn):
        out = []
        for gi in range(groups_per_block):
            lanes = slice(gi * LANES, (gi + 1) * LANES)
            rhs = jnp.concatenate(
                [vn[n * CHUNK:(n + 1) * CHUNK, lanes] for n in range(n_chunks)], axis=1)
            out.append(_dot(wtri_ref[b * groups_per_block + gi], rhs))
        return out

    def gate(b, u, sv):
        y_cols = []
        for gi in range(groups_per_block):
            hg = b * groups_per_block + gi
            lanes = slice(gi * LANES, (gi + 1) * LANES)
            bias = bs_ref[:, hg * LANES:(hg + 1) * LANES]
            y_cols.append(jnp.concatenate(
                [u[n * CHUNK:(n + 1) * CHUNK, lanes]
                 * (sv[gi][:, n * LANES:(n + 1) * LANES] + bias) for n in range(n_chunks)],
                axis=0))
        return jnp.concatenate(y_cols, axis=1).astype(BF16)

    h_prev = h_ref[...]
    mu = mu_ref[...]
    rstd = rstd_ref[...]
    zu = u_proj(0)
    vn = layer_norm_v(0)
    h = _rms_norm(x_ref[...], g_ref[...]).astype(BF16)
    acc = xp_ref[...]
    s1 = jnp.zeros((tm, 1), F32)
    s2 = jnp.zeros((tm, 1), F32)
    y_blocks = []
    for b in range(n_blocks):
        p1, p2 = v_proj(b)
        s1 = s1 + p1
        s2 = s2 + p2
        sv = spatial(b, vn)
        zu_next = u_proj(b + 1) if b + 1 < n_blocks else None
        y_blocks.append(gate(b, _gelu_tanh(zu), sv))
        if len(y_blocks) == OUT_PROJ_BLOCKS:
            k0 = (b + 1 - OUT_PROJ_BLOCKS) * col_block
            acc = acc + _dot(jnp.concatenate(y_blocks, axis=1),
                             wout_ref[k0:(b + 1) * col_block, 0:d])
            y_blocks = []
        if b + 1 < n_blocks:
            vn = layer_norm_v(b + 1)
        zu = zu_next
    o_ref[...] = acc
    mu_new = s1 * (1.0 / e_a)
    mu_ref[...] = mu_new
    rstd_ref[...] = lax.rsqrt(s2 * (1.0 / e_a) - mu_new * mu_new + EPS)
    h_ref[...] = h


def _sgu(x2, norms, win, vg, vb, ws, bs_full, wout, *, layer, mixer, tm):
    m, d = x2.shape
    e_a = wout.shape[1]
    n_groups = ws.shape[1]
    n_tiles = m // tm
    cur = lambda i: (jnp.minimum(i, n_tiles - 1), 0)
    prev = lambda i: (jnp.maximum(i - 1, 0), 0)
    return pl.pallas_call(
        functools.partial(_sgu_kernel, col_block=MXU_N),
        out_shape=jax.ShapeDtypeStruct((m, d), F32),
        grid=(n_tiles + 1,),
        in_specs=[
            pl.BlockSpec((tm, d), cur),
            pl.BlockSpec((tm, d), prev),
            _resident_layer(norms, layer),
            _resident_layer(win, mixer),
            _resident_layer(vg, mixer),
            _resident_layer(vb, mixer),
            _resident_layer(ws, mixer),
            _resident(bs_full.shape),
            _resident_layer(wout, mixer),
        ],
        out_specs=pl.BlockSpec((tm, d), prev),
        scratch_shapes=[
            pltpu.VMEM((n_groups, CHUNK, CHUNK), BF16),
            pltpu.VMEM((tm, d), BF16),
            pltpu.VMEM((e_a // MXU_N, tm, MXU_N), F32),
            pltpu.VMEM((tm, 1), F32),
            pltpu.VMEM((tm, 1), F32),
        ],
        compiler_params=_compiler_params(1),
        name="sgu_mixer",
    )(x2, x2, norms, win, vg, vb, ws, bs_full, wout)


def _conv_kernel(x_ref, g_ref, win_ref, cw_ref, wout_ref, o_ref, z_ref, *, halo):
    tm = x_ref.shape[1]
    e_b = wout_ref.shape[0]
    conv_w = cw_ref.shape[0]

    @pl.when(pl.program_id(1) == 0)
    def _():
        z_ref[0:halo, :] = jnp.zeros((halo, e_b), F32)

    @pl.when(pl.program_id(1) != 0)
    def _():
        z_ref[0:halo, :] = z_ref[tm:tm + halo, :]

    x = x_ref[0]
    h = _rms_norm(x, g_ref[...]).astype(BF16)
    z_ref[halo:halo + tm, :] = (_dot(h, win_ref[:, e_b:2 * e_b]) * _dot(h, win_ref[:, 2 * e_b:3 * e_b]))
    conv = z_ref[halo:halo + tm, :] * cw_ref[conv_w - 1:conv_w, :]
    for k in range(conv_w - 1):
        back = conv_w - 1 - k
        conv = conv + z_ref[halo - back:halo - back + tm, :] * cw_ref[k:k + 1, :]
    y = (_dot(h, win_ref[:, 0:e_b]) * conv).astype(BF16)
    o_ref[0] = x + _dot(y, wout_ref[:, 0:x.shape[1]])


def _short_conv(x3, norms, win, cw, wout, *, layer, mixer, tm):
    b, s, d = x3.shape
    e_b = wout.shape[1]
    halo = SUBLANES
    assert cw.shape[1] - 1 <= halo
    return pl.pallas_call(
        functools.partial(_conv_kernel, halo=halo),
        out_shape=jax.ShapeDtypeStruct((b, s, d), F32),
        grid=(b, s // tm),
        in_specs=[
            pl.BlockSpec((1, tm, d), lambda i, j: (i, j, 0)),
            _resident_layer(norms, layer),
            _resident_layer(win, mixer),
            _resident_layer(cw, mixer),
            _resident_layer(wout, mixer),
        ],
        out_specs=pl.BlockSpec((1, tm, d), lambda i, j: (i, j, 0)),
        scratch_shapes=[pltpu.VMEM((halo + tm, e_b), F32)],
        compiler_params=_compiler_params(2),
        name="short_conv_mixer",
    )(x3, norms, win, cw, wout)


def kernel(x, mix_norm, ffn_norm, a_w_in, a_v_gain, a_v_bias, a_w_s, a_b_s, a_w_out,
           b_w_in, b_conv_w, b_w_out, ffn_w_gate, ffn_w_up, ffn_w_down, final_norm):
    b, s, d = x.shape
    depth = mix_norm.shape[0]
    tm = 1024
    tm_sgu = 512
    assert s % tm == 0 and s % tm_sgu == 0 and tm_sgu % CHUNK == 0 and a_w_s.shape[-1] == CHUNK
    assert a_w_s.shape[1] * LANES == a_w_out.shape[1]

    mix_norm3 = mix_norm.reshape(depth, 1, d)
    ffn_norm3 = ffn_norm.reshape(depth, 1, d)
    fg = final_norm.reshape(1, d)
    a_w_in, a_w_out, b_w_in, b_w_out, ffn_w_gate, ffn_w_up, ffn_w_down = (
        _odd_tile_pitch(w.astype(BF16))
        for w in (a_w_in, a_w_out, b_w_in, b_w_out, ffn_w_gate, ffn_w_up, ffn_w_down))
    a_v_gain3 = a_v_gain.reshape(a_v_gain.shape[0], 1, -1)
    a_v_bias3 = a_v_bias.reshape(a_v_bias.shape[0], 1, -1)

    x2 = x.reshape(b * s, d)
    for i in range(depth):
        j = i // 2
        if i % 2 == 0:
            bs_full = jnp.repeat(a_b_s[j].T, LANES, axis=1)
            x2 = _sgu(x2, mix_norm3, a_w_in, a_v_gain3, a_v_bias3, a_w_s, bs_full, a_w_out,
                      layer=i, mixer=j, tm=tm_sgu)
        else:
            x2 = _short_conv(x2.reshape(b, s, d), mix_norm3, b_w_in, b_conv_w, b_w_out,
                             layer=i, mixer=j, tm=tm).reshape(b * s, d)
        x2 = _ffn(x2, ffn_norm3, ffn_w_gate, ffn_w_up, ffn_w_down, fg, layer=i, tm=tm,
                  final_norm=(i == depth - 1))
    return x2.reshape(b, s, d)
```

```python
import functools
import math
from typing import NamedTuple

import jax
import jax.numpy as jnp
from jax import lax
from jax.experimental import pallas as pl
from jax.experimental.pallas import tpu as pltpu

EPS = 1e-6
CHUNK = 128
LANES = 128
SUBLANES = 8
BF16_ROWS = 16
MXU_N = 256
OUT_PROJ_BLOCKS = 4
VMEM_LIMIT_BYTES = 56 * 1024 * 1024

BF16 = jnp.bfloat16
F32 = jnp.float32


def _dot(a, b):
    return jnp.dot(a, b, preferred_element_type=F32)


def _rms_norm(x, g):
    y = x * lax.rsqrt(jnp.mean(x * x, axis=-1, keepdims=True) + EPS)
    return y * g


def _gelu_tanh(x):
    k = -2.0 * math.sqrt(2.0 / math.pi) * math.log2(math.e)
    return x / (1.0 + jnp.exp2(x * (k + (k * 0.044715) * (x * x))))


def _odd_tile_pitch(n):
    assert n % LANES == 0
    return n if (n // LANES) % 2 == 1 else n + LANES


def _resident(shape):
    return pl.BlockSpec(shape, lambda *_: (0,) * len(shape), pipeline_mode=pl.Buffered(1))


def _resident_layer(stacked, layer):
    rest = stacked.shape[1:]
    return pl.BlockSpec((None,) + rest, lambda *_: (layer,) + (0,) * len(rest),
                        pipeline_mode=pl.Buffered(1))


def _compiler_params(n_grid_axes):
    return pltpu.CompilerParams(dimension_semantics=("arbitrary",) * n_grid_axes,
                                vmem_limit_bytes=VMEM_LIMIT_BYTES)


class _Cast(NamedTuple):
    stacked: jax.Array
    layer: int
    rows: int

    @property
    def n_blocks(self):
        n_rows = self.stacked.shape[1]
        assert n_rows % self.rows == 0 and self.rows % BF16_ROWS == 0
        return n_rows // self.rows

    def out_shape(self):
        _, n_rows, n_cols = self.stacked.shape
        return jax.ShapeDtypeStruct((n_rows, _odd_tile_pitch(n_cols)), BF16)

    def specs(self, linear_step):
        n_cols = self.stacked.shape[2]
        block = lambda *g: jnp.minimum(linear_step(*g), self.n_blocks - 1)
        return (pl.BlockSpec((None, self.rows, n_cols), lambda *g: (self.layer, block(*g), 0)),
                pl.BlockSpec((self.rows, _odd_tile_pitch(n_cols)), lambda *g: (block(*g), 0)))


def _cast_jobs(weights, layer, n_steps):
    jobs = []
    for w in weights:
        n_rows = w.shape[1]
        rows = next(r for r in range(BF16_ROWS, n_rows + 1, BF16_ROWS)
                    if n_rows % r == 0 and n_rows // r <= n_steps)
        jobs.append(_Cast(w, layer, rows))
    return jobs


def _cast_blocks(src_refs, dst_refs):
    for src, dst in zip(src_refs, dst_refs):
        n = src.shape[1]
        dst[:, 0:n] = src[...].astype(BF16)
        if dst.shape[1] > n:
            dst[:, n:] = jnp.zeros((dst.shape[0], dst.shape[1] - n), BF16)


def _cast_now(w, layer):
    n = w.shape[2]
    return jnp.pad(w[layer].astype(BF16), ((0, 0), (0, _odd_tile_pitch(n) - n)))


def _ffn_kernel(*refs, n_casts, col_block, final_norm):
    x_ref, g_ref, wg_ref, wu_ref, wd_ref, fg_ref = refs[:6]
    cast_src = refs[6:6 + n_casts]
    o_ref = refs[6 + n_casts]
    cast_dst = refs[7 + n_casts:]

    x = x_ref[...]
    d = x.shape[1]
    h = _rms_norm(x, g_ref[...]).astype(BF16)
    acc = x
    d_ff = wd_ref.shape[0]
    for c in range(0, d_ff, col_block):
        g = _dot(h, wg_ref[:, c:c + col_block])
        u = _dot(h, wu_ref[:, c:c + col_block])
        a = (g * (1.0 / (1.0 + jnp.exp(-g))) * u).astype(BF16)
        acc = acc + _dot(a, wd_ref[c:c + col_block, 0:d])
    if final_norm:
        acc = _rms_norm(acc, fg_ref[...])
    o_ref[...] = acc
    _cast_blocks(cast_src, cast_dst)


def _ffn(x2, norms, wg, wu, wd, fg, next_weights, *, layer, next_mixer, tm, final_norm):
    m, d = x2.shape
    n_steps = m // tm
    casts = _cast_jobs(next_weights, next_mixer, n_steps)
    cast_specs = [c.specs(lambda i: i) for c in casts]
    outs = pl.pallas_call(
        functools.partial(_ffn_kernel, n_casts=len(casts), col_block=MXU_N,
                          final_norm=final_norm),
        out_shape=[jax.ShapeDtypeStruct((m, d), F32)] + [c.out_shape() for c in casts],
        grid=(n_steps,),
        in_specs=[
            pl.BlockSpec((tm, d), lambda i: (i, 0)),
            _resident_layer(norms, layer),
            _resident(wg.shape),
            _resident(wu.shape),
            _resident(wd.shape),
            _resident(fg.shape),
        ] + [s[0] for s in cast_specs],
        out_specs=[pl.BlockSpec((tm, d), lambda i: (i, 0))] + [s[1] for s in cast_specs],
        compiler_params=_compiler_params(1),
        name="swiglu_final" if final_norm else "swiglu",
    )(x2, norms, wg, wu, wd, fg, *[c.stacked for c in casts])
    return outs[0], outs[1:]


def _sgu_kernel(*refs, n_casts, col_block):
    x_ref, xp_ref, g_ref, win_ref, vg_ref, vb_ref, ws_ref, bs_ref, wout_ref = refs[:9]
    cast_src = refs[9:9 + n_casts]
    o_ref = refs[9 + n_casts]
    cast_dst = refs[10 + n_casts:10 + 2 * n_casts]
    wtri_ref, gate_bias_ref, h_ref, v_ref, mu_ref, rstd_ref = refs[10 + 2 * n_casts:]

    tm, d = x_ref.shape
    e_a = wout_ref.shape[0]
    n_groups = ws_ref.shape[0]
    n_chunks = tm // CHUNK
    n_blocks = e_a // col_block
    groups_per_block = col_block // LANES
    assert n_blocks % OUT_PROJ_BLOCKS == 0

    @pl.when(pl.program_id(0) == 0)
    def _():
        row = lax.broadcasted_iota(jnp.int32, (CHUNK, CHUNK), 0)
        col = lax.broadcasted_iota(jnp.int32, (CHUNK, CHUNK), 1)
        for hg in range(n_groups):
            lanes = slice(hg * LANES, (hg + 1) * LANES)
            w_causal = jnp.where(col <= row, ws_ref[hg], 0.0)
            wtri_ref[hg] = w_causal.astype(BF16)
            gate_bias_ref[:, lanes] = (jnp.sum(w_causal, axis=1, keepdims=True) * vb_ref[:, lanes]
                                       + bs_ref[:, lanes])
        h_ref[...] = jnp.zeros(h_ref.shape, BF16)
        v_ref[...] = jnp.zeros(v_ref.shape, F32)
        mu_ref[...] = jnp.zeros(mu_ref.shape, F32)
        rstd_ref[...] = jnp.zeros(rstd_ref.shape, F32)

    def v_proj(b):
        zv = _gelu_tanh(_dot(h, win_ref[:, e_a + b * col_block:e_a + (b + 1) * col_block]))
        v_ref[b] = zv
        return jnp.sum(zv, axis=-1, keepdims=True), jnp.sum(zv * zv, axis=-1, keepdims=True)

    def u_proj(b):
        return _dot(h_prev, win_ref[:, b * col_block:(b + 1) * col_block])

    def standardize_v(b):
        return ((v_ref[b] - mu) * rstd).astype(BF16)

    def spatial(b, vhat):
        out = []
        for gi in range(groups_per_block):
            lanes = slice(gi * LANES, (gi + 1) * LANES)
            rhs = jnp.concatenate(
                [vhat[n * CHUNK:(n + 1) * CHUNK, lanes] for n in range(n_chunks)], axis=1)
            out.append(_dot(wtri_ref[b * groups_per_block + gi], rhs))
        return out

    def gate(b, u, sv):
        y_cols = []
        for gi in range(groups_per_block):
            hg = b * groups_per_block + gi
            lanes = slice(gi * LANES, (gi + 1) * LANES)
            gain = vg_ref[:, hg * LANES:(hg + 1) * LANES]
            bias = gate_bias_ref[:, hg * LANES:(hg + 1) * LANES]
            y_cols.append(jnp.concatenate(
                [u[n * CHUNK:(n + 1) * CHUNK, lanes]
                 * (sv[gi][:, n * LANES:(n + 1) * LANES] * gain + bias) for n in range(n_chunks)],
                axis=0))
        return jnp.concatenate(y_cols, axis=1).astype(BF16)

    h_prev = h_ref[...]
    mu = mu_ref[...]
    rstd = rstd_ref[...]
    zu = u_proj(0)
    vhat = standardize_v(0)
    h = _rms_norm(x_ref[...], g_ref[...]).astype(BF16)
    acc = xp_ref[...]
    s1 = jnp.zeros((tm, 1), F32)
    s2 = jnp.zeros((tm, 1), F32)
    y_blocks = []
    for b in range(n_blocks):
        p1, p2 = v_proj(b)
        s1 = s1 + p1
        s2 = s2 + p2
        sv = spatial(b, vhat)
        zu_next = u_proj(b + 1) if b + 1 < n_blocks else None
        y_blocks.append(gate(b, _gelu_tanh(zu), sv))
        if len(y_blocks) == OUT_PROJ_BLOCKS:
            k0 = (b + 1 - OUT_PROJ_BLOCKS) * col_block
            acc = acc + _dot(jnp.concatenate(y_blocks, axis=1),
                             wout_ref[k0:(b + 1) * col_block, 0:d])
            y_blocks = []
        if b + 1 < n_blocks:
            vhat = standardize_v(b + 1)
        zu = zu_next
    o_ref[...] = acc
    mu_new = s1 * (1.0 / e_a)
    mu_ref[...] = mu_new
    rstd_ref[...] = lax.rsqrt(s2 * (1.0 / e_a) - mu_new * mu_new + EPS)
    h_ref[...] = h
    _cast_blocks(cast_src, cast_dst)


def _sgu(x2, norms, win, vg, vb, ws, bs_full, wout, next_weights, *, layer, mixer, tm):
    m, d = x2.shape
    e_a = wout.shape[0]
    n_groups = ws.shape[1]
    n_tiles = m // tm
    casts = _cast_jobs(next_weights, layer, n_tiles)
    cast_specs = [c.specs(lambda i: i) for c in casts]
    cur = lambda i: (jnp.minimum(i, n_tiles - 1), 0)
    prev = lambda i: (jnp.maximum(i - 1, 0), 0)
    outs = pl.pallas_call(
        functools.partial(_sgu_kernel, n_casts=len(casts), col_block=MXU_N),
        out_shape=[jax.ShapeDtypeStruct((m, d), F32)] + [c.out_shape() for c in casts],
        grid=(n_tiles + 1,),
        in_specs=[
            pl.BlockSpec((tm, d), cur),
            pl.BlockSpec((tm, d), prev),
            _resident_layer(norms, layer),
            _resident(win.shape),
            _resident_layer(vg, mixer),
            _resident_layer(vb, mixer),
            _resident_layer(ws, mixer),
            _resident(bs_full.shape),
            _resident(wout.shape),
        ] + [s[0] for s in cast_specs],
        out_specs=[pl.BlockSpec((tm, d), prev)] + [s[1] for s in cast_specs],
        scratch_shapes=[
            pltpu.VMEM((n_groups, CHUNK, CHUNK), BF16),
            pltpu.VMEM((CHUNK, e_a), F32),
            pltpu.VMEM((tm, d), BF16),
            pltpu.VMEM((e_a // MXU_N, tm, MXU_N), F32),
            pltpu.VMEM((tm, 1), F32),
            pltpu.VMEM((tm, 1), F32),
        ],
        compiler_params=_compiler_params(1),
        name="sgu_mixer",
    )(x2, x2, norms, win, vg, vb, ws, bs_full, wout, *[c.stacked for c in casts])
    return outs[0], outs[1:]


def _conv_kernel(*refs, n_casts, halo):
    x_ref, g_ref, win_ref, cw_ref, wout_ref = refs[:5]
    cast_src = refs[5:5 + n_casts]
    o_ref = refs[5 + n_casts]
    cast_dst = refs[6 + n_casts:6 + 2 * n_casts]
    z_ref, = refs[6 + 2 * n_casts:]

    tm = x_ref.shape[1]
    e_b = wout_ref.shape[0]
    conv_w = cw_ref.shape[0]

    @pl.when(pl.program_id(1) == 0)
    def _():
        z_ref[0:halo, :] = jnp.zeros((halo, e_b), F32)

    @pl.when(pl.program_id(1) != 0)
    def _():
        z_ref[0:halo, :] = z_ref[tm:tm + halo, :]

    x = x_ref[0]
    h = _rms_norm(x, g_ref[...]).astype(BF16)
    z_ref[halo:halo + tm, :] = (_dot(h, win_ref[:, e_b:2 * e_b])
                                * _dot(h, win_ref[:, 2 * e_b:3 * e_b]))
    conv = z_ref[halo:halo + tm, :] * cw_ref[conv_w - 1:conv_w, :]
    for k in range(conv_w - 1):
        back = conv_w - 1 - k
        conv = conv + z_ref[halo - back:halo - back + tm, :] * cw_ref[k:k + 1, :]
    y = (_dot(h, win_ref[:, 0:e_b]) * conv).astype(BF16)
    o_ref[0] = x + _dot(y, wout_ref[:, 0:x.shape[1]])
    _cast_blocks(cast_src, cast_dst)


def _short_conv(x3, norms, win, cw, wout, next_weights, *, layer, mixer, tm):
    b, s, d = x3.shape
    e_b = wout.shape[0]
    halo = SUBLANES
    assert cw.shape[1] - 1 <= halo
    tiles_per_seq = s // tm
    casts = _cast_jobs(next_weights, layer, b * tiles_per_seq)
    cast_specs = [c.specs(lambda i, j: i * tiles_per_seq + j) for c in casts]
    outs = pl.pallas_call(
        functools.partial(_conv_kernel, n_casts=len(casts), halo=halo),
        out_shape=[jax.ShapeDtypeStruct((b, s, d), F32)] + [c.out_shape() for c in casts],
        grid=(b, tiles_per_seq),
        in_specs=[
            pl.BlockSpec((1, tm, d), lambda i, j: (i, j, 0)),
            _resident_layer(norms, layer),
            _resident(win.shape),
            _resident_layer(cw, mixer),
            _resident(wout.shape),
        ] + [s[0] for s in cast_specs],
        out_specs=[pl.BlockSpec((1, tm, d), lambda i, j: (i, j, 0))] + [s[1] for s in cast_specs],
        scratch_shapes=[pltpu.VMEM((halo + tm, e_b), F32)],
        compiler_params=_compiler_params(2),
        name="short_conv_mixer",
    )(x3, norms, win, cw, wout, *[c.stacked for c in casts])
    return outs[0], outs[1:]


def kernel(x, mix_norm, ffn_norm, a_w_in, a_v_gain, a_v_bias, a_w_s, a_b_s, a_w_out,
           b_w_in, b_conv_w, b_w_out, ffn_w_gate, ffn_w_up, ffn_w_down, final_norm):
    b, s, d = x.shape
    depth = mix_norm.shape[0]
    tm = 1024
    tm_sgu = 512
    assert s % tm == 0 and s % tm_sgu == 0 and tm_sgu % CHUNK == 0 and a_w_s.shape[-1] == CHUNK
    assert a_w_s.shape[1] * LANES == a_w_out.shape[1]

    mix_norm3 = mix_norm.reshape(depth, 1, d)
    ffn_norm3 = ffn_norm.reshape(depth, 1, d)
    fg = final_norm.reshape(1, d)
    a_v_gain3 = a_v_gain.reshape(a_v_gain.shape[0], 1, -1)
    a_v_bias3 = a_v_bias.reshape(a_v_bias.shape[0], 1, -1)
    ffn_weights = (ffn_w_gate, ffn_w_up, ffn_w_down)

    def mixer_weights(i):
        return (a_w_in, a_w_out) if i % 2 == 0 else (b_w_in, b_w_out)

    ready = [_cast_now(w, 0) for w in mixer_weights(0)]
    x2 = x.reshape(b * s, d)
    for i in range(depth):
        j = i // 2
        if i % 2 == 0:
            bs_full = jnp.repeat(a_b_s[j].T, LANES, axis=1)
            x2, ready = _sgu(x2, mix_norm3, ready[0], a_v_gain3, a_v_bias3, a_w_s, bs_full, ready[1],
                             ffn_weights, layer=i, mixer=j, tm=tm_sgu)
        else:
            x3, ready = _short_conv(x2.reshape(b, s, d), mix_norm3, ready[0], b_conv_w, ready[1],
                                    ffn_weights, layer=i, mixer=j, tm=tm)
            x2 = x3.reshape(b * s, d)
        last = i == depth - 1
        x2, ready = _ffn(x2, ffn_norm3, *ready, fg, () if last else mixer_weights(i + 1),
                         layer=i, next_mixer=(i + 1) // 2, tm=tm, final_norm=last)
    return x2.reshape(b, s, d)
```

```python
import functools
import math
from typing import NamedTuple

import jax
import jax.numpy as jnp
from jax import lax
from jax.experimental import pallas as pl
from jax.experimental.pallas import tpu as pltpu

EPS = 1e-6
CHUNK = 128
LANES = 128
SUBLANES = 8
BF16_ROWS = 16
MXU_N = 256
OUT_PROJ_BLOCKS = 4
DOWN_PROJ_BLOCKS = 4
VMEM_LIMIT_BYTES = 56 * 1024 * 1024

BF16 = jnp.bfloat16
F32 = jnp.float32


def _dot(a, b):
    return jnp.dot(a, b, preferred_element_type=F32)


def _rms_norm(x, g):
    y = x * lax.rsqrt(jnp.mean(x * x, axis=-1, keepdims=True) + EPS)
    return y * g


def _gelu_tanh(x):
    k = -2.0 * math.sqrt(2.0 / math.pi) * math.log2(math.e)
    return x / (1.0 + jnp.exp2(x * (k + (k * 0.044715) * (x * x))))


def _odd_tile_pitch(n):
    assert n % LANES == 0
    return n if (n // LANES) % 2 == 1 else n + LANES


def _resident(shape):
    return pl.BlockSpec(shape, lambda *_: (0,) * len(shape), pipeline_mode=pl.Buffered(1))


def _resident_layer(stacked, layer):
    rest = stacked.shape[1:]
    return pl.BlockSpec((None,) + rest, lambda *_: (layer,) + (0,) * len(rest),
                        pipeline_mode=pl.Buffered(1))


def _compiler_params(n_grid_axes):
    return pltpu.CompilerParams(dimension_semantics=("arbitrary",) * n_grid_axes,
                                vmem_limit_bytes=VMEM_LIMIT_BYTES)


class _Cast(NamedTuple):
    stacked: jax.Array
    layer: int
    rows: int

    @property
    def n_blocks(self):
        n_rows = self.stacked.shape[1]
        assert n_rows % self.rows == 0 and self.rows % BF16_ROWS == 0
        return n_rows // self.rows

    def out_shape(self):
        _, n_rows, n_cols = self.stacked.shape
        return jax.ShapeDtypeStruct((n_rows, _odd_tile_pitch(n_cols)), BF16)

    def specs(self, linear_step):
        n_cols = self.stacked.shape[2]
        block = lambda *g: jnp.minimum(linear_step(*g), self.n_blocks - 1)
        return (pl.BlockSpec((None, self.rows, n_cols), lambda *g: (self.layer, block(*g), 0)),
                pl.BlockSpec((self.rows, _odd_tile_pitch(n_cols)), lambda *g: (block(*g), 0)))


def _cast_jobs(weights, layer, n_steps):
    jobs = []
    for w in weights:
        n_rows = w.shape[1]
        rows = next(r for r in range(BF16_ROWS, n_rows + 1, BF16_ROWS)
                    if n_rows % r == 0 and n_rows // r <= n_steps)
        jobs.append(_Cast(w, layer, rows))
    return jobs


def _cast_blocks(src_refs, dst_refs):
    for src, dst in zip(src_refs, dst_refs):
        n = src.shape[1]
        dst[:, 0:n] = src[...].astype(BF16)
        if dst.shape[1] > n:
            dst[:, n:] = jnp.zeros((dst.shape[0], dst.shape[1] - n), BF16)


def _cast_kernel(*refs):
    n = len(refs) // 2
    _cast_blocks(refs[:n], refs[n:])


def _cast_call(weights, layer):
    n_steps = 8
    casts = _cast_jobs(weights, layer, n_steps)
    specs = [c.specs(lambda i: i) for c in casts]
    return pl.pallas_call(
        _cast_kernel,
        out_shape=[c.out_shape() for c in casts],
        grid=(n_steps,),
        in_specs=[s[0] for s in specs],
        out_specs=[s[1] for s in specs],
        compiler_params=_compiler_params(1),
        name="cast_weights",
    )(*[c.stacked for c in casts])


def _ffn_kernel(*refs, layer, n_casts, col_block, final_norm):
    x_ref, g_ref, wg_ref, wu_ref, wd_ref, fg_ref = refs[:6]
    cast_src = refs[6:6 + n_casts]
    o_ref = refs[6 + n_casts]
    cast_dst = refs[7 + n_casts:]

    x = x_ref[...]
    d = x.shape[1]
    h = _rms_norm(x, g_ref[layer:layer + 1, :]).astype(BF16)
    acc = x
    d_ff = wd_ref.shape[0]
    a_blocks = []
    for c in range(0, d_ff, col_block):
        g = _dot(h, wg_ref[:, c:c + col_block])
        u = _dot(h, wu_ref[:, c:c + col_block])
        a_blocks.append((g * (1.0 / (1.0 + jnp.exp(-g))) * u).astype(BF16))
        if len(a_blocks) == DOWN_PROJ_BLOCKS or c + col_block == d_ff:
            k0 = c + col_block - len(a_blocks) * col_block
            acc = acc + _dot(jnp.concatenate(a_blocks, axis=1), wd_ref[k0:c + col_block, 0:d])
            a_blocks = []
    if final_norm:
        acc = _rms_norm(acc, fg_ref[...])
    o_ref[...] = acc
    _cast_blocks(cast_src, cast_dst)


def _ffn(x2, norms, wg, wu, wd, fg, next_weights, *, layer, next_mixer, tm, final_norm):
    m, d = x2.shape
    n_steps = m // tm
    casts = _cast_jobs(next_weights, next_mixer, n_steps)
    cast_specs = [c.specs(lambda i: i) for c in casts]
    outs = pl.pallas_call(
        functools.partial(_ffn_kernel, layer=layer, n_casts=len(casts), col_block=MXU_N,
                          final_norm=final_norm),
        out_shape=[jax.ShapeDtypeStruct((m, d), F32)] + [c.out_shape() for c in casts],
        grid=(n_steps,),
        in_specs=[
            pl.BlockSpec((tm, d), lambda i: (i, 0)),
            _resident(norms.shape),
            _resident(wg.shape),
            _resident(wu.shape),
            _resident(wd.shape),
            _resident(fg.shape),
        ] + [s[0] for s in cast_specs],
        out_specs=[pl.BlockSpec((tm, d), lambda i: (i, 0))] + [s[1] for s in cast_specs],
        compiler_params=_compiler_params(1),
        name="swiglu_final" if final_norm else "swiglu",
    )(x2, norms, wg, wu, wd, fg, *[c.stacked for c in casts])
    return outs[0], outs[1:]


def _sgu_kernel(*refs, layer, n_casts, col_block):
    x_ref, xp_ref, g_ref, win_ref, vg_ref, vb_ref, ws_ref, bs_ref, wout_ref = refs[:9]
    cast_src = refs[9:9 + n_casts]
    o_ref = refs[9 + n_casts]
    cast_dst = refs[10 + n_casts:10 + 2 * n_casts]
    wtri_ref, gate_bias_ref, h_ref, v_ref, mu_ref, rstd_ref = refs[10 + 2 * n_casts:]

    tm, d = x_ref.shape
    e_a = wout_ref.shape[0]
    n_groups = ws_ref.shape[0]
    n_chunks = tm // CHUNK
    n_blocks = e_a // col_block
    groups_per_block = col_block // LANES
    assert n_blocks % OUT_PROJ_BLOCKS == 0

    @pl.when(pl.program_id(0) == 0)
    def _():
        row = lax.broadcasted_iota(jnp.int32, (CHUNK, CHUNK), 0)
        col = lax.broadcasted_iota(jnp.int32, (CHUNK, CHUNK), 1)
        for hg in range(n_groups):
            lanes = slice(hg * LANES, (hg + 1) * LANES)
            w_causal = jnp.where(col <= row, ws_ref[hg], 0.0)
            wtri_ref[hg] = w_causal.astype(BF16)
            b_col = jnp.sum(jnp.where(col == row, bs_ref[hg:hg + 1, :], 0.0), axis=1, keepdims=True)
            gate_bias_ref[:, lanes] = (jnp.sum(w_causal, axis=1, keepdims=True) * vb_ref[:, lanes]
                                       + b_col)
        h_ref[...] = jnp.zeros(h_ref.shape, BF16)
        v_ref[...] = jnp.zeros(v_ref.shape, F32)
        mu_ref[...] = jnp.zeros(mu_ref.shape, F32)
        rstd_ref[...] = jnp.zeros(rstd_ref.shape, F32)

    def v_proj(b):
        zv = _gelu_tanh(_dot(h, win_ref[:, e_a + b * col_block:e_a + (b + 1) * col_block]))
        v_ref[b] = zv
        return jnp.sum(zv, axis=-1, keepdims=True), jnp.sum(zv * zv, axis=-1, keepdims=True)

    def u_proj(b):
        return _dot(h_prev, win_ref[:, b * col_block:(b + 1) * col_block])

    def standardize_v(b):
        return ((v_ref[b] - mu) * rstd).astype(BF16)

    def spatial(b, vhat):
        out = []
        for gi in range(groups_per_block):
            lanes = slice(gi * LANES, (gi + 1) * LANES)
            rhs = jnp.concatenate(
                [vhat[n * CHUNK:(n + 1) * CHUNK, lanes] for n in range(n_chunks)], axis=1)
            out.append(_dot(wtri_ref[b * groups_per_block + gi], rhs))
        return out

    def gate(b, u, sv):
        y_cols = []
        for gi in range(groups_per_block):
            hg = b * groups_per_block + gi
            lanes = slice(gi * LANES, (gi + 1) * LANES)
            gain = vg_ref[:, hg * LANES:(hg + 1) * LANES]
            bias = gate_bias_ref[:, hg * LANES:(hg + 1) * LANES]
            y_cols.append(jnp.concatenate(
                [u[n * CHUNK:(n + 1) * CHUNK, lanes]
                 * (sv[gi][:, n * LANES:(n + 1) * LANES] * gain + bias) for n in range(n_chunks)],
                axis=0))
        return jnp.concatenate(y_cols, axis=1).astype(BF16)

    h_prev = h_ref[...]
    mu = mu_ref[...]
    rstd = rstd_ref[...]
    zu = u_proj(0)
    vhat = standardize_v(0)
    h = _rms_norm(x_ref[...], g_ref[layer:layer + 1, :]).astype(BF16)
    acc = xp_ref[...]
    s1 = jnp.zeros((tm, 1), F32)
    s2 = jnp.zeros((tm, 1), F32)
    y_blocks = []
    for b in range(n_blocks):
        p1, p2 = v_proj(b)
        s1 = s1 + p1
        s2 = s2 + p2
        sv = spatial(b, vhat)
        zu_next = u_proj(b + 1) if b + 1 < n_blocks else None
        y_blocks.append(gate(b, _gelu_tanh(zu), sv))
        if len(y_blocks) == OUT_PROJ_BLOCKS:
            k0 = (b + 1 - OUT_PROJ_BLOCKS) * col_block
            acc = acc + _dot(jnp.concatenate(y_blocks, axis=1),
                             wout_ref[k0:(b + 1) * col_block, 0:d])
            y_blocks = []
        if b + 1 < n_blocks:
            vhat = standardize_v(b + 1)
        zu = zu_next
    o_ref[...] = acc
    mu_new = s1 * (1.0 / e_a)
    mu_ref[...] = mu_new
    rstd_ref[...] = lax.rsqrt(s2 * (1.0 / e_a) - mu_new * mu_new + EPS)
    h_ref[...] = h
    _cast_blocks(cast_src, cast_dst)


def _sgu(x2, norms, win, vg, vb, ws, bs, wout, next_weights, *, layer, mixer, tm):
    m, d = x2.shape
    e_a = wout.shape[0]
    n_groups = ws.shape[1]
    n_tiles = m // tm
    casts = _cast_jobs(next_weights, layer, n_tiles)
    cast_specs = [c.specs(lambda i: i) for c in casts]
    cur = lambda i: (jnp.minimum(i, n_tiles - 1), 0)
    prev = lambda i: (jnp.maximum(i - 1, 0), 0)
    outs = pl.pallas_call(
        functools.partial(_sgu_kernel, layer=layer, n_casts=len(casts), col_block=MXU_N),
        out_shape=[jax.ShapeDtypeStruct((m, d), F32)] + [c.out_shape() for c in casts],
        grid=(n_tiles + 1,),
        in_specs=[
            pl.BlockSpec((tm, d), cur),
            pl.BlockSpec((tm, d), prev),
            _resident(norms.shape),
            _resident(win.shape),
            _resident_layer(vg, mixer),
            _resident_layer(vb, mixer),
            _resident_layer(ws, mixer),
            _resident_layer(bs, mixer),
            _resident(wout.shape),
        ] + [s[0] for s in cast_specs],
        out_specs=[pl.BlockSpec((tm, d), prev)] + [s[1] for s in cast_specs],
        scratch_shapes=[
            pltpu.VMEM((n_groups, CHUNK, CHUNK), BF16),
            pltpu.VMEM((CHUNK, e_a), F32),
            pltpu.VMEM((tm, d), BF16),
            pltpu.VMEM((e_a // MXU_N, tm, MXU_N), F32),
            pltpu.VMEM((tm, 1), F32),
            pltpu.VMEM((tm, 1), F32),
        ],
        compiler_params=_compiler_params(1),
        name="sgu_mixer",
    )(x2, x2, norms, win, vg, vb, ws, bs, wout, *[c.stacked for c in casts])
    return outs[0], outs[1:]


def _conv_kernel(*refs, layer, n_casts, halo):
    x_ref, g_ref, win_ref, cw_ref, wout_ref = refs[:5]
    cast_src = refs[5:5 + n_casts]
    o_ref = refs[5 + n_casts]
    cast_dst = refs[6 + n_casts:6 + 2 * n_casts]
    z_ref, = refs[6 + 2 * n_casts:]

    tm = x_ref.shape[1]
    e_b = wout_ref.shape[0]
    conv_w = cw_ref.shape[0]

    @pl.when(pl.program_id(1) == 0)
    def _():
        z_ref[0:halo, :] = jnp.zeros((halo, e_b), F32)

    @pl.when(pl.program_id(1) != 0)
    def _():
        z_ref[0:halo, :] = z_ref[tm:tm + halo, :]

    x = x_ref[0]
    h = _rms_norm(x, g_ref[layer:layer + 1, :]).astype(BF16)
    z_ref[halo:halo + tm, :] = (_dot(h, win_ref[:, e_b:2 * e_b])
                                * _dot(h, win_ref[:, 2 * e_b:3 * e_b]))
    conv = z_ref[halo:halo + tm, :] * cw_ref[conv_w - 1:conv_w, :]
    for k in range(conv_w - 1):
        back = conv_w - 1 - k
        conv = conv + z_ref[halo - back:halo - back + tm, :] * cw_ref[k:k + 1, :]
    y = (_dot(h, win_ref[:, 0:e_b]) * conv).astype(BF16)
    o_ref[0] = x + _dot(y, wout_ref[:, 0:x.shape[1]])
    _cast_blocks(cast_src, cast_dst)


def _short_conv(x3, norms, win, cw, wout, next_weights, *, layer, mixer, tm):
    b, s, d = x3.shape
    e_b = wout.shape[0]
    halo = SUBLANES
    assert cw.shape[1] - 1 <= halo
    tiles_per_seq = s // tm
    casts = _cast_jobs(next_weights, layer, b * tiles_per_seq)
    cast_specs = [c.specs(lambda i, j: i * tiles_per_seq + j) for c in casts]
    outs = pl.pallas_call(
        functools.partial(_conv_kernel, layer=layer, n_casts=len(casts), halo=halo),
        out_shape=[jax.ShapeDtypeStruct((b, s, d), F32)] + [c.out_shape() for c in casts],
        grid=(b, tiles_per_seq),
        in_specs=[
            pl.BlockSpec((1, tm, d), lambda i, j: (i, j, 0)),
            _resident(norms.shape),
            _resident(win.shape),
            _resident_layer(cw, mixer),
            _resident(wout.shape),
        ] + [s[0] for s in cast_specs],
        out_specs=[pl.BlockSpec((1, tm, d), lambda i, j: (i, j, 0))] + [s[1] for s in cast_specs],
        scratch_shapes=[pltpu.VMEM((halo + tm, e_b), F32)],
        compiler_params=_compiler_params(2),
        name="short_conv_mixer",
    )(x3, norms, win, cw, wout, *[c.stacked for c in casts])
    return outs[0], outs[1:]


def kernel(x, mix_norm, ffn_norm, a_w_in, a_v_gain, a_v_bias, a_w_s, a_b_s, a_w_out,
           b_w_in, b_conv_w, b_w_out, ffn_w_gate, ffn_w_up, ffn_w_down, final_norm):
    b, s, d = x.shape
    depth = mix_norm.shape[0]
    tm = 1024
    tm_sgu = 512
    assert s % tm == 0 and s % tm_sgu == 0 and tm_sgu % CHUNK == 0 and a_w_s.shape[-1] == CHUNK
    assert a_w_s.shape[1] * LANES == a_w_out.shape[1]

    fg = final_norm.reshape(1, d)
    a_v_gain3 = a_v_gain.reshape(a_v_gain.shape[0], 1, -1)
    a_v_bias3 = a_v_bias.reshape(a_v_bias.shape[0], 1, -1)
    ffn_weights = (ffn_w_gate, ffn_w_up, ffn_w_down)

    def mixer_weights(i):
        return (a_w_in, a_w_out) if i % 2 == 0 else (b_w_in, b_w_out)

    ready = _cast_call(mixer_weights(0), 0)
    x2 = x.reshape(b * s, d)
    for i in range(depth):
        j = i // 2
        if i % 2 == 0:
            x2, ready = _sgu(x2, mix_norm, ready[0], a_v_gain3, a_v_bias3, a_w_s, a_b_s, ready[1],
                             ffn_weights, layer=i, mixer=j, tm=tm_sgu)
        else:
            x3, ready = _short_conv(x2.reshape(b, s, d), mix_norm, ready[0], b_conv_w, ready[1],
                                    ffn_weights, layer=i, mixer=j, tm=tm)
            x2 = x3.reshape(b * s, d)
        last = i == depth - 1
        x2, ready = _ffn(x2, ffn_norm, *ready, fg, () if last else mixer_weights(i + 1),
                         layer=i, next_mixer=(i + 1) // 2, tm=tm, final_norm=last)
    return x2.reshape(b, s, d)
```

```python
import functools
import math
from typing import NamedTuple

import jax
import jax.numpy as jnp
from jax import lax
from jax.experimental import pallas as pl
from jax.experimental.pallas import tpu as pltpu

EPS = 1e-6
CHUNK = 128
LANES = 128
SUBLANES = 8
BF16_ROWS = 16
MXU_N = 256
OUT_PROJ_BLOCKS = 8
DOWN_PROJ_BLOCKS = 11
VMEM_LIMIT_BYTES = 56 * 1024 * 1024

BF16 = jnp.bfloat16
F32 = jnp.float32


def _dot(a, b):
    return jnp.dot(a, b, preferred_element_type=F32)


def _rms_norm(x, g):
    y = x * lax.rsqrt(jnp.mean(x * x, axis=-1, keepdims=True) + EPS)
    return y * g


def _gelu_tanh(x):
    k = -2.0 * math.sqrt(2.0 / math.pi) * math.log2(math.e)
    return x / (1.0 + jnp.exp2(x * (k + (k * 0.044715) * (x * x))))


def _odd_tile_pitch(n):
    assert n % LANES == 0
    return n if (n // LANES) % 2 == 1 else n + LANES


def _resident(shape):
    return pl.BlockSpec(shape, lambda *_: (0,) * len(shape), pipeline_mode=pl.Buffered(1))


def _resident_layer(stacked, layer):
    rest = stacked.shape[1:]
    return pl.BlockSpec((None,) + rest, lambda *_: (layer,) + (0,) * len(rest),
                        pipeline_mode=pl.Buffered(1))


def _compiler_params(n_grid_axes):
    return pltpu.CompilerParams(dimension_semantics=("arbitrary",) * n_grid_axes,
                                vmem_limit_bytes=VMEM_LIMIT_BYTES)


class _Cast(NamedTuple):
    stacked: jax.Array
    layer: int
    rows: int

    @property
    def n_blocks(self):
        n_rows = self.stacked.shape[1]
        assert n_rows % self.rows == 0 and self.rows % BF16_ROWS == 0
        return n_rows // self.rows

    def out_shape(self):
        _, n_rows, n_cols = self.stacked.shape
        return jax.ShapeDtypeStruct((n_rows, _odd_tile_pitch(n_cols)), BF16)

    def specs(self, linear_step):
        n_cols = self.stacked.shape[2]
        block = lambda *g: jnp.minimum(linear_step(*g), self.n_blocks - 1)
        return (pl.BlockSpec((None, self.rows, n_cols), lambda *g: (self.layer, block(*g), 0)),
                pl.BlockSpec((self.rows, _odd_tile_pitch(n_cols)), lambda *g: (block(*g), 0)))


def _cast_jobs(weights, layer, n_steps):
    jobs = []
    for w in weights:
        n_rows = w.shape[1]
        rows = next(r for r in range(BF16_ROWS, n_rows + 1, BF16_ROWS)
                    if n_rows % r == 0 and n_rows // r <= n_steps)
        jobs.append(_Cast(w, layer, rows))
    return jobs


def _cast_blocks(src_refs, dst_refs):
    for src, dst in zip(src_refs, dst_refs):
        n = src.shape[1]
        dst[:, 0:n] = src[...].astype(BF16)
        if dst.shape[1] > n:
            dst[:, n:] = jnp.zeros((dst.shape[0], dst.shape[1] - n), BF16)


def _cast_kernel(*refs):
    n = len(refs) // 2
    _cast_blocks(refs[:n], refs[n:])


def _cast_call(weights, layer):
    n_steps = 8
    casts = _cast_jobs(weights, layer, n_steps)
    specs = [c.specs(lambda i: i) for c in casts]
    return pl.pallas_call(
        _cast_kernel,
        out_shape=[c.out_shape() for c in casts],
        grid=(n_steps,),
        in_specs=[s[0] for s in specs],
        out_specs=[s[1] for s in specs],
        compiler_params=_compiler_params(1),
        name="cast_weights",
    )(*[c.stacked for c in casts])


def _ffn_kernel(*refs, layer, n_casts, col_block, final_norm):
    x_ref, g_ref, wg_ref, wu_ref, wd_ref, fg_ref = refs[:6]
    cast_src = refs[6:6 + n_casts]
    o_ref = refs[6 + n_casts]
    cast_dst = refs[7 + n_casts:]

    x = x_ref[...]
    d = x.shape[1]
    h = _rms_norm(x, g_ref[layer:layer + 1, :]).astype(BF16)
    acc = x
    d_ff = wd_ref.shape[0]
    a_blocks = []
    for c in range(0, d_ff, col_block):
        g = _dot(h, wg_ref[:, c:c + col_block])
        u = _dot(h, wu_ref[:, c:c + col_block])
        a_blocks.append((g * (1.0 / (1.0 + jnp.exp(-g))) * u).astype(BF16))
        if len(a_blocks) == DOWN_PROJ_BLOCKS or c + col_block == d_ff:
            k0 = c + col_block - len(a_blocks) * col_block
            acc = acc + _dot(jnp.concatenate(a_blocks, axis=1), wd_ref[k0:c + col_block, 0:d])
            a_blocks = []
    if final_norm:
        acc = _rms_norm(acc, fg_ref[...])
    o_ref[...] = acc
    _cast_blocks(cast_src, cast_dst)


def _ffn(x2, norms, wg, wu, wd, fg, next_weights, *, layer, next_mixer, tm, final_norm):
    m, d = x2.shape
    n_steps = m // tm
    casts = _cast_jobs(next_weights, next_mixer, n_steps)
    cast_specs = [c.specs(lambda i: i) for c in casts]
    outs = pl.pallas_call(
        functools.partial(_ffn_kernel, layer=layer, n_casts=len(casts), col_block=MXU_N,
                          final_norm=final_norm),
        out_shape=[jax.ShapeDtypeStruct((m, d), F32)] + [c.out_shape() for c in casts],
        grid=(n_steps,),
        in_specs=[
            pl.BlockSpec((tm, d), lambda i: (i, 0)),
            _resident(norms.shape),
            _resident(wg.shape),
            _resident(wu.shape),
            _resident(wd.shape),
            _resident(fg.shape),
        ] + [s[0] for s in cast_specs],
        out_specs=[pl.BlockSpec((tm, d), lambda i: (i, 0))] + [s[1] for s in cast_specs],
        compiler_params=_compiler_params(1),
        name="swiglu_final" if final_norm else "swiglu",
    )(x2, norms, wg, wu, wd, fg, *[c.stacked for c in casts])
    return outs[0], outs[1:]


def _sgu_kernel(*refs, layer, n_casts, col_block):
    x_ref, xp_ref, g_ref, win_ref, vg_ref, vb_ref, ws_ref, bs_ref, wout_ref = refs[:9]
    cast_src = refs[9:9 + n_casts]
    o_ref = refs[9 + n_casts]
    cast_dst = refs[10 + n_casts:10 + 2 * n_casts]
    wtri_ref, gate_bias_ref, h_ref, v_ref, mu_ref, rstd_ref = refs[10 + 2 * n_casts:]

    tm, d = x_ref.shape
    e_a = wout_ref.shape[0]
    n_groups = ws_ref.shape[0]
    n_chunks = tm // CHUNK
    n_blocks = e_a // col_block
    groups_per_block = col_block // LANES
    assert n_blocks % OUT_PROJ_BLOCKS == 0

    @pl.when(pl.program_id(0) == 0)
    def _():
        row = lax.broadcasted_iota(jnp.int32, (CHUNK, CHUNK), 0)
        col = lax.broadcasted_iota(jnp.int32, (CHUNK, CHUNK), 1)
        for hg in range(n_groups):
            lanes = slice(hg * LANES, (hg + 1) * LANES)
            w_causal = jnp.where(col <= row, ws_ref[hg], 0.0)
            wtri_ref[hg] = w_causal.astype(BF16)
            b_col = jnp.sum(jnp.where(col == row, bs_ref[hg:hg + 1, :], 0.0), axis=1, keepdims=True)
            gate_bias_ref[:, lanes] = (jnp.sum(w_causal, axis=1, keepdims=True) * vb_ref[:, lanes]
                                       + b_col)
        h_ref[...] = jnp.zeros(h_ref.shape, BF16)
        v_ref[...] = jnp.zeros(v_ref.shape, F32)
        mu_ref[...] = jnp.zeros(mu_ref.shape, F32)
        rstd_ref[...] = jnp.zeros(rstd_ref.shape, F32)

    def v_proj(b):
        zv = _gelu_tanh(_dot(h, win_ref[:, e_a + b * col_block:e_a + (b + 1) * col_block]))
        v_ref[b] = zv
        return jnp.sum(zv, axis=-1, keepdims=True), jnp.sum(zv * zv, axis=-1, keepdims=True)

    def u_proj(b):
        return _dot(h_prev, win_ref[:, b * col_block:(b + 1) * col_block])

    def standardize_v(b):
        return ((v_ref[b] - mu) * rstd).astype(BF16)

    def spatial(b, vhat):
        out = []
        for gi in range(groups_per_block):
            lanes = slice(gi * LANES, (gi + 1) * LANES)
            rhs = jnp.concatenate(
                [vhat[n * CHUNK:(n + 1) * CHUNK, lanes] for n in range(n_chunks)], axis=1)
            out.append(_dot(wtri_ref[b * groups_per_block + gi], rhs))
        return out

    def gate(b, u, sv):
        y_cols = []
        for gi in range(groups_per_block):
            hg = b * groups_per_block + gi
            lanes = slice(gi * LANES, (gi + 1) * LANES)
            gain = vg_ref[:, hg * LANES:(hg + 1) * LANES]
            bias = gate_bias_ref[:, hg * LANES:(hg + 1) * LANES]
            y_cols.append(jnp.concatenate(
                [u[n * CHUNK:(n + 1) * CHUNK, lanes]
                 * (sv[gi][:, n * LANES:(n + 1) * LANES] * gain + bias) for n in range(n_chunks)],
                axis=0))
        return jnp.concatenate(y_cols, axis=1).astype(BF16)

    h_prev = h_ref[...]
    mu = mu_ref[...]
    rstd = rstd_ref[...]
    zu = u_proj(0)
    vhat = standardize_v(0)
    h = _rms_norm(x_ref[...], g_ref[layer:layer + 1, :]).astype(BF16)
    acc = xp_ref[...]
    s1 = jnp.zeros((tm, 1), F32)
    s2 = jnp.zeros((tm, 1), F32)
    y_blocks = []
    for b in range(n_blocks):
        p1, p2 = v_proj(b)
        s1 = s1 + p1
        s2 = s2 + p2
        sv = spatial(b, vhat)
        zu_next = u_proj(b + 1) if b + 1 < n_blocks else None
        y_blocks.append(gate(b, _gelu_tanh(zu), sv))
        if len(y_blocks) == OUT_PROJ_BLOCKS:
            k0 = (b + 1 - OUT_PROJ_BLOCKS) * col_block
            acc = acc + _dot(jnp.concatenate(y_blocks, axis=1),
                             wout_ref[k0:(b + 1) * col_block, 0:d])
            y_blocks = []
        if b + 1 < n_blocks:
            vhat = standardize_v(b + 1)
        zu = zu_next
    o_ref[...] = acc
    mu_new = s1 * (1.0 / e_a)
    mu_ref[...] = mu_new
    rstd_ref[...] = lax.rsqrt(s2 * (1.0 / e_a) - mu_new * mu_new + EPS)
    h_ref[...] = h
    _cast_blocks(cast_src, cast_dst)


def _sgu(x2, norms, win, vg, vb, ws, bs, wout, next_weights, *, layer, mixer, tm):
    m, d = x2.shape
    e_a = wout.shape[0]
    n_groups = ws.shape[1]
    n_tiles = m // tm
    casts = _cast_jobs(next_weights, layer, n_tiles)
    cast_specs = [c.specs(lambda i: i) for c in casts]
    cur = lambda i: (jnp.minimum(i, n_tiles - 1), 0)
    prev = lambda i: (jnp.maximum(i - 1, 0), 0)
    outs = pl.pallas_call(
        functools.partial(_sgu_kernel, layer=layer, n_casts=len(casts), col_block=MXU_N),
        out_shape=[jax.ShapeDtypeStruct((m, d), F32)] + [c.out_shape() for c in casts],
        grid=(n_tiles + 1,),
        in_specs=[
            pl.BlockSpec((tm, d), cur),
            pl.BlockSpec((tm, d), prev),
            _resident(norms.shape),
            _resident(win.shape),
            _resident_layer(vg, mixer),
            _resident_layer(vb, mixer),
            _resident_layer(ws, mixer),
            _resident_layer(bs, mixer),
            _resident(wout.shape),
        ] + [s[0] for s in cast_specs],
        out_specs=[pl.BlockSpec((tm, d), prev)] + [s[1] for s in cast_specs],
        scratch_shapes=[
            pltpu.VMEM((n_groups, CHUNK, CHUNK), BF16),
            pltpu.VMEM((CHUNK, e_a), F32),
            pltpu.VMEM((tm, d), BF16),
            pltpu.VMEM((e_a // MXU_N, tm, MXU_N), F32),
            pltpu.VMEM((tm, 1), F32),
            pltpu.VMEM((tm, 1), F32),
        ],
        compiler_params=_compiler_params(1),
        name="sgu_mixer",
    )(x2, x2, norms, win, vg, vb, ws, bs, wout, *[c.stacked for c in casts])
    return outs[0], outs[1:]


def _conv_kernel(*refs, layer, n_casts, halo):
    x_ref, g_ref, win_ref, cw_ref, wout_ref = refs[:5]
    cast_src = refs[5:5 + n_casts]
    o_ref = refs[5 + n_casts]
    cast_dst = refs[6 + n_casts:6 + 2 * n_casts]
    z_ref, = refs[6 + 2 * n_casts:]

    tm = x_ref.shape[1]
    e_b = wout_ref.shape[0]
    conv_w = cw_ref.shape[0]

    @pl.when(pl.program_id(1) == 0)
    def _():
        z_ref[0:halo, :] = jnp.zeros((halo, e_b), F32)

    @pl.when(pl.program_id(1) != 0)
    def _():
        z_ref[0:halo, :] = z_ref[tm:tm + halo, :]

    x = x_ref[0]
    h = _rms_norm(x, g_ref[layer:layer + 1, :]).astype(BF16)
    z_ref[halo:halo + tm, :] = (_dot(h, win_ref[:, e_b:2 * e_b])
                                * _dot(h, win_ref[:, 2 * e_b:3 * e_b]))
    conv = z_ref[halo:halo + tm, :] * cw_ref[conv_w - 1:conv_w, :]
    for k in range(conv_w - 1):
        back = conv_w - 1 - k
        conv = conv + z_ref[halo - back:halo - back + tm, :] * cw_ref[k:k + 1, :]
    y = (_dot(h, win_ref[:, 0:e_b]) * conv).astype(BF16)
    o_ref[0] = x + _dot(y, wout_ref[:, 0:x.shape[1]])
    _cast_blocks(cast_src, cast_dst)


def _short_conv(x3, norms, win, cw, wout, next_weights, *, layer, mixer, tm):
    b, s, d = x3.shape
    e_b = wout.shape[0]
    halo = SUBLANES
    assert cw.shape[1] - 1 <= halo
    tiles_per_seq = s // tm
    casts = _cast_jobs(next_weights, layer, b * tiles_per_seq)
    cast_specs = [c.specs(lambda i, j: i * tiles_per_seq + j) for c in casts]
    outs = pl.pallas_call(
        functools.partial(_conv_kernel, layer=layer, n_casts=len(casts), halo=halo),
        out_shape=[jax.ShapeDtypeStruct((b, s, d), F32)] + [c.out_shape() for c in casts],
        grid=(b, tiles_per_seq),
        in_specs=[
            pl.BlockSpec((1, tm, d), lambda i, j: (i, j, 0)),
            _resident(norms.shape),
            _resident(win.shape),
            _resident_layer(cw, mixer),
            _resident(wout.shape),
        ] + [s[0] for s in cast_specs],
        out_specs=[pl.BlockSpec((1, tm, d), lambda i, j: (i, j, 0))] + [s[1] for s in cast_specs],
        scratch_shapes=[pltpu.VMEM((halo + tm, e_b), F32)],
        compiler_params=_compiler_params(2),
        name="short_conv_mixer",
    )(x3, norms, win, cw, wout, *[c.stacked for c in casts])
    return outs[0], outs[1:]


def kernel(x, mix_norm, ffn_norm, a_w_in, a_v_gain, a_v_bias, a_w_s, a_b_s, a_w_out,
           b_w_in, b_conv_w, b_w_out, ffn_w_gate, ffn_w_up, ffn_w_down, final_norm):
    b, s, d = x.shape
    depth = mix_norm.shape[0]
    tm = 1024
    tm_sgu = 512
    assert s % tm == 0 and s % tm_sgu == 0 and tm_sgu % CHUNK == 0 and a_w_s.shape[-1] == CHUNK
    assert a_w_s.shape[1] * LANES == a_w_out.shape[1]

    fg = final_norm.reshape(1, d)
    a_v_gain3 = a_v_gain.reshape(a_v_gain.shape[0], 1, -1)
    a_v_bias3 = a_v_bias.reshape(a_v_bias.shape[0], 1, -1)
    ffn_weights = (ffn_w_gate, ffn_w_up, ffn_w_down)

    def mixer_weights(i):
        return (a_w_in, a_w_out) if i % 2 == 0 else (b_w_in, b_w_out)

    ready = _cast_call(mixer_weights(0), 0)
    x2 = x.reshape(b * s, d)
    for i in range(depth):
        j = i // 2
        if i % 2 == 0:
            x2, ready = _sgu(x2, mix_norm, ready[0], a_v_gain3, a_v_bias3, a_w_s, a_b_s, ready[1],
                             ffn_weights, layer=i, mixer=j, tm=tm_sgu)
        else:
            x3, ready = _short_conv(x2.reshape(b, s, d), mix_norm, ready[0], b_conv_w, ready[1],
                                    ffn_weights, layer=i, mixer=j, tm=tm)
            x2 = x3.reshape(b * s, d)
        last = i == depth - 1
        x2, ready = _ffn(x2, ffn_norm, *ready, fg, () if last else mixer_weights(i + 1),
                         layer=i, next_mixer=(i + 1) // 2, tm=tm, final_norm=last)
    return x2.reshape(b, s, d)
```

```python
import functools
import math
from typing import NamedTuple

import jax
import jax.numpy as jnp
from jax import lax
from jax.experimental import pallas as pl
from jax.experimental.pallas import tpu as pltpu

EPS = 1e-6
CHUNK = 128
LANES = 128
SUBLANES = 8
BF16_ROWS = 16
MXU_N = 256
OUT_PROJ_BLOCKS = 4
DOWN_PROJ_BLOCKS = 11
VMEM_LIMIT_BYTES = 56 * 1024 * 1024

BF16 = jnp.bfloat16
F32 = jnp.float32


def _dot(a, b):
    return jnp.dot(a, b, preferred_element_type=F32)


def _rms_norm(x, g):
    y = x * lax.rsqrt(jnp.mean(x * x, axis=-1, keepdims=True) + EPS)
    return y * g


def _gelu_tanh(x):
    k = -2.0 * math.sqrt(2.0 / math.pi) * math.log2(math.e)
    return x / (1.0 + jnp.exp2(x * (k + (k * 0.044715) * (x * x))))


def _odd_tile_pitch(n):
    assert n % LANES == 0
    return n if (n // LANES) % 2 == 1 else n + LANES


def _resident(shape):
    return pl.BlockSpec(shape, lambda *_: (0,) * len(shape), pipeline_mode=pl.Buffered(1))


def _resident_layer(stacked, layer):
    rest = stacked.shape[1:]
    return pl.BlockSpec((None,) + rest, lambda *_: (layer,) + (0,) * len(rest),
                        pipeline_mode=pl.Buffered(1))


def _compiler_params(n_grid_axes):
    return pltpu.CompilerParams(dimension_semantics=("arbitrary",) * n_grid_axes,
                                vmem_limit_bytes=VMEM_LIMIT_BYTES)


class _Cast(NamedTuple):
    stacked: jax.Array
    layer: int
    rows: int

    @property
    def n_blocks(self):
        n_rows = self.stacked.shape[1]
        assert n_rows % self.rows == 0 and self.rows % BF16_ROWS == 0
        return n_rows // self.rows

    def out_shape(self):
        _, n_rows, n_cols = self.stacked.shape
        return jax.ShapeDtypeStruct((n_rows, _odd_tile_pitch(n_cols)), BF16)

    def specs(self, linear_step):
        n_cols = self.stacked.shape[2]
        block = lambda *g: jnp.minimum(linear_step(*g), self.n_blocks - 1)
        return (pl.BlockSpec((None, self.rows, n_cols), lambda *g: (self.layer, block(*g), 0)),
                pl.BlockSpec((self.rows, _odd_tile_pitch(n_cols)), lambda *g: (block(*g), 0)))


def _cast_jobs(weights, layer, n_steps):
    jobs = []
    for w in weights:
        n_rows = w.shape[1]
        rows = next(r for r in range(BF16_ROWS, n_rows + 1, BF16_ROWS)
                    if n_rows % r == 0 and n_rows // r <= n_steps)
        jobs.append(_Cast(w, layer, rows))
    return jobs


def _cast_blocks(src_refs, dst_refs):
    for src, dst in zip(src_refs, dst_refs):
        n = src.shape[1]
        dst[:, 0:n] = src[...].astype(BF16)
        if dst.shape[1] > n:
            dst[:, n:] = jnp.zeros((dst.shape[0], dst.shape[1] - n), BF16)


def _cast_kernel(*refs):
    n = len(refs) // 2
    _cast_blocks(refs[:n], refs[n:])


def _cast_call(weights, layer):
    n_steps = 8
    casts = _cast_jobs(weights, layer, n_steps)
    specs = [c.specs(lambda i: i) for c in casts]
    return pl.pallas_call(
        _cast_kernel,
        out_shape=[c.out_shape() for c in casts],
        grid=(n_steps,),
        in_specs=[s[0] for s in specs],
        out_specs=[s[1] for s in specs],
        compiler_params=_compiler_params(1),
        name="cast_weights",
    )(*[c.stacked for c in casts])


def _ffn_kernel(*refs, layer, n_casts, col_block, final_norm):
    x_ref, g_ref, wg_ref, wu_ref, wd_ref, fg_ref = refs[:6]
    cast_src = refs[6:6 + n_casts]
    o_ref = refs[6 + n_casts]
    cast_dst = refs[7 + n_casts:]

    x = x_ref[...]
    d = x.shape[1]
    h = _rms_norm(x, g_ref[layer:layer + 1, :]).astype(BF16)
    acc = x
    d_ff = wd_ref.shape[0]
    a_blocks = []
    for c in range(0, d_ff, col_block):
        g = _dot(h, wg_ref[:, c:c + col_block])
        u = _dot(h, wu_ref[:, c:c + col_block])
        a_blocks.append((g * (1.0 / (1.0 + jnp.exp(-g))) * u).astype(BF16))
        if len(a_blocks) == DOWN_PROJ_BLOCKS or c + col_block == d_ff:
            k0 = c + col_block - len(a_blocks) * col_block
            acc = acc + _dot(jnp.concatenate(a_blocks, axis=1), wd_ref[k0:c + col_block, 0:d])
            a_blocks = []
    if final_norm:
        acc = _rms_norm(acc, fg_ref[...])
    o_ref[...] = acc
    _cast_blocks(cast_src, cast_dst)


def _ffn(x2, norms, wg, wu, wd, fg, next_weights, *, layer, next_mixer, tm, final_norm):
    m, d = x2.shape
    n_steps = m // tm
    casts = _cast_jobs(next_weights, next_mixer, n_steps)
    cast_specs = [c.specs(lambda i: i) for c in casts]
    outs = pl.pallas_call(
        functools.partial(_ffn_kernel, layer=layer, n_casts=len(casts), col_block=MXU_N,
                          final_norm=final_norm),
        out_shape=[jax.ShapeDtypeStruct((m, d), F32)] + [c.out_shape() for c in casts],
        grid=(n_steps,),
        in_specs=[
            pl.BlockSpec((tm, d), lambda i: (i, 0)),
            _resident(norms.shape),
            _resident(wg.shape),
            _resident(wu.shape),
            _resident(wd.shape),
            _resident(fg.shape),
        ] + [s[0] for s in cast_specs],
        out_specs=[pl.BlockSpec((tm, d), lambda i: (i, 0))] + [s[1] for s in cast_specs],
        compiler_params=_compiler_params(1),
        name="swiglu_final" if final_norm else "swiglu",
    )(x2, norms, wg, wu, wd, fg, *[c.stacked for c in casts])
    return outs[0], outs[1:]


def _sgu_kernel(*refs, layer, n_casts, col_block):
    x_ref, xp_ref, g_ref, win_ref, vg_ref, vb_ref, ws_ref, bs_ref, wout_ref = refs[:9]
    cast_src = refs[9:9 + n_casts]
    o_ref = refs[9 + n_casts]
    cast_dst = refs[10 + n_casts:10 + 2 * n_casts]
    wtri_ref, gate_bias_ref, h_ref, v_ref, mu_ref, rstd_ref = refs[10 + 2 * n_casts:]

    tm, d = x_ref.shape
    e_a = wout_ref.shape[0]
    n_groups = ws_ref.shape[0]
    n_chunks = tm // CHUNK
    n_blocks = e_a // col_block
    groups_per_block = col_block // LANES
    assert n_blocks % OUT_PROJ_BLOCKS == 0

    @pl.when(pl.program_id(0) == 0)
    def _():
        row = lax.broadcasted_iota(jnp.int32, (CHUNK, CHUNK), 0)
        col = lax.broadcasted_iota(jnp.int32, (CHUNK, CHUNK), 1)
        for hg in range(n_groups):
            lanes = slice(hg * LANES, (hg + 1) * LANES)
            w_causal = jnp.where(col <= row, ws_ref[hg], 0.0)
            wtri_ref[hg] = w_causal.astype(BF16)
            b_col = jnp.sum(jnp.where(col == row, bs_ref[hg:hg + 1, :], 0.0), axis=1, keepdims=True)
            gate_bias_ref[:, lanes] = (jnp.sum(w_causal, axis=1, keepdims=True) * vb_ref[:, lanes]
                                       + b_col)
        h_ref[...] = jnp.zeros(h_ref.shape, BF16)
        v_ref[...] = jnp.zeros(v_ref.shape, F32)
        mu_ref[...] = jnp.zeros(mu_ref.shape, F32)
        rstd_ref[...] = jnp.zeros(rstd_ref.shape, F32)

    def v_proj(b):
        zv = _gelu_tanh(_dot(h, win_ref[:, e_a + b * col_block:e_a + (b + 1) * col_block]))
        v_ref[b] = zv
        return jnp.sum(zv, axis=-1, keepdims=True), jnp.sum(zv * zv, axis=-1, keepdims=True)

    def u_proj(b):
        return _dot(h_prev, win_ref[:, b * col_block:(b + 1) * col_block])

    def standardize_v(b):
        return ((v_ref[b] - mu) * rstd).astype(BF16)

    def spatial(b, vhat):
        out = []
        for gi in range(groups_per_block):
            lanes = slice(gi * LANES, (gi + 1) * LANES)
            rhs = jnp.concatenate(
                [vhat[n * CHUNK:(n + 1) * CHUNK, lanes] for n in range(n_chunks)], axis=1)
            out.append(_dot(wtri_ref[b * groups_per_block + gi], rhs))
        return out

    def gate(b, u, sv):
        y_cols = []
        for gi in range(groups_per_block):
            hg = b * groups_per_block + gi
            lanes = slice(gi * LANES, (gi + 1) * LANES)
            gain = vg_ref[:, hg * LANES:(hg + 1) * LANES]
            bias = gate_bias_ref[:, hg * LANES:(hg + 1) * LANES]
            y_cols.append(jnp.concatenate(
                [u[n * CHUNK:(n + 1) * CHUNK, lanes]
                 * (sv[gi][:, n * LANES:(n + 1) * LANES] * gain + bias) for n in range(n_chunks)],
                axis=0))
        return jnp.concatenate(y_cols, axis=1).astype(BF16)

    h_prev = h_ref[...]
    mu = mu_ref[...]
    rstd = rstd_ref[...]
    zu = u_proj(0)
    vhat = standardize_v(0)
    h = _rms_norm(x_ref[...], g_ref[layer:layer + 1, :]).astype(BF16)
    acc = xp_ref[...]
    s1 = jnp.zeros((tm, 1), F32)
    s2 = jnp.zeros((tm, 1), F32)
    y_blocks = []
    for b in range(n_blocks):
        p1, p2 = v_proj(b)
        s1 = s1 + p1
        s2 = s2 + p2
        sv = spatial(b, vhat)
        zu_next = u_proj(b + 1) if b + 1 < n_blocks else None
        y_blocks.append(gate(b, _gelu_tanh(zu), sv))
        if len(y_blocks) == OUT_PROJ_BLOCKS:
            k0 = (b + 1 - OUT_PROJ_BLOCKS) * col_block
            acc = acc + _dot(jnp.concatenate(y_blocks, axis=1),
                             wout_ref[k0:(b + 1) * col_block, 0:d])
            y_blocks = []
        if b + 1 < n_blocks:
            vhat = standardize_v(b + 1)
        zu = zu_next
    o_ref[...] = acc
    mu_new = s1 * (1.0 / e_a)
    mu_ref[...] = mu_new
    rstd_ref[...] = lax.rsqrt(s2 * (1.0 / e_a) - mu_new * mu_new + EPS)
    h_ref[...] = h
    _cast_blocks(cast_src, cast_dst)


def _sgu(x2, norms, win, vg, vb, ws, bs, wout, next_weights, *, layer, mixer, tm):
    m, d = x2.shape
    e_a = wout.shape[0]
    n_groups = ws.shape[1]
    n_tiles = m // tm
    casts = _cast_jobs(next_weights, layer, n_tiles)
    cast_specs = [c.specs(lambda i: i) for c in casts]
    cur = lambda i: (jnp.minimum(i, n_tiles - 1), 0)
    prev = lambda i: (jnp.maximum(i - 1, 0), 0)
    outs = pl.pallas_call(
        functools.partial(_sgu_kernel, layer=layer, n_casts=len(casts), col_block=MXU_N),
        out_shape=[jax.ShapeDtypeStruct((m, d), F32)] + [c.out_shape() for c in casts],
        grid=(n_tiles + 1,),
        in_specs=[
            pl.BlockSpec((tm, d), cur),
            pl.BlockSpec((tm, d), prev),
            _resident(norms.shape),
            _resident(win.shape),
            _resident_layer(vg, mixer),
            _resident_layer(vb, mixer),
            _resident_layer(ws, mixer),
            _resident_layer(bs, mixer),
            _resident(wout.shape),
        ] + [s[0] for s in cast_specs],
        out_specs=[pl.BlockSpec((tm, d), prev)] + [s[1] for s in cast_specs],
        scratch_shapes=[
            pltpu.VMEM((n_groups, CHUNK, CHUNK), BF16),
            pltpu.VMEM((CHUNK, e_a), F32),
            pltpu.VMEM((tm, d), BF16),
            pltpu.VMEM((e_a // MXU_N, tm, MXU_N), F32),
            pltpu.VMEM((tm, 1), F32),
            pltpu.VMEM((tm, 1), F32),
        ],
        compiler_params=_compiler_params(1),
        name="sgu_mixer",
    )(x2, x2, norms, win, vg, vb, ws, bs, wout, *[c.stacked for c in casts])
    return outs[0], outs[1:]


def _conv_kernel(*refs, layer, n_casts, halo):
    x_ref, g_ref, win_ref, cw_ref, wout_ref = refs[:5]
    cast_src = refs[5:5 + n_casts]
    o_ref = refs[5 + n_casts]
    cast_dst = refs[6 + n_casts:6 + 2 * n_casts]
    z_ref, = refs[6 + 2 * n_casts:]

    tm = x_ref.shape[1]
    e_b = wout_ref.shape[0]
    conv_w = cw_ref.shape[0]
    n_blocks, _, col_block = z_ref.shape

    @pl.when(pl.program_id(1) == 0)
    def _():
        z_ref[:, 0:halo, :] = jnp.zeros((n_blocks, halo, col_block), F32)

    @pl.when(pl.program_id(1) != 0)
    def _():
        z_ref[:, 0:halo, :] = z_ref[:, tm:tm + halo, :]

    x = x_ref[0]
    h = _rms_norm(x, g_ref[layer:layer + 1, :]).astype(BF16)
    y_blocks = []
    for j in range(n_blocks):
        c = j * col_block
        cols = slice(c, c + col_block)
        z_ref[j, halo:halo + tm, :] = (_dot(h, win_ref[:, e_b + c:e_b + c + col_block])
                                       * _dot(h, win_ref[:, 2 * e_b + c:2 * e_b + c + col_block]))
        b_gate = _dot(h, win_ref[:, cols])
        conv = z_ref[j, halo:halo + tm, :] * cw_ref[conv_w - 1:conv_w, cols]
        for k in range(conv_w - 1):
            back = conv_w - 1 - k
            conv = conv + z_ref[j, halo - back:halo - back + tm, :] * cw_ref[k:k + 1, cols]
        y_blocks.append((b_gate * conv).astype(BF16))
    o_ref[0] = x + _dot(jnp.concatenate(y_blocks, axis=1), wout_ref[:, 0:x.shape[1]])
    _cast_blocks(cast_src, cast_dst)


def _short_conv(x3, norms, win, cw, wout, next_weights, *, layer, mixer, tm):
    b, s, d = x3.shape
    e_b = wout.shape[0]
    halo = SUBLANES
    assert cw.shape[1] - 1 <= halo
    tiles_per_seq = s // tm
    casts = _cast_jobs(next_weights, layer, b * tiles_per_seq)
    cast_specs = [c.specs(lambda i, j: i * tiles_per_seq + j) for c in casts]
    outs = pl.pallas_call(
        functools.partial(_conv_kernel, layer=layer, n_casts=len(casts), halo=halo),
        out_shape=[jax.ShapeDtypeStruct((b, s, d), F32)] + [c.out_shape() for c in casts],
        grid=(b, tiles_per_seq),
        in_specs=[
            pl.BlockSpec((1, tm, d), lambda i, j: (i, j, 0)),
            _resident(norms.shape),
            _resident(win.shape),
            _resident_layer(cw, mixer),
            _resident(wout.shape),
        ] + [s[0] for s in cast_specs],
        out_specs=[pl.BlockSpec((1, tm, d), lambda i, j: (i, j, 0))] + [s[1] for s in cast_specs],
        scratch_shapes=[pltpu.VMEM((e_b // MXU_N, halo + tm, MXU_N), F32)],
        compiler_params=_compiler_params(2),
        name="short_conv_mixer",
    )(x3, norms, win, cw, wout, *[c.stacked for c in casts])
    return outs[0], outs[1:]


def kernel(x, mix_norm, ffn_norm, a_w_in, a_v_gain, a_v_bias, a_w_s, a_b_s, a_w_out,
           b_w_in, b_conv_w, b_w_out, ffn_w_gate, ffn_w_up, ffn_w_down, final_norm):
    b, s, d = x.shape
    depth = mix_norm.shape[0]
    tm = 1024
    tm_sgu = 512
    assert s % tm == 0 and s % tm_sgu == 0 and tm_sgu % CHUNK == 0 and a_w_s.shape[-1] == CHUNK
    assert a_w_s.shape[1] * LANES == a_w_out.shape[1]

    fg = final_norm.reshape(1, d)
    a_v_gain3 = a_v_gain.reshape(a_v_gain.shape[0], 1, -1)
    a_v_bias3 = a_v_bias.reshape(a_v_bias.shape[0], 1, -1)
    ffn_weights = (ffn_w_gate, ffn_w_up, ffn_w_down)

    def mixer_weights(i):
        return (a_w_in, a_w_out) if i % 2 == 0 else (b_w_in, b_w_out)

    ready = _cast_call(mixer_weights(0), 0)
    x2 = x.reshape(b * s, d)
    for i in range(depth):
        j = i // 2
        if i % 2 == 0:
            x2, ready = _sgu(x2, mix_norm, ready[0], a_v_gain3, a_v_bias3, a_w_s, a_b_s, ready[1],
                             ffn_weights, layer=i, mixer=j, tm=tm_sgu)
        else:
            x3, ready = _short_conv(x2.reshape(b, s, d), mix_norm, ready[0], b_conv_w, ready[1],
                                    ffn_weights, layer=i, mixer=j, tm=tm)
            x2 = x3.reshape(b * s, d)
        last = i == depth - 1
        x2, ready = _ffn(x2, ffn_norm, *ready, fg, () if last else mixer_weights(i + 1),
                         layer=i, next_mixer=(i + 1) // 2, tm=tm, final_norm=last)
    return x2.reshape(b, s, d)
```

```python
import functools
import math
from typing import NamedTuple

import jax
import jax.numpy as jnp
from jax import lax
from jax.experimental import pallas as pl
from jax.experimental.pallas import tpu as pltpu

EPS = 1e-6
CHUNK = 128
LANES = 128
SUBLANES = 8
BF16_ROWS = 16
MXU_N = 256
OUT_PROJ_BLOCKS = 4
DOWN_PROJ_BLOCKS = 11
VMEM_LIMIT_BYTES = 56 * 1024 * 1024

BF16 = jnp.bfloat16
F32 = jnp.float32


def _dot(a, b):
    return jnp.dot(a, b, preferred_element_type=F32)


def _rms_norm(x, g):
    y = x * lax.rsqrt(jnp.mean(x * x, axis=-1, keepdims=True) + EPS)
    return y * g


def _gelu_tanh(x):
    k = -2.0 * math.sqrt(2.0 / math.pi) * math.log2(math.e)
    return x / (1.0 + jnp.exp2(x * (k + (k * 0.044715) * (x * x))))


def _odd_tile_pitch(n):
    assert n % LANES == 0
    return n if (n // LANES) % 2 == 1 else n + LANES


def _resident(shape):
    return pl.BlockSpec(shape, lambda *_: (0,) * len(shape), pipeline_mode=pl.Buffered(1))


def _resident_layer(stacked, layer):
    rest = stacked.shape[1:]
    return pl.BlockSpec((None,) + rest, lambda *_: (layer,) + (0,) * len(rest),
                        pipeline_mode=pl.Buffered(1))


def _compiler_params(n_grid_axes):
    return pltpu.CompilerParams(dimension_semantics=("arbitrary",) * n_grid_axes,
                                vmem_limit_bytes=VMEM_LIMIT_BYTES)


class _Cast(NamedTuple):
    stacked: jax.Array
    layer: int
    rows: int

    @property
    def n_blocks(self):
        n_rows = self.stacked.shape[1]
        assert n_rows % self.rows == 0 and self.rows % BF16_ROWS == 0
        return n_rows // self.rows

    def out_shape(self):
        _, n_rows, n_cols = self.stacked.shape
        return jax.ShapeDtypeStruct((n_rows, _odd_tile_pitch(n_cols)), BF16)

    def specs(self, linear_step):
        n_cols = self.stacked.shape[2]
        block = lambda *g: jnp.minimum(linear_step(*g), self.n_blocks - 1)
        return (pl.BlockSpec((None, self.rows, n_cols), lambda *g: (self.layer, block(*g), 0)),
                pl.BlockSpec((self.rows, _odd_tile_pitch(n_cols)), lambda *g: (block(*g), 0)))


def _cast_jobs(weights, layer, n_steps):
    jobs = []
    for w in weights:
        n_rows = w.shape[1]
        rows = next(r for r in range(BF16_ROWS, n_rows + 1, BF16_ROWS)
                    if n_rows % r == 0 and n_rows // r <= n_steps)
        jobs.append(_Cast(w, layer, rows))
    return jobs


def _cast_blocks(src_refs, dst_refs):
    for src, dst in zip(src_refs, dst_refs):
        n = src.shape[1]
        dst[:, 0:n] = src[...].astype(BF16)
        if dst.shape[1] > n:
            dst[:, n:] = jnp.zeros((dst.shape[0], dst.shape[1] - n), BF16)


def _cast_kernel(*refs):
    n = len(refs) // 2
    _cast_blocks(refs[:n], refs[n:])


def _cast_call(weights, layer):
    n_steps = 8
    casts = _cast_jobs(weights, layer, n_steps)
    specs = [c.specs(lambda i: i) for c in casts]
    return pl.pallas_call(
        _cast_kernel,
        out_shape=[c.out_shape() for c in casts],
        grid=(n_steps,),
        in_specs=[s[0] for s in specs],
        out_specs=[s[1] for s in specs],
        compiler_params=_compiler_params(1),
        name="cast_weights",
    )(*[c.stacked for c in casts])


def _ffn_kernel(*refs, layer, n_casts, col_block, final_norm):
    x_ref, g_ref, wg_ref, wu_ref, wd_ref, fg_ref = refs[:6]
    cast_src = refs[6:6 + n_casts]
    o_ref = refs[6 + n_casts]
    cast_dst = refs[7 + n_casts:]

    x = x_ref[...]
    d = x.shape[1]
    h = _rms_norm(x, g_ref[layer:layer + 1, :]).astype(BF16)
    acc = x
    d_ff = wd_ref.shape[0]
    a_blocks = []
    for c in range(0, d_ff, col_block):
        g = _dot(h, wg_ref[:, c:c + col_block])
        u = _dot(h, wu_ref[:, c:c + col_block])
        a_blocks.append((g * (1.0 / (1.0 + jnp.exp(-g))) * u).astype(BF16))
        if len(a_blocks) == DOWN_PROJ_BLOCKS or c + col_block == d_ff:
            k0 = c + col_block - len(a_blocks) * col_block
            acc = acc + _dot(jnp.concatenate(a_blocks, axis=1), wd_ref[k0:c + col_block, 0:d])
            a_blocks = []
    if final_norm:
        acc = _rms_norm(acc, fg_ref[...])
    o_ref[...] = acc
    _cast_blocks(cast_src, cast_dst)


def _ffn(x2, norms, wg, wu, wd, fg, next_weights, *, layer, next_mixer, tm, final_norm):
    m, d = x2.shape
    n_steps = m // tm
    casts = _cast_jobs(next_weights, next_mixer, n_steps)
    cast_specs = [c.specs(lambda i: i) for c in casts]
    outs = pl.pallas_call(
        functools.partial(_ffn_kernel, layer=layer, n_casts=len(casts), col_block=MXU_N,
                          final_norm=final_norm),
        out_shape=[jax.ShapeDtypeStruct((m, d), F32)] + [c.out_shape() for c in casts],
        grid=(n_steps,),
        in_specs=[
            pl.BlockSpec((tm, d), lambda i: (i, 0)),
            _resident(norms.shape),
            _resident(wg.shape),
            _resident(wu.shape),
            _resident(wd.shape),
            _resident(fg.shape),
        ] + [s[0] for s in cast_specs],
        out_specs=[pl.BlockSpec((tm, d), lambda i: (i, 0))] + [s[1] for s in cast_specs],
        compiler_params=_compiler_params(1),
        name="swiglu_final" if final_norm else "swiglu",
    )(x2, norms, wg, wu, wd, fg, *[c.stacked for c in casts])
    return outs[0], outs[1:]


def _sgu_kernel(*refs, layer, n_casts, col_block):
    x_ref, xp_ref, g_ref, win_ref, vg_ref, vb_ref, ws_ref, bs_ref, wout_ref = refs[:9]
    cast_src = refs[9:9 + n_casts]
    o_ref = refs[9 + n_casts]
    cast_dst = refs[10 + n_casts:10 + 2 * n_casts]
    wtri_ref, gate_bias_ref, h_ref, v_ref, sv_ref = refs[10 + 2 * n_casts:]

    tm, d = x_ref.shape
    e_a = wout_ref.shape[0]
    n_groups = ws_ref.shape[0]
    n_chunks = tm // CHUNK
    n_blocks = e_a // col_block
    groups_per_block = col_block // LANES
    assert n_blocks % (2 * OUT_PROJ_BLOCKS) == 0

    @pl.when(pl.program_id(0) == 0)
    def _():
        row = lax.broadcasted_iota(jnp.int32, (CHUNK, CHUNK), 0)
        col = lax.broadcasted_iota(jnp.int32, (CHUNK, CHUNK), 1)
        for hg in range(n_groups):
            lanes = slice(hg * LANES, (hg + 1) * LANES)
            w_causal = jnp.where(col <= row, ws_ref[hg], 0.0)
            wtri_ref[hg] = w_causal.astype(BF16)
            b_col = jnp.sum(jnp.where(col == row, bs_ref[hg:hg + 1, :], 0.0), axis=1, keepdims=True)
            gate_bias_ref[:, lanes] = (jnp.sum(w_causal, axis=1, keepdims=True) * vb_ref[:, lanes]
                                       + b_col)
        h_ref[...] = jnp.zeros(h_ref.shape, BF16)
        sv_ref[...] = jnp.zeros(sv_ref.shape, F32)

    def v_proj(b):
        v_ref[b] = _gelu_tanh(_dot(h, win_ref[:, e_a + b * col_block:e_a + (b + 1) * col_block]))

    def spatial(b):
        vhat = ((v_ref[b] - mu) * rstd).astype(BF16)
        for gi in range(groups_per_block):
            lanes = slice(gi * LANES, (gi + 1) * LANES)
            rhs = jnp.concatenate(
                [vhat[n * CHUNK:(n + 1) * CHUNK, lanes] for n in range(n_chunks)], axis=1)
            sv_ref[b * groups_per_block + gi] = _dot(wtri_ref[b * groups_per_block + gi], rhs)

    def u_proj(b):
        return _dot(h_prev, win_ref[:, b * col_block:(b + 1) * col_block])

    def gate(b, u):
        y_cols = []
        for gi in range(groups_per_block):
            hg = b * groups_per_block + gi
            lanes = slice(gi * LANES, (gi + 1) * LANES)
            gain = vg_ref[:, hg * LANES:(hg + 1) * LANES]
            bias = gate_bias_ref[:, hg * LANES:(hg + 1) * LANES]
            y_cols.append(jnp.concatenate(
                [u[n * CHUNK:(n + 1) * CHUNK, lanes]
                 * (sv_ref[hg, :, n * LANES:(n + 1) * LANES] * gain + bias)
                 for n in range(n_chunks)], axis=0))
        return jnp.concatenate(y_cols, axis=1).astype(BF16)

    u_state = dict(b=0, zu=None, y_blocks=[], acc=None)

    def u_stream_block():
        b = u_state["b"]
        zu_next = u_proj(b + 1) if b + 1 < n_blocks else None
        u_state["y_blocks"].append(gate(b, _gelu_tanh(u_state["zu"])))
        u_state["zu"] = zu_next
        u_state["b"] = b + 1

    def u_stream_out_proj():
        b = u_state["b"]
        assert len(u_state["y_blocks"]) == OUT_PROJ_BLOCKS
        k0 = (b - OUT_PROJ_BLOCKS) * col_block
        u_state["acc"] = u_state["acc"] + _dot(
            jnp.concatenate(u_state["y_blocks"], axis=1), wout_ref[k0:b * col_block, 0:d])
        u_state["y_blocks"] = []

    h_prev = h_ref[...]
    u_state["zu"] = u_proj(0)
    h = _rms_norm(x_ref[...], g_ref[layer:layer + 1, :]).astype(BF16)
    u_state["acc"] = xp_ref[...]
    for b in range(n_blocks):
        v_proj(b)
        if b % 2 == 1:
            u_stream_block()
    u_stream_out_proj()
    sum_v = jnp.zeros((tm, LANES), F32)
    sum_vv = jnp.zeros((tm, LANES), F32)
    for b in range(n_blocks):
        for c in range(0, col_block, LANES):
            t = v_ref[b, :, c:c + LANES]
            sum_v = sum_v + t
            sum_vv = sum_vv + t * t
    mu = jnp.sum(sum_v, axis=-1, keepdims=True) * (1.0 / e_a)
    rstd = lax.rsqrt(jnp.sum(sum_vv, axis=-1, keepdims=True) * (1.0 / e_a) - mu * mu + EPS)
    for b in range(n_blocks):
        if b % 2 == 0:
            u_stream_block()
        spatial(b)
    u_stream_out_proj()
    o_ref[...] = u_state["acc"]
    h_ref[...] = h
    _cast_blocks(cast_src, cast_dst)


def _sgu(x2, norms, win, vg, vb, ws, bs, wout, next_weights, *, layer, mixer, tm):
    m, d = x2.shape
    e_a = wout.shape[0]
    n_groups = ws.shape[1]
    n_tiles = m // tm
    casts = _cast_jobs(next_weights, layer, n_tiles)
    cast_specs = [c.specs(lambda i: i) for c in casts]
    cur = lambda i: (jnp.minimum(i, n_tiles - 1), 0)
    prev = lambda i: (jnp.maximum(i - 1, 0), 0)
    outs = pl.pallas_call(
        functools.partial(_sgu_kernel, layer=layer, n_casts=len(casts), col_block=MXU_N),
        out_shape=[jax.ShapeDtypeStruct((m, d), F32)] + [c.out_shape() for c in casts],
        grid=(n_tiles + 1,),
        in_specs=[
            pl.BlockSpec((tm, d), cur),
            pl.BlockSpec((tm, d), prev),
            _resident(norms.shape),
            _resident(win.shape),
            _resident_layer(vg, mixer),
            _resident_layer(vb, mixer),
            _resident_layer(ws, mixer),
            _resident_layer(bs, mixer),
            _resident(wout.shape),
        ] + [s[0] for s in cast_specs],
        out_specs=[pl.BlockSpec((tm, d), prev)] + [s[1] for s in cast_specs],
        scratch_shapes=[
            pltpu.VMEM((n_groups, CHUNK, CHUNK), BF16),
            pltpu.VMEM((CHUNK, e_a), F32),
            pltpu.VMEM((tm, d), BF16),
            pltpu.VMEM((e_a // MXU_N, tm, MXU_N), F32),
            pltpu.VMEM((n_groups, CHUNK, tm), F32),
        ],
        compiler_params=_compiler_params(1),
        name="sgu_mixer",
    )(x2, x2, norms, win, vg, vb, ws, bs, wout, *[c.stacked for c in casts])
    return outs[0], outs[1:]


def _conv_kernel(*refs, layer, n_casts, halo):
    x_ref, g_ref, win_ref, cw_ref, wout_ref = refs[:5]
    cast_src = refs[5:5 + n_casts]
    o_ref = refs[5 + n_casts]
    cast_dst = refs[6 + n_casts:6 + 2 * n_casts]
    z_ref, = refs[6 + 2 * n_casts:]

    tm = x_ref.shape[1]
    e_b = wout_ref.shape[0]
    conv_w = cw_ref.shape[0]
    n_blocks, _, col_block = z_ref.shape

    @pl.when(pl.program_id(1) == 0)
    def _():
        z_ref[:, 0:halo, :] = jnp.zeros((n_blocks, halo, col_block), F32)

    @pl.when(pl.program_id(1) != 0)
    def _():
        z_ref[:, 0:halo, :] = z_ref[:, tm:tm + halo, :]

    x = x_ref[0]
    h = _rms_norm(x, g_ref[layer:layer + 1, :]).astype(BF16)
    y_blocks = []
    for j in range(n_blocks):
        c = j * col_block
        cols = slice(c, c + col_block)
        z_ref[j, halo:halo + tm, :] = (_dot(h, win_ref[:, e_b + c:e_b + c + col_block])
                                       * _dot(h, win_ref[:, 2 * e_b + c:2 * e_b + c + col_block]))
        b_gate = _dot(h, win_ref[:, cols])
        conv = z_ref[j, halo:halo + tm, :] * cw_ref[conv_w - 1:conv_w, cols]
        for k in range(conv_w - 1):
            back = conv_w - 1 - k
            conv = conv + z_ref[j, halo - back:halo - back + tm, :] * cw_ref[k:k + 1, cols]
        y_blocks.append((b_gate * conv).astype(BF16))
    o_ref[0] = x + _dot(jnp.concatenate(y_blocks, axis=1), wout_ref[:, 0:x.shape[1]])
    _cast_blocks(cast_src, cast_dst)


def _short_conv(x3, norms, win, cw, wout, next_weights, *, layer, mixer, tm):
    b, s, d = x3.shape
    e_b = wout.shape[0]
    halo = SUBLANES
    assert cw.shape[1] - 1 <= halo
    tiles_per_seq = s // tm
    casts = _cast_jobs(next_weights, layer, b * tiles_per_seq)
    cast_specs = [c.specs(lambda i, j: i * tiles_per_seq + j) for c in casts]
    outs = pl.pallas_call(
        functools.partial(_conv_kernel, layer=layer, n_casts=len(casts), halo=halo),
        out_shape=[jax.ShapeDtypeStruct((b, s, d), F32)] + [c.out_shape() for c in casts],
        grid=(b, tiles_per_seq),
        in_specs=[
            pl.BlockSpec((1, tm, d), lambda i, j: (i, j, 0)),
            _resident(norms.shape),
            _resident(win.shape),
            _resident_layer(cw, mixer),
            _resident(wout.shape),
        ] + [s[0] for s in cast_specs],
        out_specs=[pl.BlockSpec((1, tm, d), lambda i, j: (i, j, 0))] + [s[1] for s in cast_specs],
        scratch_shapes=[pltpu.VMEM((e_b // MXU_N, halo + tm, MXU_N), F32)],
        compiler_params=_compiler_params(2),
        name="short_conv_mixer",
    )(x3, norms, win, cw, wout, *[c.stacked for c in casts])
    return outs[0], outs[1:]


def kernel(x, mix_norm, ffn_norm, a_w_in, a_v_gain, a_v_bias, a_w_s, a_b_s, a_w_out,
           b_w_in, b_conv_w, b_w_out, ffn_w_gate, ffn_w_up, ffn_w_down, final_norm):
    b, s, d = x.shape
    depth = mix_norm.shape[0]
    tm = 1024
    tm_sgu = 512
    assert s % tm == 0 and s % tm_sgu == 0 and tm_sgu % CHUNK == 0 and a_w_s.shape[-1] == CHUNK
    assert a_w_s.shape[1] * LANES == a_w_out.shape[1]

    fg = final_norm.reshape(1, d)
    a_v_gain3 = a_v_gain.reshape(a_v_gain.shape[0], 1, -1)
    a_v_bias3 = a_v_bias.reshape(a_v_bias.shape[0], 1, -1)
    ffn_weights = (ffn_w_gate, ffn_w_up, ffn_w_down)

    def mixer_weights(i):
        return (a_w_in, a_w_out) if i % 2 == 0 else (b_w_in, b_w_out)

    ready = _cast_call(mixer_weights(0), 0)
    x2 = x.reshape(b * s, d)
    for i in range(depth):
        j = i // 2
        if i % 2 == 0:
            x2, ready = _sgu(x2, mix_norm, ready[0], a_v_gain3, a_v_bias3, a_w_s, a_b_s, ready[1],
                             ffn_weights, layer=i, mixer=j, tm=tm_sgu)
        else:
            x3, ready = _short_conv(x2.reshape(b, s, d), mix_norm, ready[0], b_conv_w, ready[1],
                                    ffn_weights, layer=i, mixer=j, tm=tm)
            x2 = x3.reshape(b * s, d)
        last = i == depth - 1
        x2, ready = _ffn(x2, ffn_norm, *ready, fg, () if last else mixer_weights(i + 1),
                         layer=i, next_mixer=(i + 1) // 2, tm=tm, final_norm=last)
    return x2.reshape(b, s, d)
```

```python
import functools
import math
from typing import NamedTuple

import jax
import jax.numpy as jnp
from jax import lax
from jax.experimental import pallas as pl
from jax.experimental.pallas import tpu as pltpu

EPS = 1e-6
CHUNK = 128
LANES = 128
SUBLANES = 8
BF16_ROWS = 16
MXU_N = 256
OUT_PROJ_BLOCKS = 4
DOWN_PROJ_BLOCKS = 11
VMEM_LIMIT_BYTES = 56 * 1024 * 1024

BF16 = jnp.bfloat16
F32 = jnp.float32


def _dot(a, b):
    return jnp.dot(a, b, preferred_element_type=F32)


def _rms_norm(x, g):
    y = x * lax.rsqrt(jnp.mean(x * x, axis=-1, keepdims=True) + EPS)
    return y * g


def _gelu_tanh(x):
    k = -2.0 * math.sqrt(2.0 / math.pi) * math.log2(math.e)
    return x / (1.0 + jnp.exp2(x * (k + (k * 0.044715) * (x * x))))


def _odd_tile_pitch(n):
    assert n % LANES == 0
    return n if (n // LANES) % 2 == 1 else n + LANES


def _resident(shape):
    return pl.BlockSpec(shape, lambda *_: (0,) * len(shape), pipeline_mode=pl.Buffered(1))


def _resident_layer(stacked, layer):
    rest = stacked.shape[1:]
    return pl.BlockSpec((None,) + rest, lambda *_: (layer,) + (0,) * len(rest),
                        pipeline_mode=pl.Buffered(1))


def _compiler_params(n_grid_axes):
    return pltpu.CompilerParams(dimension_semantics=("arbitrary",) * n_grid_axes,
                                vmem_limit_bytes=VMEM_LIMIT_BYTES)


class _Cast(NamedTuple):
    stacked: jax.Array
    layer: int
    rows: int

    @property
    def n_blocks(self):
        n_rows = self.stacked.shape[1]
        assert n_rows % self.rows == 0 and self.rows % BF16_ROWS == 0
        return n_rows // self.rows

    def out_shape(self):
        _, n_rows, n_cols = self.stacked.shape
        return jax.ShapeDtypeStruct((n_rows, _odd_tile_pitch(n_cols)), BF16)

    def specs(self, linear_step):
        n_cols = self.stacked.shape[2]
        block = lambda *g: jnp.minimum(linear_step(*g), self.n_blocks - 1)
        return (pl.BlockSpec((None, self.rows, n_cols), lambda *g: (self.layer, block(*g), 0)),
                pl.BlockSpec((self.rows, _odd_tile_pitch(n_cols)), lambda *g: (block(*g), 0)))


def _cast_jobs(weights, layer, n_steps):
    jobs = []
    for w in weights:
        n_rows = w.shape[1]
        rows = next(r for r in range(BF16_ROWS, n_rows + 1, BF16_ROWS)
                    if n_rows % r == 0 and n_rows // r <= n_steps)
        jobs.append(_Cast(w, layer, rows))
    return jobs


def _cast_blocks(src_refs, dst_refs):
    for src, dst in zip(src_refs, dst_refs):
        n = src.shape[1]
        dst[:, 0:n] = src[...].astype(BF16)
        if dst.shape[1] > n:
            dst[:, n:] = jnp.zeros((dst.shape[0], dst.shape[1] - n), BF16)


def _cast_kernel(*refs):
    n = len(refs) // 2
    _cast_blocks(refs[:n], refs[n:])


def _cast_call(weights, layer):
    n_steps = 8
    casts = _cast_jobs(weights, layer, n_steps)
    specs = [c.specs(lambda i: i) for c in casts]
    return pl.pallas_call(
        _cast_kernel,
        out_shape=[c.out_shape() for c in casts],
        grid=(n_steps,),
        in_specs=[s[0] for s in specs],
        out_specs=[s[1] for s in specs],
        compiler_params=_compiler_params(1),
        name="cast_weights",
    )(*[c.stacked for c in casts])


def _ffn_kernel(*refs, layer, n_casts, col_block, final_norm):
    x_ref, g_ref, wg_ref, wu_ref, wd_ref, fg_ref = refs[:6]
    cast_src = refs[6:6 + n_casts]
    o_ref = refs[6 + n_casts]
    cast_dst = refs[7 + n_casts:]

    x = x_ref[...]
    d = x.shape[1]
    h = _rms_norm(x, g_ref[layer:layer + 1, :]).astype(BF16)
    acc = x
    d_ff = wd_ref.shape[0]
    a_blocks = []
    for c in range(0, d_ff, col_block):
        g = _dot(h, wg_ref[:, c:c + col_block])
        u = _dot(h, wu_ref[:, c:c + col_block])
        a_blocks.append((g * (1.0 / (1.0 + jnp.exp(-g))) * u).astype(BF16))
        if len(a_blocks) == DOWN_PROJ_BLOCKS or c + col_block == d_ff:
            k0 = c + col_block - len(a_blocks) * col_block
            acc = acc + _dot(jnp.concatenate(a_blocks, axis=1), wd_ref[k0:c + col_block, 0:d])
            a_blocks = []
    if final_norm:
        acc = _rms_norm(acc, fg_ref[...])
    o_ref[...] = acc
    _cast_blocks(cast_src, cast_dst)


def _ffn(x2, norms, wg, wu, wd, fg, next_weights, *, layer, next_mixer, tm, final_norm):
    m, d = x2.shape
    n_steps = m // tm
    casts = _cast_jobs(next_weights, next_mixer, n_steps)
    cast_specs = [c.specs(lambda i: i) for c in casts]
    outs = pl.pallas_call(
        functools.partial(_ffn_kernel, layer=layer, n_casts=len(casts), col_block=MXU_N,
                          final_norm=final_norm),
        out_shape=[jax.ShapeDtypeStruct((m, d), F32)] + [c.out_shape() for c in casts],
        grid=(n_steps,),
        in_specs=[
            pl.BlockSpec((tm, d), lambda i: (i, 0)),
            _resident(norms.shape),
            _resident(wg.shape),
            _resident(wu.shape),
            _resident(wd.shape),
            _resident(fg.shape),
        ] + [s[0] for s in cast_specs],
        out_specs=[pl.BlockSpec((tm, d), lambda i: (i, 0))] + [s[1] for s in cast_specs],
        compiler_params=_compiler_params(1),
        name="swiglu_final" if final_norm else "swiglu",
    )(x2, norms, wg, wu, wd, fg, *[c.stacked for c in casts])
    return outs[0], outs[1:]


def _sgu_kernel(*refs, layer, n_casts, col_block):
    x_ref, xp_ref, g_ref, win_ref, vg_ref, vb_ref, ws_ref, bs_ref, wout_ref = refs[:9]
    cast_src = refs[9:9 + n_casts]
    o_ref = refs[9 + n_casts]
    cast_dst = refs[10 + n_casts:10 + 2 * n_casts]
    wtri_ref, gate_bias_ref, h_ref, sv_ref = refs[10 + 2 * n_casts:]

    tm, d = x_ref.shape
    e_a = wout_ref.shape[0]
    n_groups = ws_ref.shape[0]
    n_chunks = tm // CHUNK
    n_blocks = e_a // col_block
    groups_per_block = col_block // LANES
    assert n_blocks % OUT_PROJ_BLOCKS == 0

    @pl.when(pl.program_id(0) == 0)
    def _():
        row = lax.broadcasted_iota(jnp.int32, (CHUNK, CHUNK), 0)
        col = lax.broadcasted_iota(jnp.int32, (CHUNK, CHUNK), 1)
        for hg in range(n_groups):
            lanes = slice(hg * LANES, (hg + 1) * LANES)
            w_causal = jnp.where(col <= row, ws_ref[hg], 0.0)
            wtri_ref[hg] = w_causal.astype(BF16)
            b_col = jnp.sum(jnp.where(col == row, bs_ref[hg:hg + 1, :], 0.0), axis=1, keepdims=True)
            gate_bias_ref[:, lanes] = (jnp.sum(w_causal, axis=1, keepdims=True) * vb_ref[:, lanes]
                                       + b_col)
        h_ref[...] = jnp.zeros(h_ref.shape, BF16)
        sv_ref[...] = jnp.zeros(sv_ref.shape, F32)

    def v_proj(b):
        return _dot(h, win_ref[:, e_a + b * col_block:e_a + (b + 1) * col_block])

    def spatial(b):
        vhat = ((v_blocks[b] - mu) * rstd).astype(BF16)
        for gi in range(groups_per_block):
            lanes = slice(gi * LANES, (gi + 1) * LANES)
            rhs = jnp.concatenate(
                [vhat[n * CHUNK:(n + 1) * CHUNK, lanes] for n in range(n_chunks)], axis=1)
            sv_ref[b * groups_per_block + gi] = _dot(wtri_ref[b * groups_per_block + gi], rhs)

    def u_proj(b):
        return _dot(h_prev, win_ref[:, b * col_block:(b + 1) * col_block])

    def gate(b, u):
        y_cols = []
        for gi in range(groups_per_block):
            hg = b * groups_per_block + gi
            lanes = slice(gi * LANES, (gi + 1) * LANES)
            gain = vg_ref[:, hg * LANES:(hg + 1) * LANES]
            bias = gate_bias_ref[:, hg * LANES:(hg + 1) * LANES]
            y_cols.append(jnp.concatenate(
                [u[n * CHUNK:(n + 1) * CHUNK, lanes]
                 * (sv_ref[hg, :, n * LANES:(n + 1) * LANES] * gain + bias)
                 for n in range(n_chunks)], axis=0))
        return jnp.concatenate(y_cols, axis=1).astype(BF16)

    h_prev = h_ref[...]
    zu = u_proj(0)
    h = _rms_norm(x_ref[...], g_ref[layer:layer + 1, :]).astype(BF16)
    zv_blocks = []
    y_blocks = []
    for b in range(n_blocks):
        zv_blocks.append(v_proj(b))
        zu_next = u_proj(b + 1) if b + 1 < n_blocks else None
        y_blocks.append(gate(b, _gelu_tanh(zu)))
        zu = zu_next
    acc = xp_ref[...]
    for k in range(0, n_blocks, OUT_PROJ_BLOCKS):
        acc = acc + _dot(jnp.concatenate(y_blocks[k:k + OUT_PROJ_BLOCKS], axis=1),
                         wout_ref[k * col_block:(k + OUT_PROJ_BLOCKS) * col_block, 0:d])
    o_ref[...] = acc
    v_blocks = []
    sum_v = jnp.zeros((tm, LANES), F32)
    sum_vv = jnp.zeros((tm, LANES), F32)
    for b in range(n_blocks):
        v_blocks.append(_gelu_tanh(zv_blocks[b]))
        for c in range(0, col_block, LANES):
            t = v_blocks[b][:, c:c + LANES]
            sum_v = sum_v + t
            sum_vv = sum_vv + t * t
    mu = jnp.sum(sum_v, axis=-1, keepdims=True) * (1.0 / e_a)
    rstd = lax.rsqrt(jnp.sum(sum_vv, axis=-1, keepdims=True) * (1.0 / e_a) - mu * mu + EPS)
    for b in range(n_blocks):
        spatial(b)
    h_ref[...] = h
    _cast_blocks(cast_src, cast_dst)


def _sgu(x2, norms, win, vg, vb, ws, bs, wout, next_weights, *, layer, mixer, tm):
    m, d = x2.shape
    e_a = wout.shape[0]
    n_groups = ws.shape[1]
    n_tiles = m // tm
    casts = _cast_jobs(next_weights, layer, n_tiles)
    cast_specs = [c.specs(lambda i: i) for c in casts]
    cur = lambda i: (jnp.minimum(i, n_tiles - 1), 0)
    prev = lambda i: (jnp.maximum(i - 1, 0), 0)
    outs = pl.pallas_call(
        functools.partial(_sgu_kernel, layer=layer, n_casts=len(casts), col_block=MXU_N),
        out_shape=[jax.ShapeDtypeStruct((m, d), F32)] + [c.out_shape() for c in casts],
        grid=(n_tiles + 1,),
        in_specs=[
            pl.BlockSpec((tm, d), cur),
            pl.BlockSpec((tm, d), prev),
            _resident(norms.shape),
            _resident(win.shape),
            _resident_layer(vg, mixer),
            _resident_layer(vb, mixer),
            _resident_layer(ws, mixer),
            _resident_layer(bs, mixer),
            _resident(wout.shape),
        ] + [s[0] for s in cast_specs],
        out_specs=[pl.BlockSpec((tm, d), prev)] + [s[1] for s in cast_specs],
        scratch_shapes=[
            pltpu.VMEM((n_groups, CHUNK, CHUNK), BF16),
            pltpu.VMEM((CHUNK, e_a), F32),
            pltpu.VMEM((tm, d), BF16),
            pltpu.VMEM((n_groups, CHUNK, tm), F32),
        ],
        compiler_params=_compiler_params(1),
        name="sgu_mixer",
    )(x2, x2, norms, win, vg, vb, ws, bs, wout, *[c.stacked for c in casts])
    return outs[0], outs[1:]


def _conv_kernel(*refs, layer, n_casts, halo):
    x_ref, g_ref, win_ref, cw_ref, wout_ref = refs[:5]
    cast_src = refs[5:5 + n_casts]
    o_ref = refs[5 + n_casts]
    cast_dst = refs[6 + n_casts:6 + 2 * n_casts]
    z_ref, = refs[6 + 2 * n_casts:]

    tm = x_ref.shape[1]
    e_b = wout_ref.shape[0]
    conv_w = cw_ref.shape[0]
    n_blocks, _, col_block = z_ref.shape

    @pl.when(pl.program_id(1) == 0)
    def _():
        z_ref[:, 0:halo, :] = jnp.zeros((n_blocks, halo, col_block), F32)

    @pl.when(pl.program_id(1) != 0)
    def _():
        z_ref[:, 0:halo, :] = z_ref[:, tm:tm + halo, :]

    x = x_ref[0]
    h = _rms_norm(x, g_ref[layer:layer + 1, :]).astype(BF16)
    y_blocks = []
    for j in range(n_blocks):
        c = j * col_block
        cols = slice(c, c + col_block)
        z_ref[j, halo:halo + tm, :] = (_dot(h, win_ref[:, e_b + c:e_b + c + col_block])
                                       * _dot(h, win_ref[:, 2 * e_b + c:2 * e_b + c + col_block]))
        b_gate = _dot(h, win_ref[:, cols])
        conv = z_ref[j, halo:halo + tm, :] * cw_ref[conv_w - 1:conv_w, cols]
        for k in range(conv_w - 1):
            back = conv_w - 1 - k
            conv = conv + z_ref[j, halo - back:halo - back + tm, :] * cw_ref[k:k + 1, cols]
        y_blocks.append((b_gate * conv).astype(BF16))
    o_ref[0] = x + _dot(jnp.concatenate(y_blocks, axis=1), wout_ref[:, 0:x.shape[1]])
    _cast_blocks(cast_src, cast_dst)


def _short_conv(x3, norms, win, cw, wout, next_weights, *, layer, mixer, tm):
    b, s, d = x3.shape
    e_b = wout.shape[0]
    halo = SUBLANES
    assert cw.shape[1] - 1 <= halo
    tiles_per_seq = s // tm
    casts = _cast_jobs(next_weights, layer, b * tiles_per_seq)
    cast_specs = [c.specs(lambda i, j: i * tiles_per_seq + j) for c in casts]
    outs = pl.pallas_call(
        functools.partial(_conv_kernel, layer=layer, n_casts=len(casts), halo=halo),
        out_shape=[jax.ShapeDtypeStruct((b, s, d), F32)] + [c.out_shape() for c in casts],
        grid=(b, tiles_per_seq),
        in_specs=[
            pl.BlockSpec((1, tm, d), lambda i, j: (i, j, 0)),
            _resident(norms.shape),
            _resident(win.shape),
            _resident_layer(cw, mixer),
            _resident(wout.shape),
        ] + [s[0] for s in cast_specs],
        out_specs=[pl.BlockSpec((1, tm, d), lambda i, j: (i, j, 0))] + [s[1] for s in cast_specs],
        scratch_shapes=[pltpu.VMEM((e_b // MXU_N, halo + tm, MXU_N), F32)],
        compiler_params=_compiler_params(2),
        name="short_conv_mixer",
    )(x3, norms, win, cw, wout, *[c.stacked for c in casts])
    return outs[0], outs[1:]


def kernel(x, mix_norm, ffn_norm, a_w_in, a_v_gain, a_v_bias, a_w_s, a_b_s, a_w_out,
           b_w_in, b_conv_w, b_w_out, ffn_w_gate, ffn_w_up, ffn_w_down, final_norm):
    b, s, d = x.shape
    depth = mix_norm.shape[0]
    tm = 1024
    tm_sgu = 512
    assert s % tm == 0 and s % tm_sgu == 0 and tm_sgu % CHUNK == 0 and a_w_s.shape[-1] == CHUNK
    assert a_w_s.shape[1] * LANES == a_w_out.shape[1]

    fg = final_norm.reshape(1, d)
    a_v_gain3 = a_v_gain.reshape(a_v_gain.shape[0], 1, -1)
    a_v_bias3 = a_v_bias.reshape(a_v_bias.shape[0], 1, -1)
    ffn_weights = (ffn_w_gate, ffn_w_up, ffn_w_down)

    def mixer_weights(i):
        return (a_w_in, a_w_out) if i % 2 == 0 else (b_w_in, b_w_out)

    ready = _cast_call(mixer_weights(0), 0)
    x2 = x.reshape(b * s, d)
    for i in range(depth):
        j = i // 2
        if i % 2 == 0:
            x2, ready = _sgu(x2, mix_norm, ready[0], a_v_gain3, a_v_bias3, a_w_s, a_b_s, ready[1],
                             ffn_weights, layer=i, mixer=j, tm=tm_sgu)
        else:
            x3, ready = _short_conv(x2.reshape(b, s, d), mix_norm, ready[0], b_conv_w, ready[1],
                                    ffn_weights, layer=i, mixer=j, tm=tm)
            x2 = x3.reshape(b * s, d)
        last = i == depth - 1
        x2, ready = _ffn(x2, ffn_norm, *ready, fg, () if last else mixer_weights(i + 1),
                         layer=i, next_mixer=(i + 1) // 2, tm=tm, final_norm=last)
    return x2.reshape(b, s, d)
```

```python
import functools
import math
from typing import NamedTuple

import jax
import jax.numpy as jnp
from jax import lax
from jax.experimental import pallas as pl
from jax.experimental.pallas import tpu as pltpu

EPS = 1e-6
CHUNK = 128
LANES = 128
SUBLANES = 8
BF16_ROWS = 16
MXU_N = 256
OUT_PROJ_BLOCKS = 4
VMEM_LIMIT_BYTES = 56 * 1024 * 1024

BF16 = jnp.bfloat16
F32 = jnp.float32


def _dot(a, b):
    return jnp.dot(a, b, preferred_element_type=F32)


def _rms_norm(x, g):
    y = x * lax.rsqrt(jnp.mean(x * x, axis=-1, keepdims=True) + EPS)
    return y * g


def _gelu_tanh(x):
    k = -2.0 * math.sqrt(2.0 / math.pi) * math.log2(math.e)
    return x / (1.0 + jnp.exp2(x * (k + (k * 0.044715) * (x * x))))


def _odd_tile_pitch(n):
    assert n % LANES == 0
    return n if (n // LANES) % 2 == 1 else n + LANES


def _resident(shape):
    return pl.BlockSpec(shape, lambda *_: (0,) * len(shape), pipeline_mode=pl.Buffered(1))


def _resident_layer(stacked, layer):
    rest = stacked.shape[1:]
    return pl.BlockSpec((None,) + rest, lambda *_: (layer,) + (0,) * len(rest),
                        pipeline_mode=pl.Buffered(1))


def _compiler_params(n_grid_axes):
    return pltpu.CompilerParams(dimension_semantics=("arbitrary",) * n_grid_axes,
                                vmem_limit_bytes=VMEM_LIMIT_BYTES)


class _Cast(NamedTuple):
    stacked: jax.Array
    layer: int
    rows: int

    @property
    def n_blocks(self):
        n_rows = self.stacked.shape[1]
        assert n_rows % self.rows == 0 and self.rows % BF16_ROWS == 0
        return n_rows // self.rows

    def out_shape(self):
        _, n_rows, n_cols = self.stacked.shape
        return jax.ShapeDtypeStruct((n_rows, _odd_tile_pitch(n_cols)), BF16)

    def specs(self, linear_step):
        n_cols = self.stacked.shape[2]
        block = lambda *g: jnp.minimum(linear_step(*g), self.n_blocks - 1)
        return (pl.BlockSpec((None, self.rows, n_cols), lambda *g: (self.layer, block(*g), 0)),
                pl.BlockSpec((self.rows, _odd_tile_pitch(n_cols)), lambda *g: (block(*g), 0)))


def _cast_jobs(weights, layer, n_steps):
    jobs = []
    for w in weights:
        n_rows = w.shape[1]
        rows = next(r for r in range(BF16_ROWS, n_rows + 1, BF16_ROWS)
                    if n_rows % r == 0 and n_rows // r <= n_steps)
        jobs.append(_Cast(w, layer, rows))
    return jobs


def _cast_blocks(src_refs, dst_refs):
    for src, dst in zip(src_refs, dst_refs):
        n = src.shape[1]
        dst[:, 0:n] = src[...].astype(BF16)
        if dst.shape[1] > n:
            dst[:, n:] = jnp.zeros((dst.shape[0], dst.shape[1] - n), BF16)


def _cast_kernel(*refs):
    n = len(refs) // 2
    _cast_blocks(refs[:n], refs[n:])


def _cast_call(weights, layer):
    n_steps = 8
    casts = _cast_jobs(weights, layer, n_steps)
    specs = [c.specs(lambda i: i) for c in casts]
    return pl.pallas_call(
        _cast_kernel,
        out_shape=[c.out_shape() for c in casts],
        grid=(n_steps,),
        in_specs=[s[0] for s in specs],
        out_specs=[s[1] for s in specs],
        compiler_params=_compiler_params(1),
        name="cast_weights",
    )(*[c.stacked for c in casts])


def _ffn_kernel(*refs, layer, n_casts, col_block, final_norm):
    x_ref, xn_ref, g_ref, wg_ref, wu_ref, wd_ref, fg_ref = refs[:7]
    cast_src = refs[7:7 + n_casts]
    o_ref = refs[7 + n_casts]
    cast_dst = refs[8 + n_casts:8 + 2 * n_casts]
    h_ref, g0_ref, u0_ref = refs[8 + 2 * n_casts:]
    d = x_ref.shape[1]
    d_ff = wd_ref.shape[0]
    gain = g_ref[layer:layer + 1, :]

    def start_tile(x):
        h = _rms_norm(x, gain).astype(BF16)
        h_ref[...] = h
        g0_ref[...] = _dot(h, wg_ref[:, 0:col_block])
        u0_ref[...] = _dot(h, wu_ref[:, 0:col_block])

    def swiglu(g, u):
        return (g * (1.0 / (1.0 + jnp.exp(-g))) * u).astype(BF16)

    @pl.when(pl.program_id(0) == 0)
    def _():
        start_tile(x_ref[...])

    h = h_ref[...]
    a_blocks = [swiglu(g0_ref[...], u0_ref[...])]
    for c in range(col_block, d_ff, col_block):
        a_blocks.append(swiglu(_dot(h, wg_ref[:, c:c + col_block]),
                               _dot(h, wu_ref[:, c:c + col_block])))
    acc = x_ref[...] + _dot(jnp.concatenate(a_blocks, axis=1), wd_ref[:, 0:d])
    if final_norm:
        acc = _rms_norm(acc, fg_ref[...])
    o_ref[...] = acc
    start_tile(xn_ref[...])
    _cast_blocks(cast_src, cast_dst)


def _ffn(x2, norms, wg, wu, wd, fg, next_weights, *, layer, next_mixer, tm, final_norm):
    m, d = x2.shape
    n_steps = m // tm
    casts = _cast_jobs(next_weights, next_mixer, n_steps)
    cast_specs = [c.specs(lambda i: i) for c in casts]
    outs = pl.pallas_call(
        functools.partial(_ffn_kernel, layer=layer, n_casts=len(casts), col_block=MXU_N,
                          final_norm=final_norm),
        out_shape=[jax.ShapeDtypeStruct((m, d), F32)] + [c.out_shape() for c in casts],
        grid=(n_steps,),
        in_specs=[
            pl.BlockSpec((tm, d), lambda i: (i, 0)),
            pl.BlockSpec((tm, d), lambda i: (jnp.minimum(i + 1, n_steps - 1), 0)),
            _resident(norms.shape),
            _resident(wg.shape),
            _resident(wu.shape),
            _resident(wd.shape),
            _resident(fg.shape),
        ] + [s[0] for s in cast_specs],
        out_specs=[pl.BlockSpec((tm, d), lambda i: (i, 0))] + [s[1] for s in cast_specs],
        scratch_shapes=[
            pltpu.VMEM((tm, d), BF16),
            pltpu.VMEM((tm, MXU_N), F32),
            pltpu.VMEM((tm, MXU_N), F32),
        ],
        compiler_params=_compiler_params(1),
        name="swiglu_final" if final_norm else "swiglu",
    )(x2, x2, norms, wg, wu, wd, fg, *[c.stacked for c in casts])
    return outs[0], outs[1:]


def _sgu_kernel(*refs, layer, n_casts, col_block):
    x_ref, xp_ref, g_ref, win_ref, vg_ref, vb_ref, ws_ref, bs_ref, wout_ref = refs[:9]
    cast_src = refs[9:9 + n_casts]
    o_ref = refs[9 + n_casts]
    cast_dst = refs[10 + n_casts:10 + 2 * n_casts]
    wtri_ref, gate_bias_ref, h_ref, sv_ref = refs[10 + 2 * n_casts:]

    tm, d = x_ref.shape
    e_a = wout_ref.shape[0]
    n_groups = ws_ref.shape[0]
    n_chunks = tm // CHUNK
    n_blocks = e_a // col_block
    groups_per_block = col_block // LANES
    assert n_blocks % OUT_PROJ_BLOCKS == 0

    @pl.when(pl.program_id(0) == 0)
    def _():
        row = lax.broadcasted_iota(jnp.int32, (CHUNK, CHUNK), 0)
        col = lax.broadcasted_iota(jnp.int32, (CHUNK, CHUNK), 1)
        for hg in range(n_groups):
            lanes = slice(hg * LANES, (hg + 1) * LANES)
            w_causal = jnp.where(col <= row, ws_ref[hg], 0.0)
            wtri_ref[hg] = w_causal.astype(BF16)
            b_col = jnp.sum(jnp.where(col == row, bs_ref[hg:hg + 1, :], 0.0), axis=1, keepdims=True)
            gate_bias_ref[:, lanes] = (jnp.sum(w_causal, axis=1, keepdims=True) * vb_ref[:, lanes]
                                       + b_col)
        h_ref[...] = jnp.zeros(h_ref.shape, BF16)
        sv_ref[...] = jnp.zeros(sv_ref.shape, F32)

    def v_proj(b):
        return _dot(h, win_ref[:, e_a + b * col_block:e_a + (b + 1) * col_block])

    def spatial(b):
        vhat = ((v_blocks[b] - mu) * rstd).astype(BF16)
        for gi in range(groups_per_block):
            lanes = slice(gi * LANES, (gi + 1) * LANES)
            rhs = jnp.concatenate(
                [vhat[n * CHUNK:(n + 1) * CHUNK, lanes] for n in range(n_chunks)], axis=1)
            sv_ref[b * groups_per_block + gi] = _dot(wtri_ref[b * groups_per_block + gi], rhs)

    def u_proj(b):
        return _dot(h_prev, win_ref[:, b * col_block:(b + 1) * col_block])

    def gate(b, u):
        y_cols = []
        for gi in range(groups_per_block):
            hg = b * groups_per_block + gi
            lanes = slice(gi * LANES, (gi + 1) * LANES)
            gain = vg_ref[:, hg * LANES:(hg + 1) * LANES]
            bias = gate_bias_ref[:, hg * LANES:(hg + 1) * LANES]
            y_cols.append(jnp.concatenate(
                [u[n * CHUNK:(n + 1) * CHUNK, lanes]
                 * (sv_ref[hg, :, n * LANES:(n + 1) * LANES] * gain + bias)
                 for n in range(n_chunks)], axis=0))
        return jnp.concatenate(y_cols, axis=1).astype(BF16)

    h_prev = h_ref[...]
    zu = u_proj(0)
    h = _rms_norm(x_ref[...], g_ref[layer:layer + 1, :]).astype(BF16)
    zv_blocks = []
    y_blocks = []
    for b in range(n_blocks):
        zv_blocks.append(v_proj(b))
        zu_next = u_proj(b + 1) if b + 1 < n_blocks else None
        y_blocks.append(gate(b, _gelu_tanh(zu)))
        zu = zu_next
    acc = xp_ref[...]
    for k in range(0, n_blocks, OUT_PROJ_BLOCKS):
        acc = acc + _dot(jnp.concatenate(y_blocks[k:k + OUT_PROJ_BLOCKS], axis=1),
                         wout_ref[k * col_block:(k + OUT_PROJ_BLOCKS) * col_block, 0:d])
    o_ref[...] = acc
    v_blocks = []
    sum_v = jnp.zeros((tm, LANES), F32)
    sum_vv = jnp.zeros((tm, LANES), F32)
    for b in range(n_blocks):
        v_blocks.append(_gelu_tanh(zv_blocks[b]))
        for c in range(0, col_block, LANES):
            t = v_blocks[b][:, c:c + LANES]
            sum_v = sum_v + t
            sum_vv = sum_vv + t * t
    mu = jnp.sum(sum_v, axis=-1, keepdims=True) * (1.0 / e_a)
    rstd = lax.rsqrt(jnp.sum(sum_vv, axis=-1, keepdims=True) * (1.0 / e_a) - mu * mu + EPS)
    for b in range(n_blocks):
        spatial(b)
    h_ref[...] = h
    _cast_blocks(cast_src, cast_dst)


def _sgu(x2, norms, win, vg, vb, ws, bs, wout, next_weights, *, layer, mixer, tm):
    m, d = x2.shape
    e_a = wout.shape[0]
    n_groups = ws.shape[1]
    n_tiles = m // tm
    casts = _cast_jobs(next_weights, layer, n_tiles)
    cast_specs = [c.specs(lambda i: i) for c in casts]
    cur = lambda i: (jnp.minimum(i, n_tiles - 1), 0)
    prev = lambda i: (jnp.maximum(i - 1, 0), 0)
    outs = pl.pallas_call(
        functools.partial(_sgu_kernel, layer=layer, n_casts=len(casts), col_block=MXU_N),
        out_shape=[jax.ShapeDtypeStruct((m, d), F32)] + [c.out_shape() for c in casts],
        grid=(n_tiles + 1,),
        in_specs=[
            pl.BlockSpec((tm, d), cur),
            pl.BlockSpec((tm, d), prev),
            _resident(norms.shape),
            _resident(win.shape),
            _resident_layer(vg, mixer),
            _resident_layer(vb, mixer),
            _resident_layer(ws, mixer),
            _resident_layer(bs, mixer),
            _resident(wout.shape),
        ] + [s[0] for s in cast_specs],
        out_specs=[pl.BlockSpec((tm, d), prev)] + [s[1] for s in cast_specs],
        scratch_shapes=[
            pltpu.VMEM((n_groups, CHUNK, CHUNK), BF16),
            pltpu.VMEM((CHUNK, e_a), F32),
            pltpu.VMEM((tm, d), BF16),
            pltpu.VMEM((n_groups, CHUNK, tm), F32),
        ],
        compiler_params=_compiler_params(1),
        name="sgu_mixer",
    )(x2, x2, norms, win, vg, vb, ws, bs, wout, *[c.stacked for c in casts])
    return outs[0], outs[1:]


def _conv_kernel(*refs, layer, n_casts, halo):
    x_ref, g_ref, win_ref, cw_ref, wout_ref = refs[:5]
    cast_src = refs[5:5 + n_casts]
    o_ref = refs[5 + n_casts]
    cast_dst = refs[6 + n_casts:6 + 2 * n_casts]
    z_ref, = refs[6 + 2 * n_casts:]

    tm = x_ref.shape[1]
    e_b = wout_ref.shape[0]
    conv_w = cw_ref.shape[0]
    n_blocks, _, col_block = z_ref.shape

    @pl.when(pl.program_id(1) == 0)
    def _():
        z_ref[:, 0:halo, :] = jnp.zeros((n_blocks, halo, col_block), F32)

    @pl.when(pl.program_id(1) != 0)
    def _():
        z_ref[:, 0:halo, :] = z_ref[:, tm:tm + halo, :]

    x = x_ref[0]
    h = _rms_norm(x, g_ref[layer:layer + 1, :]).astype(BF16)
    y_blocks = []
    for j in range(n_blocks):
        c = j * col_block
        cols = slice(c, c + col_block)
        z_ref[j, halo:halo + tm, :] = (_dot(h, win_ref[:, e_b + c:e_b + c + col_block])
                                       * _dot(h, win_ref[:, 2 * e_b + c:2 * e_b + c + col_block]))
        b_gate = _dot(h, win_ref[:, cols])
        conv = z_ref[j, halo:halo + tm, :] * cw_ref[conv_w - 1:conv_w, cols]
        for k in range(conv_w - 1):
            back = conv_w - 1 - k
            conv = conv + z_ref[j, halo - back:halo - back + tm, :] * cw_ref[k:k + 1, cols]
        y_blocks.append((b_gate * conv).astype(BF16))
    o_ref[0] = x + _dot(jnp.concatenate(y_blocks, axis=1), wout_ref[:, 0:x.shape[1]])
    _cast_blocks(cast_src, cast_dst)


def _short_conv(x3, norms, win, cw, wout, next_weights, *, layer, mixer, tm):
    b, s, d = x3.shape
    e_b = wout.shape[0]
    halo = SUBLANES
    assert cw.shape[1] - 1 <= halo
    tiles_per_seq = s // tm
    casts = _cast_jobs(next_weights, layer, b * tiles_per_seq)
    cast_specs = [c.specs(lambda i, j: i * tiles_per_seq + j) for c in casts]
    outs = pl.pallas_call(
        functools.partial(_conv_kernel, layer=layer, n_casts=len(casts), halo=halo),
        out_shape=[jax.ShapeDtypeStruct((b, s, d), F32)] + [c.out_shape() for c in casts],
        grid=(b, tiles_per_seq),
        in_specs=[
            pl.BlockSpec((1, tm, d), lambda i, j: (i, j, 0)),
            _resident(norms.shape),
            _resident(win.shape),
            _resident_layer(cw, mixer),
            _resident(wout.shape),
        ] + [s[0] for s in cast_specs],
        out_specs=[pl.BlockSpec((1, tm, d), lambda i, j: (i, j, 0))] + [s[1] for s in cast_specs],
        scratch_shapes=[pltpu.VMEM((e_b // MXU_N, halo + tm, MXU_N), F32)],
        compiler_params=_compiler_params(2),
        name="short_conv_mixer",
    )(x3, norms, win, cw, wout, *[c.stacked for c in casts])
    return outs[0], outs[1:]


def kernel(x, mix_norm, ffn_norm, a_w_in, a_v_gain, a_v_bias, a_w_s, a_b_s, a_w_out,
           b_w_in, b_conv_w, b_w_out, ffn_w_gate, ffn_w_up, ffn_w_down, final_norm):
    b, s, d = x.shape
    depth = mix_norm.shape[0]
    tm = 1024
    tm_ffn = 512
    tm_sgu = 512
    assert s % tm == 0 and s % tm_sgu == 0 and tm_sgu % CHUNK == 0 and a_w_s.shape[-1] == CHUNK
    assert (b * s) % tm_ffn == 0
    assert a_w_s.shape[1] * LANES == a_w_out.shape[1]

    fg = final_norm.reshape(1, d)
    a_v_gain3 = a_v_gain.reshape(a_v_gain.shape[0], 1, -1)
    a_v_bias3 = a_v_bias.reshape(a_v_bias.shape[0], 1, -1)
    ffn_weights = (ffn_w_gate, ffn_w_up, ffn_w_down)

    def mixer_weights(i):
        return (a_w_in, a_w_out) if i % 2 == 0 else (b_w_in, b_w_out)

    ready = _cast_call(mixer_weights(0), 0)
    x2 = x.reshape(b * s, d)
    for i in range(depth):
        j = i // 2
        if i % 2 == 0:
            x2, ready = _sgu(x2, mix_norm, ready[0], a_v_gain3, a_v_bias3, a_w_s, a_b_s, ready[1],
                             ffn_weights, layer=i, mixer=j, tm=tm_sgu)
        else:
            x3, ready = _short_conv(x2.reshape(b, s, d), mix_norm, ready[0], b_conv_w, ready[1],
                                    ffn_weights, layer=i, mixer=j, tm=tm)
            x2 = x3.reshape(b * s, d)
        last = i == depth - 1
        x2, ready = _ffn(x2, ffn_norm, *ready, fg, () if last else mixer_weights(i + 1),
                         layer=i, next_mixer=(i + 1) // 2, tm=tm_ffn, final_norm=last)
    return x2.reshape(b, s, d)
```

```python
import functools
import math
from typing import NamedTuple

import jax
import jax.numpy as jnp
from jax import lax
from jax.experimental import pallas as pl
from jax.experimental.pallas import tpu as pltpu

EPS = 1e-6
CHUNK = 128
LANES = 128
SUBLANES = 8
BF16_ROWS = 16
MXU_N = 256
OUT_PROJ_BLOCKS = 4
DOWN_PROJ_ROW_PARTS = 4
VMEM_LIMIT_BYTES = 56 * 1024 * 1024

BF16 = jnp.bfloat16
F32 = jnp.float32


def _dot(a, b):
    return jnp.dot(a, b, preferred_element_type=F32)


def _rms_norm(x, g):
    y = x * lax.rsqrt(jnp.mean(x * x, axis=-1, keepdims=True) + EPS)
    return y * g


def _gelu_tanh(x):
    k = -2.0 * math.sqrt(2.0 / math.pi) * math.log2(math.e)
    return x / (1.0 + jnp.exp2(x * (k + (k * 0.044715) * (x * x))))


def _odd_tile_pitch(n):
    assert n % LANES == 0
    return n if (n // LANES) % 2 == 1 else n + LANES


def _resident(shape):
    return pl.BlockSpec(shape, lambda *_: (0,) * len(shape), pipeline_mode=pl.Buffered(1))


def _resident_layer(stacked, layer):
    rest = stacked.shape[1:]
    return pl.BlockSpec((None,) + rest, lambda *_: (layer,) + (0,) * len(rest),
                        pipeline_mode=pl.Buffered(1))


def _compiler_params(n_grid_axes):
    return pltpu.CompilerParams(dimension_semantics=("arbitrary",) * n_grid_axes,
                                vmem_limit_bytes=VMEM_LIMIT_BYTES)


class _Cast(NamedTuple):
    stacked: jax.Array
    layer: int
    rows: int

    @property
    def n_blocks(self):
        n_rows = self.stacked.shape[1]
        assert n_rows % self.rows == 0 and self.rows % BF16_ROWS == 0
        return n_rows // self.rows

    def out_shape(self):
        _, n_rows, n_cols = self.stacked.shape
        return jax.ShapeDtypeStruct((n_rows, _odd_tile_pitch(n_cols)), BF16)

    def specs(self, linear_step):
        n_cols = self.stacked.shape[2]
        block = lambda *g: jnp.minimum(linear_step(*g), self.n_blocks - 1)
        return (pl.BlockSpec((None, self.rows, n_cols), lambda *g: (self.layer, block(*g), 0)),
                pl.BlockSpec((self.rows, _odd_tile_pitch(n_cols)), lambda *g: (block(*g), 0)))


def _cast_jobs(weights, layer, n_steps):
    jobs = []
    for w in weights:
        n_rows = w.shape[1]
        rows = next(r for r in range(BF16_ROWS, n_rows + 1, BF16_ROWS)
                    if n_rows % r == 0 and n_rows // r <= n_steps)
        jobs.append(_Cast(w, layer, rows))
    return jobs


def _cast_blocks(src_refs, dst_refs):
    for src, dst in zip(src_refs, dst_refs):
        n = src.shape[1]
        dst[:, 0:n] = src[...].astype(BF16)
        if dst.shape[1] > n:
            dst[:, n:] = jnp.zeros((dst.shape[0], dst.shape[1] - n), BF16)


def _cast_kernel(*refs):
    n = len(refs) // 2
    _cast_blocks(refs[:n], refs[n:])


def _cast_call(weights, layer):
    n_steps = 8
    casts = _cast_jobs(weights, layer, n_steps)
    specs = [c.specs(lambda i: i) for c in casts]
    return pl.pallas_call(
        _cast_kernel,
        out_shape=[c.out_shape() for c in casts],
        grid=(n_steps,),
        in_specs=[s[0] for s in specs],
        out_specs=[s[1] for s in specs],
        compiler_params=_compiler_params(1),
        name="cast_weights",
    )(*[c.stacked for c in casts])


def _ffn_kernel(*refs, layer, n_casts, col_block, final_norm):
    x_ref, g_ref, wg_ref, wu_ref, wd_ref, fg_ref = refs[:6]
    cast_src = refs[6:6 + n_casts]
    o_ref = refs[6 + n_casts]
    cast_dst = refs[7 + n_casts:]

    x = x_ref[...]
    d = x.shape[1]
    h = _rms_norm(x, g_ref[layer:layer + 1, :]).astype(BF16)
    d_ff = wd_ref.shape[0]
    a_blocks = []
    for c in range(0, d_ff, col_block):
        g = _dot(h, wg_ref[:, c:c + col_block])
        u = _dot(h, wu_ref[:, c:c + col_block])
        a_blocks.append((g * (1.0 / (1.0 + jnp.exp(-g))) * u).astype(BF16))
    a = jnp.concatenate(a_blocks, axis=1)
    rows = x.shape[0] // DOWN_PROJ_ROW_PARTS
    for r in range(0, x.shape[0], rows):
        acc = x[r:r + rows] + _dot(a[r:r + rows], wd_ref[:, 0:d])
        if final_norm:
            acc = _rms_norm(acc, fg_ref[...])
        o_ref[r:r + rows, :] = acc
    _cast_blocks(cast_src, cast_dst)


def _ffn(x2, norms, wg, wu, wd, fg, next_weights, *, layer, next_mixer, tm, final_norm):
    m, d = x2.shape
    n_steps = m // tm
    casts = _cast_jobs(next_weights, next_mixer, n_steps)
    cast_specs = [c.specs(lambda i: i) for c in casts]
    outs = pl.pallas_call(
        functools.partial(_ffn_kernel, layer=layer, n_casts=len(casts), col_block=MXU_N,
                          final_norm=final_norm),
        out_shape=[jax.ShapeDtypeStruct((m, d), F32)] + [c.out_shape() for c in casts],
        grid=(n_steps,),
        in_specs=[
            pl.BlockSpec((tm, d), lambda i: (i, 0)),
            _resident(norms.shape),
            _resident(wg.shape),
            _resident(wu.shape),
            _resident(wd.shape),
            _resident(fg.shape),
        ] + [s[0] for s in cast_specs],
        out_specs=[pl.BlockSpec((tm, d), lambda i: (i, 0))] + [s[1] for s in cast_specs],
        compiler_params=_compiler_params(1),
        name="swiglu_final" if final_norm else "swiglu",
    )(x2, norms, wg, wu, wd, fg, *[c.stacked for c in casts])
    return outs[0], outs[1:]


def _sgu_kernel(*refs, layer, n_casts, col_block):
    x_ref, xp_ref, g_ref, win_ref, vg_ref, vb_ref, ws_ref, bs_ref, wout_ref = refs[:9]
    cast_src = refs[9:9 + n_casts]
    o_ref = refs[9 + n_casts]
    cast_dst = refs[10 + n_casts:10 + 2 * n_casts]
    wtri_ref, gate_bias_ref, h_ref, sv_ref = refs[10 + 2 * n_casts:]

    tm, d = x_ref.shape
    e_a = wout_ref.shape[0]
    n_groups = ws_ref.shape[0]
    n_chunks = tm // CHUNK
    n_blocks = e_a // col_block
    groups_per_block = col_block // LANES
    assert n_blocks % OUT_PROJ_BLOCKS == 0

    @pl.when(pl.program_id(0) == 0)
    def _():
        row = lax.broadcasted_iota(jnp.int32, (CHUNK, CHUNK), 0)
        col = lax.broadcasted_iota(jnp.int32, (CHUNK, CHUNK), 1)
        for hg in range(n_groups):
            lanes = slice(hg * LANES, (hg + 1) * LANES)
            w_causal = jnp.where(col <= row, ws_ref[hg], 0.0)
            wtri_ref[hg] = w_causal.astype(BF16)
            b_col = jnp.sum(jnp.where(col == row, bs_ref[hg:hg + 1, :], 0.0), axis=1, keepdims=True)
            gate_bias_ref[:, lanes] = (jnp.sum(w_causal, axis=1, keepdims=True) * vb_ref[:, lanes]
                                       + b_col)
        h_ref[...] = jnp.zeros(h_ref.shape, BF16)
        sv_ref[...] = jnp.zeros(sv_ref.shape, F32)

    def v_proj(b):
        return _dot(h, win_ref[:, e_a + b * col_block:e_a + (b + 1) * col_block])

    def spatial(b):
        vhat = ((v_blocks[b] - mu) * rstd).astype(BF16)
        for gi in range(groups_per_block):
            lanes = slice(gi * LANES, (gi + 1) * LANES)
            rhs = jnp.concatenate(
                [vhat[n * CHUNK:(n + 1) * CHUNK, lanes] for n in range(n_chunks)], axis=1)
            sv_ref[b * groups_per_block + gi] = _dot(wtri_ref[b * groups_per_block + gi], rhs)

    def u_proj(b):
        return _dot(h_prev, win_ref[:, b * col_block:(b + 1) * col_block])

    def gate(b, u):
        y_cols = []
        for gi in range(groups_per_block):
            hg = b * groups_per_block + gi
            lanes = slice(gi * LANES, (gi + 1) * LANES)
            gain = vg_ref[:, hg * LANES:(hg + 1) * LANES]
            bias = gate_bias_ref[:, hg * LANES:(hg + 1) * LANES]
            y_cols.append(jnp.concatenate(
                [u[n * CHUNK:(n + 1) * CHUNK, lanes]
                 * (sv_ref[hg, :, n * LANES:(n + 1) * LANES] * gain + bias)
                 for n in range(n_chunks)], axis=0))
        return jnp.concatenate(y_cols, axis=1).astype(BF16)

    h_prev = h_ref[...]
    zu = u_proj(0)
    h = _rms_norm(x_ref[...], g_ref[layer:layer + 1, :]).astype(BF16)
    zv_blocks = []
    y_blocks = []
    for b in range(n_blocks):
        zv_blocks.append(v_proj(b))
        zu_next = u_proj(b + 1) if b + 1 < n_blocks else None
        y_blocks.append(gate(b, _gelu_tanh(zu)))
        zu = zu_next
    acc = xp_ref[...]
    for k in range(0, n_blocks, OUT_PROJ_BLOCKS):
        acc = acc + _dot(jnp.concatenate(y_blocks[k:k + OUT_PROJ_BLOCKS], axis=1),
                         wout_ref[k * col_block:(k + OUT_PROJ_BLOCKS) * col_block, 0:d])
    o_ref[...] = acc
    v_blocks = []
    sum_v = jnp.zeros((tm, LANES), F32)
    sum_vv = jnp.zeros((tm, LANES), F32)
    for b in range(n_blocks):
        v_blocks.append(_gelu_tanh(zv_blocks[b]))
        for c in range(0, col_block, LANES):
            t = v_blocks[b][:, c:c + LANES]
            sum_v = sum_v + t
            sum_vv = sum_vv + t * t
    mu = jnp.sum(sum_v, axis=-1, keepdims=True) * (1.0 / e_a)
    rstd = lax.rsqrt(jnp.sum(sum_vv, axis=-1, keepdims=True) * (1.0 / e_a) - mu * mu + EPS)
    for b in range(n_blocks):
        spatial(b)
    h_ref[...] = h
    _cast_blocks(cast_src, cast_dst)


def _sgu(x2, norms, win, vg, vb, ws, bs, wout, next_weights, *, layer, mixer, tm):
    m, d = x2.shape
    e_a = wout.shape[0]
    n_groups = ws.shape[1]
    n_tiles = m // tm
    casts = _cast_jobs(next_weights, layer, n_tiles)
    cast_specs = [c.specs(lambda i: i) for c in casts]
    cur = lambda i: (jnp.minimum(i, n_tiles - 1), 0)
    prev = lambda i: (jnp.maximum(i - 1, 0), 0)
    outs = pl.pallas_call(
        functools.partial(_sgu_kernel, layer=layer, n_casts=len(casts), col_block=MXU_N),
        out_shape=[jax.ShapeDtypeStruct((m, d), F32)] + [c.out_shape() for c in casts],
        grid=(n_tiles + 1,),
        in_specs=[
            pl.BlockSpec((tm, d), cur),
            pl.BlockSpec((tm, d), prev),
            _resident(norms.shape),
            _resident(win.shape),
            _resident_layer(vg, mixer),
            _resident_layer(vb, mixer),
            _resident_layer(ws, mixer),
            _resident_layer(bs, mixer),
            _resident(wout.shape),
        ] + [s[0] for s in cast_specs],
        out_specs=[pl.BlockSpec((tm, d), prev)] + [s[1] for s in cast_specs],
        scratch_shapes=[
            pltpu.VMEM((n_groups, CHUNK, CHUNK), BF16),
            pltpu.VMEM((CHUNK, e_a), F32),
            pltpu.VMEM((tm, d), BF16),
            pltpu.VMEM((n_groups, CHUNK, tm), F32),
        ],
        compiler_params=_compiler_params(1),
        name="sgu_mixer",
    )(x2, x2, norms, win, vg, vb, ws, bs, wout, *[c.stacked for c in casts])
    return outs[0], outs[1:]


def _conv_kernel(*refs, layer, n_casts, halo):
    x_ref, g_ref, win_ref, cw_ref, wout_ref = refs[:5]
    cast_src = refs[5:5 + n_casts]
    o_ref = refs[5 + n_casts]
    cast_dst = refs[6 + n_casts:6 + 2 * n_casts]
    z_ref, = refs[6 + 2 * n_casts:]

    tm = x_ref.shape[1]
    e_b = wout_ref.shape[0]
    conv_w = cw_ref.shape[0]
    n_blocks, _, col_block = z_ref.shape

    @pl.when(pl.program_id(1) == 0)
    def _():
        z_ref[:, 0:halo, :] = jnp.zeros((n_blocks, halo, col_block), F32)

    @pl.when(pl.program_id(1) != 0)
    def _():
        z_ref[:, 0:halo, :] = z_ref[:, tm:tm + halo, :]

    x = x_ref[0]
    h = _rms_norm(x, g_ref[layer:layer + 1, :]).astype(BF16)
    y_blocks = []
    for j in range(n_blocks):
        c = j * col_block
        cols = slice(c, c + col_block)
        z_ref[j, halo:halo + tm, :] = (_dot(h, win_ref[:, e_b + c:e_b + c + col_block])
                                       * _dot(h, win_ref[:, 2 * e_b + c:2 * e_b + c + col_block]))
        b_gate = _dot(h, win_ref[:, cols])
        conv = z_ref[j, halo:halo + tm, :] * cw_ref[conv_w - 1:conv_w, cols]
        for k in range(conv_w - 1):
            back = conv_w - 1 - k
            conv = conv + z_ref[j, halo - back:halo - back + tm, :] * cw_ref[k:k + 1, cols]
        y_blocks.append((b_gate * conv).astype(BF16))
    o_ref[0] = x + _dot(jnp.concatenate(y_blocks, axis=1), wout_ref[:, 0:x.shape[1]])
    _cast_blocks(cast_src, cast_dst)


def _short_conv(x3, norms, win, cw, wout, next_weights, *, layer, mixer, tm):
    b, s, d = x3.shape
    e_b = wout.shape[0]
    halo = SUBLANES
    assert cw.shape[1] - 1 <= halo
    tiles_per_seq = s // tm
    casts = _cast_jobs(next_weights, layer, b * tiles_per_seq)
    cast_specs = [c.specs(lambda i, j: i * tiles_per_seq + j) for c in casts]
    outs = pl.pallas_call(
        functools.partial(_conv_kernel, layer=layer, n_casts=len(casts), halo=halo),
        out_shape=[jax.ShapeDtypeStruct((b, s, d), F32)] + [c.out_shape() for c in casts],
        grid=(b, tiles_per_seq),
        in_specs=[
            pl.BlockSpec((1, tm, d), lambda i, j: (i, j, 0)),
            _resident(norms.shape),
            _resident(win.shape),
            _resident_layer(cw, mixer),
            _resident(wout.shape),
        ] + [s[0] for s in cast_specs],
        out_specs=[pl.BlockSpec((1, tm, d), lambda i, j: (i, j, 0))] + [s[1] for s in cast_specs],
        scratch_shapes=[pltpu.VMEM((e_b // MXU_N, halo + tm, MXU_N), F32)],
        compiler_params=_compiler_params(2),
        name="short_conv_mixer",
    )(x3, norms, win, cw, wout, *[c.stacked for c in casts])
    return outs[0], outs[1:]


def kernel(x, mix_norm, ffn_norm, a_w_in, a_v_gain, a_v_bias, a_w_s, a_b_s, a_w_out,
           b_w_in, b_conv_w, b_w_out, ffn_w_gate, ffn_w_up, ffn_w_down, final_norm):
    b, s, d = x.shape
    depth = mix_norm.shape[0]
    tm = 1024
    tm_sgu = 512
    assert s % tm == 0 and s % tm_sgu == 0 and tm_sgu % CHUNK == 0 and a_w_s.shape[-1] == CHUNK
    assert a_w_s.shape[1] * LANES == a_w_out.shape[1]

    fg = final_norm.reshape(1, d)
    a_v_gain3 = a_v_gain.reshape(a_v_gain.shape[0], 1, -1)
    a_v_bias3 = a_v_bias.reshape(a_v_bias.shape[0], 1, -1)
    ffn_weights = (ffn_w_gate, ffn_w_up, ffn_w_down)

    def mixer_weights(i):
        return (a_w_in, a_w_out) if i % 2 == 0 else (b_w_in, b_w_out)

    ready = _cast_call(mixer_weights(0), 0)
    x2 = x.reshape(b * s, d)
    for i in range(depth):
        j = i // 2
        if i % 2 == 0:
            x2, ready = _sgu(x2, mix_norm, ready[0], a_v_gain3, a_v_bias3, a_w_s, a_b_s, ready[1],
                             ffn_weights, layer=i, mixer=j, tm=tm_sgu)
        else:
            x3, ready = _short_conv(x2.reshape(b, s, d), mix_norm, ready[0], b_conv_w, ready[1],
                                    ffn_weights, layer=i, mixer=j, tm=tm)
            x2 = x3.reshape(b * s, d)
        last = i == depth - 1
        x2, ready = _ffn(x2, ffn_norm, *ready, fg, () if last else mixer_weights(i + 1),
                         layer=i, next_mixer=(i + 1) // 2, tm=tm, final_norm=last)
    return x2.reshape(b, s, d)
```

```python
import functools
import math
from typing import NamedTuple

import jax
import jax.numpy as jnp
from jax import lax
from jax.experimental import pallas as pl
from jax.experimental.pallas import tpu as pltpu

EPS = 1e-6
CHUNK = 128
LANES = 128
SUBLANES = 8
BF16_ROWS = 16
MXU_N = 256
OUT_PROJ_BLOCKS = 8
ROW_PARTS = 4
VMEM_LIMIT_BYTES = 56 * 1024 * 1024

BF16 = jnp.bfloat16
F32 = jnp.float32


def _dot(a, b):
    return jnp.dot(a, b, preferred_element_type=F32)


def _rms_norm(x, g):
    y = x * lax.rsqrt(jnp.mean(x * x, axis=-1, keepdims=True) + EPS)
    return y * g


def _gelu_tanh(x):
    k = -2.0 * math.sqrt(2.0 / math.pi) * math.log2(math.e)
    return x / (1.0 + jnp.exp2(x * (k + (k * 0.044715) * (x * x))))


def _odd_tile_pitch(n):
    assert n % LANES == 0
    return n if (n // LANES) % 2 == 1 else n + LANES


def _resident(shape):
    return pl.BlockSpec(shape, lambda *_: (0,) * len(shape), pipeline_mode=pl.Buffered(1))


def _resident_layer(stacked, layer):
    rest = stacked.shape[1:]
    return pl.BlockSpec((None,) + rest, lambda *_: (layer,) + (0,) * len(rest),
                        pipeline_mode=pl.Buffered(1))


def _compiler_params(n_grid_axes):
    return pltpu.CompilerParams(dimension_semantics=("arbitrary",) * n_grid_axes,
                                vmem_limit_bytes=VMEM_LIMIT_BYTES)


class _Cast(NamedTuple):
    stacked: jax.Array
    layer: int
    rows: int

    @property
    def n_blocks(self):
        n_rows = self.stacked.shape[1]
        assert n_rows % self.rows == 0 and self.rows % BF16_ROWS == 0
        return n_rows // self.rows

    def out_shape(self):
        _, n_rows, n_cols = self.stacked.shape
        return jax.ShapeDtypeStruct((n_rows, _odd_tile_pitch(n_cols)), BF16)

    def specs(self, linear_step):
        n_cols = self.stacked.shape[2]
        block = lambda *g: jnp.minimum(linear_step(*g), self.n_blocks - 1)
        return (pl.BlockSpec((None, self.rows, n_cols), lambda *g: (self.layer, block(*g), 0)),
                pl.BlockSpec((self.rows, _odd_tile_pitch(n_cols)), lambda *g: (block(*g), 0)))


def _cast_jobs(weights, layer, n_steps):
    jobs = []
    for w in weights:
        n_rows = w.shape[1]
        rows = next(r for r in range(BF16_ROWS, n_rows + 1, BF16_ROWS)
                    if n_rows % r == 0 and n_rows // r <= n_steps)
        jobs.append(_Cast(w, layer, rows))
    return jobs


def _cast_blocks(src_refs, dst_refs):
    for src, dst in zip(src_refs, dst_refs):
        n = src.shape[1]
        dst[:, 0:n] = src[...].astype(BF16)
        if dst.shape[1] > n:
            dst[:, n:] = jnp.zeros((dst.shape[0], dst.shape[1] - n), BF16)


def _cast_kernel(*refs):
    n = len(refs) // 2
    _cast_blocks(refs[:n], refs[n:])


def _cast_call(weights, layer):
    n_steps = 8
    casts = _cast_jobs(weights, layer, n_steps)
    specs = [c.specs(lambda i: i) for c in casts]
    return pl.pallas_call(
        _cast_kernel,
        out_shape=[c.out_shape() for c in casts],
        grid=(n_steps,),
        in_specs=[s[0] for s in specs],
        out_specs=[s[1] for s in specs],
        compiler_params=_compiler_params(1),
        name="cast_weights",
    )(*[c.stacked for c in casts])


def _ffn_kernel(*refs, layer, n_casts, col_block, final_norm):
    x_ref, g_ref, wg_ref, wu_ref, wd_ref, fg_ref = refs[:6]
    cast_src = refs[6:6 + n_casts]
    o_ref = refs[6 + n_casts]
    cast_dst = refs[7 + n_casts:]

    x = x_ref[...]
    tm, d = x.shape
    rows = tm // ROW_PARTS
    h_parts = [_rms_norm(x[r:r + rows], g_ref[layer:layer + 1, :]).astype(BF16)
               for r in range(0, tm, rows)]
    h = jnp.concatenate(h_parts, axis=0)
    d_ff = wd_ref.shape[0]
    a_blocks = []
    for c in range(0, d_ff, col_block):
        if c == 0:
            g = jnp.concatenate([_dot(hp, wg_ref[:, c:c + col_block]) for hp in h_parts], axis=0)
            u = jnp.concatenate([_dot(hp, wu_ref[:, c:c + col_block]) for hp in h_parts], axis=0)
        else:
            g = _dot(h, wg_ref[:, c:c + col_block])
            u = _dot(h, wu_ref[:, c:c + col_block])
        a_blocks.append((g * (1.0 / (1.0 + jnp.exp(-g))) * u).astype(BF16))
    a = jnp.concatenate(a_blocks, axis=1)
    for r in range(0, tm, rows):
        acc = x[r:r + rows] + _dot(a[r:r + rows], wd_ref[:, 0:d])
        if final_norm:
            acc = _rms_norm(acc, fg_ref[...])
        o_ref[r:r + rows, :] = acc
    _cast_blocks(cast_src, cast_dst)


def _ffn(x2, norms, wg, wu, wd, fg, next_weights, *, layer, next_mixer, tm, final_norm):
    m, d = x2.shape
    n_steps = m // tm
    casts = _cast_jobs(next_weights, next_mixer, n_steps)
    cast_specs = [c.specs(lambda i: i) for c in casts]
    outs = pl.pallas_call(
        functools.partial(_ffn_kernel, layer=layer, n_casts=len(casts), col_block=MXU_N,
                          final_norm=final_norm),
        out_shape=[jax.ShapeDtypeStruct((m, d), F32)] + [c.out_shape() for c in casts],
        grid=(n_steps,),
        in_specs=[
            pl.BlockSpec((tm, d), lambda i: (i, 0)),
            _resident(norms.shape),
            _resident(wg.shape),
            _resident(wu.shape),
            _resident(wd.shape),
            _resident(fg.shape),
        ] + [s[0] for s in cast_specs],
        out_specs=[pl.BlockSpec((tm, d), lambda i: (i, 0))] + [s[1] for s in cast_specs],
        compiler_params=_compiler_params(1),
        name="swiglu_final" if final_norm else "swiglu",
    )(x2, norms, wg, wu, wd, fg, *[c.stacked for c in casts])
    return outs[0], outs[1:]


def _sgu_kernel(*refs, layer, n_casts, col_block):
    x_ref, xp_ref, g_ref, win_ref, vg_ref, vb_ref, ws_ref, bs_ref, wout_ref = refs[:9]
    cast_src = refs[9:9 + n_casts]
    o_ref = refs[9 + n_casts]
    cast_dst = refs[10 + n_casts:10 + 2 * n_casts]
    wtri_ref, gate_bias_ref, h_ref, sv_ref = refs[10 + 2 * n_casts:]

    tm, d = x_ref.shape
    e_a = wout_ref.shape[0]
    n_groups = ws_ref.shape[0]
    n_chunks = tm // CHUNK
    n_blocks = e_a // col_block
    groups_per_block = col_block // LANES
    assert n_blocks % OUT_PROJ_BLOCKS == 0

    @pl.when(pl.program_id(0) == 0)
    def _():
        row = lax.broadcasted_iota(jnp.int32, (CHUNK, CHUNK), 0)
        col = lax.broadcasted_iota(jnp.int32, (CHUNK, CHUNK), 1)
        for hg in range(n_groups):
            lanes = slice(hg * LANES, (hg + 1) * LANES)
            w_causal = jnp.where(col <= row, ws_ref[hg], 0.0)
            wtri_ref[hg] = w_causal.astype(BF16)
            b_col = jnp.sum(jnp.where(col == row, bs_ref[hg:hg + 1, :], 0.0), axis=1, keepdims=True)
            gate_bias_ref[:, lanes] = (jnp.sum(w_causal, axis=1, keepdims=True) * vb_ref[:, lanes]
                                       + b_col)
        h_ref[...] = jnp.zeros(h_ref.shape, BF16)
        sv_ref[...] = jnp.zeros(sv_ref.shape, F32)

    def v_proj(b):
        return _dot(h, win_ref[:, e_a + b * col_block:e_a + (b + 1) * col_block])

    def spatial(b):
        vhat = ((v_blocks[b] - mu) * rstd).astype(BF16)
        for gi in range(groups_per_block):
            lanes = slice(gi * LANES, (gi + 1) * LANES)
            rhs = jnp.concatenate(
                [vhat[n * CHUNK:(n + 1) * CHUNK, lanes] for n in range(n_chunks)], axis=1)
            sv_ref[b * groups_per_block + gi] = _dot(wtri_ref[b * groups_per_block + gi], rhs)

    def u_proj(b):
        return _dot(h_prev, win_ref[:, b * col_block:(b + 1) * col_block])

    def gate(b, u):
        y_cols = []
        for gi in range(groups_per_block):
            hg = b * groups_per_block + gi
            lanes = slice(gi * LANES, (gi + 1) * LANES)
            gain = vg_ref[:, hg * LANES:(hg + 1) * LANES]
            bias = gate_bias_ref[:, hg * LANES:(hg + 1) * LANES]
            y_cols.append(jnp.concatenate(
                [u[n * CHUNK:(n + 1) * CHUNK, lanes]
                 * (sv_ref[hg, :, n * LANES:(n + 1) * LANES] * gain + bias)
                 for n in range(n_chunks)], axis=0))
        return jnp.concatenate(y_cols, axis=1).astype(BF16)

    h_prev = h_ref[...]
    zu = u_proj(0)
    h = _rms_norm(x_ref[...], g_ref[layer:layer + 1, :]).astype(BF16)
    zv_blocks = []
    y_blocks = []
    for b in range(n_blocks):
        zv_blocks.append(v_proj(b))
        zu_next = u_proj(b + 1) if b + 1 < n_blocks else None
        y_blocks.append(gate(b, _gelu_tanh(zu)))
        zu = zu_next
    acc = xp_ref[...]
    for k in range(0, n_blocks, OUT_PROJ_BLOCKS):
        acc = acc + _dot(jnp.concatenate(y_blocks[k:k + OUT_PROJ_BLOCKS], axis=1),
                         wout_ref[k * col_block:(k + OUT_PROJ_BLOCKS) * col_block, 0:d])
    o_ref[...] = acc
    v_blocks = []
    sum_v = jnp.zeros((tm, LANES), F32)
    sum_vv = jnp.zeros((tm, LANES), F32)
    for b in range(n_blocks):
        v_blocks.append(_gelu_tanh(zv_blocks[b]))
        for c in range(0, col_block, LANES):
            t = v_blocks[b][:, c:c + LANES]
            sum_v = sum_v + t
            sum_vv = sum_vv + t * t
    mu = jnp.sum(sum_v, axis=-1, keepdims=True) * (1.0 / e_a)
    rstd = lax.rsqrt(jnp.sum(sum_vv, axis=-1, keepdims=True) * (1.0 / e_a) - mu * mu + EPS)
    for b in range(n_blocks):
        spatial(b)
    h_ref[...] = h
    _cast_blocks(cast_src, cast_dst)


def _sgu(x2, norms, win, vg, vb, ws, bs, wout, next_weights, *, layer, mixer, tm):
    m, d = x2.shape
    e_a = wout.shape[0]
    n_groups = ws.shape[1]
    n_tiles = m // tm
    casts = _cast_jobs(next_weights, layer, n_tiles)
    cast_specs = [c.specs(lambda i: i) for c in casts]
    cur = lambda i: (jnp.minimum(i, n_tiles - 1), 0)
    prev = lambda i: (jnp.maximum(i - 1, 0), 0)
    outs = pl.pallas_call(
        functools.partial(_sgu_kernel, layer=layer, n_casts=len(casts), col_block=MXU_N),
        out_shape=[jax.ShapeDtypeStruct((m, d), F32)] + [c.out_shape() for c in casts],
        grid=(n_tiles + 1,),
        in_specs=[
            pl.BlockSpec((tm, d), cur),
            pl.BlockSpec((tm, d), prev),
            _resident(norms.shape),
            _resident(win.shape),
            _resident_layer(vg, mixer),
            _resident_layer(vb, mixer),
            _resident_layer(ws, mixer),
            _resident_layer(bs, mixer),
            _resident(wout.shape),
        ] + [s[0] for s in cast_specs],
        out_specs=[pl.BlockSpec((tm, d), prev)] + [s[1] for s in cast_specs],
        scratch_shapes=[
            pltpu.VMEM((n_groups, CHUNK, CHUNK), BF16),
            pltpu.VMEM((CHUNK, e_a), F32),
            pltpu.VMEM((tm, d), BF16),
            pltpu.VMEM((n_groups, CHUNK, tm), F32),
        ],
        compiler_params=_compiler_params(1),
        name="sgu_mixer",
    )(x2, x2, norms, win, vg, vb, ws, bs, wout, *[c.stacked for c in casts])
    return outs[0], outs[1:]


def _conv_kernel(*refs, layer, n_casts, halo):
    x_ref, g_ref, win_ref, cw_ref, wout_ref = refs[:5]
    cast_src = refs[5:5 + n_casts]
    o_ref = refs[5 + n_casts]
    cast_dst = refs[6 + n_casts:6 + 2 * n_casts]
    z_ref, = refs[6 + 2 * n_casts:]

    tm = x_ref.shape[1]
    e_b = wout_ref.shape[0]
    conv_w = cw_ref.shape[0]
    n_blocks, _, col_block = z_ref.shape

    @pl.when(pl.program_id(1) == 0)
    def _():
        z_ref[:, 0:halo, :] = jnp.zeros((n_blocks, halo, col_block), F32)

    @pl.when(pl.program_id(1) != 0)
    def _():
        z_ref[:, 0:halo, :] = z_ref[:, tm:tm + halo, :]

    x = x_ref[0]
    h = _rms_norm(x, g_ref[layer:layer + 1, :]).astype(BF16)
    y_blocks = []
    for j in range(n_blocks):
        c = j * col_block
        cols = slice(c, c + col_block)
        z_ref[j, halo:halo + tm, :] = (_dot(h, win_ref[:, e_b + c:e_b + c + col_block])
                                       * _dot(h, win_ref[:, 2 * e_b + c:2 * e_b + c + col_block]))
        b_gate = _dot(h, win_ref[:, cols])
        conv = z_ref[j, halo:halo + tm, :] * cw_ref[conv_w - 1:conv_w, cols]
        for k in range(conv_w - 1):
            back = conv_w - 1 - k
            conv = conv + z_ref[j, halo - back:halo - back + tm, :] * cw_ref[k:k + 1, cols]
        y_blocks.append((b_gate * conv).astype(BF16))
    o_ref[0] = x + _dot(jnp.concatenate(y_blocks, axis=1), wout_ref[:, 0:x.shape[1]])
    _cast_blocks(cast_src, cast_dst)


def _short_conv(x3, norms, win, cw, wout, next_weights, *, layer, mixer, tm):
    b, s, d = x3.shape
    e_b = wout.shape[0]
    halo = SUBLANES
    assert cw.shape[1] - 1 <= halo
    tiles_per_seq = s // tm
    casts = _cast_jobs(next_weights, layer, b * tiles_per_seq)
    cast_specs = [c.specs(lambda i, j: i * tiles_per_seq + j) for c in casts]
    outs = pl.pallas_call(
        functools.partial(_conv_kernel, layer=layer, n_casts=len(casts), halo=halo),
        out_shape=[jax.ShapeDtypeStruct((b, s, d), F32)] + [c.out_shape() for c in casts],
        grid=(b, tiles_per_seq),
        in_specs=[
            pl.BlockSpec((1, tm, d), lambda i, j: (i, j, 0)),
            _resident(norms.shape),
            _resident(win.shape),
            _resident_layer(cw, mixer),
            _resident(wout.shape),
        ] + [s[0] for s in cast_specs],
        out_specs=[pl.BlockSpec((1, tm, d), lambda i, j: (i, j, 0))] + [s[1] for s in cast_specs],
        scratch_shapes=[pltpu.VMEM((e_b // MXU_N, halo + tm, MXU_N), F32)],
        compiler_params=_compiler_params(2),
        name="short_conv_mixer",
    )(x3, norms, win, cw, wout, *[c.stacked for c in casts])
    return outs[0], outs[1:]


def kernel(x, mix_norm, ffn_norm, a_w_in, a_v_gain, a_v_bias, a_w_s, a_b_s, a_w_out,
           b_w_in, b_conv_w, b_w_out, ffn_w_gate, ffn_w_up, ffn_w_down, final_norm):
    b, s, d = x.shape
    depth = mix_norm.shape[0]
    tm = 1024
    tm_sgu = 512
    assert s % tm == 0 and s % tm_sgu == 0 and tm_sgu % CHUNK == 0 and a_w_s.shape[-1] == CHUNK
    assert a_w_s.shape[1] * LANES == a_w_out.shape[1]

    fg = final_norm.reshape(1, d)
    a_v_gain3 = a_v_gain.reshape(a_v_gain.shape[0], 1, -1)
    a_v_bias3 = a_v_bias.reshape(a_v_bias.shape[0], 1, -1)
    ffn_weights = (ffn_w_gate, ffn_w_up, ffn_w_down)

    def mixer_weights(i):
        return (a_w_in, a_w_out) if i % 2 == 0 else (b_w_in, b_w_out)

    ready = _cast_call(mixer_weights(0), 0)
    x2 = x.reshape(b * s, d)
    for i in range(depth):
        j = i // 2
        if i % 2 == 0:
            x2, ready = _sgu(x2, mix_norm, ready[0], a_v_gain3, a_v_bias3, a_w_s, a_b_s, ready[1],
                             ffn_weights, layer=i, mixer=j, tm=tm_sgu)
        else:
            x3, ready = _short_conv(x2.reshape(b, s, d), mix_norm, ready[0], b_conv_w, ready[1],
                                    ffn_weights, layer=i, mixer=j, tm=tm)
            x2 = x3.reshape(b * s, d)
        last = i == depth - 1
        x2, ready = _ffn(x2, ffn_norm, *ready, fg, () if last else mixer_weights(i + 1),
                         layer=i, next_mixer=(i + 1) // 2, tm=tm, final_norm=last)
    return x2.reshape(b, s, d)
```

```python
import functools
import math
from typing import NamedTuple

import jax
import jax.numpy as jnp
from jax import lax
from jax.experimental import pallas as pl
from jax.experimental.pallas import tpu as pltpu

EPS = 1e-6
CHUNK = 128
LANES = 128
SUBLANES = 8
BF16_ROWS = 16
MXU_N = 256
OUT_PROJ_BLOCKS = 4
DOWN_PROJ_ROW_PARTS = 4
VMEM_LIMIT_BYTES = 56 * 1024 * 1024

BF16 = jnp.bfloat16
F32 = jnp.float32


def _dot(a, b):
    return jnp.dot(a, b, preferred_element_type=F32)


def _rms_norm(x, g):
    y = x * lax.rsqrt(jnp.mean(x * x, axis=-1, keepdims=True) + EPS)
    return y * g


def _gelu_tanh(x):
    k = -2.0 * math.sqrt(2.0 / math.pi) * math.log2(math.e)
    return x / (1.0 + jnp.exp2(x * (k + (k * 0.044715) * (x * x))))


def _odd_tile_pitch(n):
    assert n % LANES == 0
    return n if (n // LANES) % 2 == 1 else n + LANES


def _resident(shape):
    return pl.BlockSpec(shape, lambda *_: (0,) * len(shape), pipeline_mode=pl.Buffered(1))


def _resident_layer(stacked, layer):
    rest = stacked.shape[1:]
    return pl.BlockSpec((None,) + rest, lambda *_: (layer,) + (0,) * len(rest),
                        pipeline_mode=pl.Buffered(1))


def _compiler_params(n_grid_axes):
    return pltpu.CompilerParams(dimension_semantics=("arbitrary",) * n_grid_axes,
                                vmem_limit_bytes=VMEM_LIMIT_BYTES)


class _Cast(NamedTuple):
    stacked: jax.Array
    layer: int
    rows: int

    @property
    def n_blocks(self):
        n_rows = self.stacked.shape[1]
        assert n_rows % self.rows == 0 and self.rows % BF16_ROWS == 0
        return n_rows // self.rows

    def out_shape(self):
        _, n_rows, n_cols = self.stacked.shape
        return jax.ShapeDtypeStruct((n_rows, _odd_tile_pitch(n_cols)), BF16)

    def specs(self, linear_step):
        n_cols = self.stacked.shape[2]
        block = lambda *g: jnp.minimum(linear_step(*g), self.n_blocks - 1)
        return (pl.BlockSpec((None, self.rows, n_cols), lambda *g: (self.layer, block(*g), 0)),
                pl.BlockSpec((self.rows, _odd_tile_pitch(n_cols)), lambda *g: (block(*g), 0)))


def _cast_jobs(weights, layer, n_steps):
    jobs = []
    for w in weights:
        n_rows = w.shape[1]
        rows = next(r for r in range(BF16_ROWS, n_rows + 1, BF16_ROWS)
                    if n_rows % r == 0 and n_rows // r <= n_steps)
        jobs.append(_Cast(w, layer, rows))
    return jobs


def _cast_blocks(src_refs, dst_refs):
    for src, dst in zip(src_refs, dst_refs):
        n = src.shape[1]
        dst[:, 0:n] = src[...].astype(BF16)
        if dst.shape[1] > n:
            dst[:, n:] = jnp.zeros((dst.shape[0], dst.shape[1] - n), BF16)


def _cast_kernel(*refs):
    n = len(refs) // 2
    _cast_blocks(refs[:n], refs[n:])


def _cast_call(weights, layer):
    n_steps = 8
    casts = _cast_jobs(weights, layer, n_steps)
    specs = [c.specs(lambda i: i) for c in casts]
    return pl.pallas_call(
        _cast_kernel,
        out_shape=[c.out_shape() for c in casts],
        grid=(n_steps,),
        in_specs=[s[0] for s in specs],
        out_specs=[s[1] for s in specs],
        compiler_params=_compiler_params(1),
        name="cast_weights",
    )(*[c.stacked for c in casts])


def _ffn_kernel(*refs, layer, n_casts, col_block, final_norm):
    x_ref, g_ref, wg_ref, wu_ref, wd_ref, fg_ref = refs[:6]
    cast_src = refs[6:6 + n_casts]
    o_ref = refs[6 + n_casts]
    cast_dst = refs[7 + n_casts:]

    x = x_ref[...]
    d = x.shape[1]
    h = _rms_norm(x, g_ref[layer:layer + 1, :]).astype(BF16)
    d_ff = wd_ref.shape[0]
    a_blocks = []
    for c in range(0, d_ff, col_block):
        g = _dot(h, wg_ref[:, c:c + col_block])
        u = _dot(h, wu_ref[:, c:c + col_block])
        a_blocks.append((g * (1.0 / (1.0 + jnp.exp(-g))) * u).astype(BF16))
    a = jnp.concatenate(a_blocks, axis=1)
    rows = x.shape[0] // DOWN_PROJ_ROW_PARTS
    for r in range(0, x.shape[0], rows):
        acc = x[r:r + rows] + _dot(a[r:r + rows], wd_ref[:, 0:d])
        if final_norm:
            acc = _rms_norm(acc, fg_ref[...])
        o_ref[r:r + rows, :] = acc
    _cast_blocks(cast_src, cast_dst)


def _ffn(x2, norms, wg, wu, wd, fg, next_weights, *, layer, next_mixer, tm, final_norm):
    m, d = x2.shape
    n_steps = m // tm
    casts = _cast_jobs(next_weights, next_mixer, n_steps)
    cast_specs = [c.specs(lambda i: i) for c in casts]
    outs = pl.pallas_call(
        functools.partial(_ffn_kernel, layer=layer, n_casts=len(casts), col_block=MXU_N,
                          final_norm=final_norm),
        out_shape=[jax.ShapeDtypeStruct((m, d), F32)] + [c.out_shape() for c in casts],
        grid=(n_steps,),
        in_specs=[
            pl.BlockSpec((tm, d), lambda i: (i, 0)),
            _resident(norms.shape),
            _resident(wg.shape),
            _resident(wu.shape),
            _resident(wd.shape),
            _resident(fg.shape),
        ] + [s[0] for s in cast_specs],
        out_specs=[pl.BlockSpec((tm, d), lambda i: (i, 0))] + [s[1] for s in cast_specs],
        compiler_params=_compiler_params(1),
        name="swiglu_final" if final_norm else "swiglu",
    )(x2, norms, wg, wu, wd, fg, *[c.stacked for c in casts])
    return outs[0], outs[1:]


def _sgu_kernel(*refs, layer, n_casts, col_block):
    x_ref, xp_ref, g_ref, win_ref, vg_ref, vb_ref, ws_ref, bs_ref, wout_ref = refs[:9]
    cast_src = refs[9:9 + n_casts]
    o_ref = refs[9 + n_casts]
    cast_dst = refs[10 + n_casts:10 + 2 * n_casts]
    wtri_ref, gate_bias_ref, h_ref, sv_ref = refs[10 + 2 * n_casts:]

    tm, d = x_ref.shape
    e_a = wout_ref.shape[0]
    n_groups = ws_ref.shape[0]
    n_chunks = tm // CHUNK
    n_blocks = e_a // col_block
    groups_per_block = col_block // LANES
    assert n_blocks % OUT_PROJ_BLOCKS == 0

    @pl.when(pl.program_id(0) == 0)
    def _():
        row = lax.broadcasted_iota(jnp.int32, (CHUNK, CHUNK), 0)
        col = lax.broadcasted_iota(jnp.int32, (CHUNK, CHUNK), 1)
        for hg in range(n_groups):
            lanes = slice(hg * LANES, (hg + 1) * LANES)
            w_causal = jnp.where(col <= row, ws_ref[hg], 0.0)
            wtri_ref[hg] = w_causal.astype(BF16)
            b_col = jnp.sum(jnp.where(col == row, bs_ref[hg:hg + 1, :], 0.0), axis=1, keepdims=True)
            gate_bias_ref[:, lanes] = (jnp.sum(w_causal, axis=1, keepdims=True) * vb_ref[:, lanes]
                                       + b_col)
        h_ref[...] = jnp.zeros(h_ref.shape, BF16)
        sv_ref[...] = jnp.zeros(sv_ref.shape, F32)

    def v_proj(b):
        return _dot(h, win_ref[:, e_a + b * col_block:e_a + (b + 1) * col_block])

    def spatial(b):
        vhat = ((v_blocks[b] - mu) * rstd).astype(BF16)
        for gi in range(groups_per_block):
            lanes = slice(gi * LANES, (gi + 1) * LANES)
            rhs = jnp.concatenate(
                [vhat[n * CHUNK:(n + 1) * CHUNK, lanes] for n in range(n_chunks)], axis=1)
            sv_ref[b * groups_per_block + gi] = _dot(wtri_ref[b * groups_per_block + gi], rhs)

    def u_proj(b):
        return _dot(h_prev, win_ref[:, b * col_block:(b + 1) * col_block])

    def gate(b, u):
        y_cols = []
        for gi in range(groups_per_block):
            hg = b * groups_per_block + gi
            lanes = slice(gi * LANES, (gi + 1) * LANES)
            gain = vg_ref[:, hg * LANES:(hg + 1) * LANES]
            bias = gate_bias_ref[:, hg * LANES:(hg + 1) * LANES]
            y_cols.append(jnp.concatenate(
                [u[n * CHUNK:(n + 1) * CHUNK, lanes]
                 * (sv_ref[hg, :, n * LANES:(n + 1) * LANES] * gain + bias)
                 for n in range(n_chunks)], axis=0))
        return jnp.concatenate(y_cols, axis=1).astype(BF16)

    h_prev = h_ref[...]
    zu = u_proj(0)
    h = _rms_norm(x_ref[...], g_ref[layer:layer + 1, :]).astype(BF16)
    zv_blocks = []
    y_blocks = []
    for b in range(n_blocks):
        zv_blocks.append(v_proj(b))
        zu_next = u_proj(b + 1) if b + 1 < n_blocks else None
        y_blocks.append(gate(b, _gelu_tanh(zu)))
        zu = zu_next
    acc = xp_ref[...]
    for k in range(0, n_blocks, OUT_PROJ_BLOCKS):
        acc = acc + _dot(jnp.concatenate(y_blocks[k:k + OUT_PROJ_BLOCKS], axis=1),
                         wout_ref[k * col_block:(k + OUT_PROJ_BLOCKS) * col_block, 0:d])
    o_ref[...] = acc
    v_blocks = []
    sum_v = jnp.zeros((tm, LANES), F32)
    sum_vv = jnp.zeros((tm, LANES), F32)
    for b in range(n_blocks):
        v_blocks.append(_gelu_tanh(zv_blocks[b]))
        for c in range(0, col_block, LANES):
            t = v_blocks[b][:, c:c + LANES]
            sum_v = sum_v + t
            sum_vv = sum_vv + t * t
    mu = jnp.sum(sum_v, axis=-1, keepdims=True) * (1.0 / e_a)
    var = jnp.maximum(jnp.sum(sum_vv, axis=-1, keepdims=True) * (1.0 / e_a) - mu * mu, 0.0)
    rstd = lax.rsqrt(var + EPS)
    for b in range(n_blocks):
        spatial(b)
    h_ref[...] = h
    _cast_blocks(cast_src, cast_dst)


def _sgu(x2, norms, win, vg, vb, ws, bs, wout, next_weights, *, layer, mixer, tm):
    m, d = x2.shape
    e_a = wout.shape[0]
    n_groups = ws.shape[1]
    n_tiles = m // tm
    casts = _cast_jobs(next_weights, layer, n_tiles)
    cast_specs = [c.specs(lambda i: i) for c in casts]
    cur = lambda i: (jnp.minimum(i, n_tiles - 1), 0)
    prev = lambda i: (jnp.maximum(i - 1, 0), 0)
    outs = pl.pallas_call(
        functools.partial(_sgu_kernel, layer=layer, n_casts=len(casts), col_block=MXU_N),
        out_shape=[jax.ShapeDtypeStruct((m, d), F32)] + [c.out_shape() for c in casts],
        grid=(n_tiles + 1,),
        in_specs=[
            pl.BlockSpec((tm, d), cur),
            pl.BlockSpec((tm, d), prev),
            _resident(norms.shape),
            _resident(win.shape),
            _resident_layer(vg, mixer),
            _resident_layer(vb, mixer),
            _resident_layer(ws, mixer),
            _resident_layer(bs, mixer),
            _resident(wout.shape),
        ] + [s[0] for s in cast_specs],
        out_specs=[pl.BlockSpec((tm, d), prev)] + [s[1] for s in cast_specs],
        scratch_shapes=[
            pltpu.VMEM((n_groups, CHUNK, CHUNK), BF16),
            pltpu.VMEM((CHUNK, e_a), F32),
            pltpu.VMEM((tm, d), BF16),
            pltpu.VMEM((n_groups, CHUNK, tm), F32),
        ],
        compiler_params=_compiler_params(1),
        name="sgu_mixer",
    )(x2, x2, norms, win, vg, vb, ws, bs, wout, *[c.stacked for c in casts])
    return outs[0], outs[1:]


def _conv_kernel(*refs, layer, n_casts, halo):
    x_ref, g_ref, win_ref, cw_ref, wout_ref = refs[:5]
    cast_src = refs[5:5 + n_casts]
    o_ref = refs[5 + n_casts]
    cast_dst = refs[6 + n_casts:6 + 2 * n_casts]
    z_ref, = refs[6 + 2 * n_casts:]

    tm = x_ref.shape[1]
    e_b = wout_ref.shape[0]
    conv_w = cw_ref.shape[0]
    n_blocks, _, col_block = z_ref.shape

    @pl.when(pl.program_id(1) == 0)
    def _():
        z_ref[:, 0:halo, :] = jnp.zeros((n_blocks, halo, col_block), F32)

    @pl.when(pl.program_id(1) != 0)
    def _():
        z_ref[:, 0:halo, :] = z_ref[:, tm:tm + halo, :]

    x = x_ref[0]
    h = _rms_norm(x, g_ref[layer:layer + 1, :]).astype(BF16)
    y_blocks = []
    for j in range(n_blocks):
        c = j * col_block
        cols = slice(c, c + col_block)
        z_ref[j, halo:halo + tm, :] = (_dot(h, win_ref[:, e_b + c:e_b + c + col_block])
                                       * _dot(h, win_ref[:, 2 * e_b + c:2 * e_b + c + col_block]))
        b_gate = _dot(h, win_ref[:, cols])
        conv = z_ref[j, halo:halo + tm, :] * cw_ref[conv_w - 1:conv_w, cols]
        for k in range(conv_w - 1):
            back = conv_w - 1 - k
            conv = conv + z_ref[j, halo - back:halo - back + tm, :] * cw_ref[k:k + 1, cols]
        y_blocks.append((b_gate * conv).astype(BF16))
    o_ref[0] = x + _dot(jnp.concatenate(y_blocks, axis=1), wout_ref[:, 0:x.shape[1]])
    _cast_blocks(cast_src, cast_dst)


def _short_conv(x3, norms, win, cw, wout, next_weights, *, layer, mixer, tm):
    b, s, d = x3.shape
    e_b = wout.shape[0]
    halo = SUBLANES
    assert cw.shape[1] - 1 <= halo
    tiles_per_seq = s // tm
    casts = _cast_jobs(next_weights, layer, b * tiles_per_seq)
    cast_specs = [c.specs(lambda i, j: i * tiles_per_seq + j) for c in casts]
    outs = pl.pallas_call(
        functools.partial(_conv_kernel, layer=layer, n_casts=len(casts), halo=halo),
        out_shape=[jax.ShapeDtypeStruct((b, s, d), F32)] + [c.out_shape() for c in casts],
        grid=(b, tiles_per_seq),
        in_specs=[
            pl.BlockSpec((1, tm, d), lambda i, j: (i, j, 0)),
            _resident(norms.shape),
            _resident(win.shape),
            _resident_layer(cw, mixer),
            _resident(wout.shape),
        ] + [s[0] for s in cast_specs],
        out_specs=[pl.BlockSpec((1, tm, d), lambda i, j: (i, j, 0))] + [s[1] for s in cast_specs],
        scratch_shapes=[pltpu.VMEM((e_b // MXU_N, halo + tm, MXU_N), F32)],
        compiler_params=_compiler_params(2),
        name="short_conv_mixer",
    )(x3, norms, win, cw, wout, *[c.stacked for c in casts])
    return outs[0], outs[1:]


def kernel(x, mix_norm, ffn_norm, a_w_in, a_v_gain, a_v_bias, a_w_s, a_b_s, a_w_out,
           b_w_in, b_conv_w, b_w_out, ffn_w_gate, ffn_w_up, ffn_w_down, final_norm):
    b, s, d = x.shape
    depth = mix_norm.shape[0]
    tm = 1024
    tm_sgu = 512
    assert s % tm == 0 and s % tm_sgu == 0 and tm_sgu % CHUNK == 0 and a_w_s.shape[-1] == CHUNK
    assert a_w_s.shape[1] * LANES == a_w_out.shape[1]

    fg = final_norm.reshape(1, d)
    a_v_gain3 = a_v_gain.reshape(a_v_gain.shape[0], 1, -1)
    a_v_bias3 = a_v_bias.reshape(a_v_bias.shape[0], 1, -1)
    ffn_weights = (ffn_w_gate, ffn_w_up, ffn_w_down)

    def mixer_weights(i):
        return (a_w_in, a_w_out) if i % 2 == 0 else (b_w_in, b_w_out)

    ready = _cast_call(mixer_weights(0), 0)
    x2 = x.reshape(b * s, d)
    for i in range(depth):
        j = i // 2
        if i % 2 == 0:
            x2, ready = _sgu(x2, mix_norm, ready[0], a_v_gain3, a_v_bias3, a_w_s, a_b_s, ready[1],
                             ffn_weights, layer=i, mixer=j, tm=tm_sgu)
        else:
            x3, ready = _short_conv(x2.reshape(b, s, d), mix_norm, ready[0], b_conv_w, ready[1],
                                    ffn_weights, layer=i, mixer=j, tm=tm)
            x2 = x3.reshape(b * s, d)
        last = i == depth - 1
        x2, ready = _ffn(x2, ffn_norm, *ready, fg, () if last else mixer_weights(i + 1),
                         layer=i, next_mixer=(i + 1) // 2, tm=tm, final_norm=last)
    return x2.reshape(b, s, d)
```

```python
import functools
import math
from typing import NamedTuple, Optional

import jax
import jax.numpy as jnp
from jax import lax
from jax.experimental import pallas as pl
from jax.experimental.pallas import tpu as pltpu

EPS = 1e-6
CHUNK = 128
LANES = 128
SUBLANES = 8
BF16_ROWS = 16
MXU_N = 256
OUT_PROJ_BLOCKS = 4
DOWN_PROJ_ROW_PARTS = 4
VMEM_LIMIT_BYTES = 56 * 1024 * 1024

BF16 = jnp.bfloat16
F32 = jnp.float32


def _dot(a, b):
    return jnp.dot(a, b, preferred_element_type=F32)


def _rms_scale(x):
    return x * lax.rsqrt(jnp.mean(x * x, axis=-1, keepdims=True) + EPS)


def _gelu_tanh(x):
    k = -2.0 * math.sqrt(2.0 / math.pi) * math.log2(math.e)
    return x / (1.0 + jnp.exp2(x * (k + (k * 0.044715) * (x * x))))


def _odd_tile_pitch(n):
    assert n % LANES == 0
    return n if (n // LANES) % 2 == 1 else n + LANES


def _resident(shape):
    return pl.BlockSpec(shape, lambda *_: (0,) * len(shape), pipeline_mode=pl.Buffered(1))


def _resident_layer(stacked, layer):
    rest = stacked.shape[1:]
    return pl.BlockSpec((None,) + rest, lambda *_: (layer,) + (0,) * len(rest),
                        pipeline_mode=pl.Buffered(1))


def _compiler_params(n_grid_axes):
    return pltpu.CompilerParams(dimension_semantics=("arbitrary",) * n_grid_axes,
                                vmem_limit_bytes=VMEM_LIMIT_BYTES)


class _Cast(NamedTuple):
    stacked: jax.Array
    layer: int
    rows: int
    row_gain: Optional[jax.Array]

    @property
    def n_blocks(self):
        n_rows = self.stacked.shape[1]
        assert n_rows % self.rows == 0 and self.rows % BF16_ROWS == 0
        return n_rows // self.rows

    def out_shape(self):
        _, n_rows, n_cols = self.stacked.shape
        return jax.ShapeDtypeStruct((n_rows, _odd_tile_pitch(n_cols)), BF16)

    def operands(self):
        return (self.stacked,) if self.row_gain is None else (self.stacked, self.row_gain)

    def specs(self, linear_step):
        n_cols = self.stacked.shape[2]
        block = lambda *g: jnp.minimum(linear_step(*g), self.n_blocks - 1)
        ins = [pl.BlockSpec((None, self.rows, n_cols), lambda *g: (self.layer, block(*g), 0))]
        if self.row_gain is not None:
            ins.append(pl.BlockSpec((self.rows, 1), lambda *g: (block(*g), 0)))
        return ins, pl.BlockSpec((self.rows, _odd_tile_pitch(n_cols)), lambda *g: (block(*g), 0))


def _cast_jobs(weights, layer, n_steps):
    jobs = []
    for w, row_gain in weights:
        n_rows = w.shape[1]
        rows = next(r for r in range(BF16_ROWS, n_rows + 1, BF16_ROWS)
                    if n_rows % r == 0 and n_rows // r <= n_steps)
        jobs.append(_Cast(w, layer, rows, row_gain))
    return jobs


def _cast_plumbing(casts, linear_step):
    specs = [c.specs(linear_step) for c in casts]
    return ([a for c in casts for a in c.operands()],
            [s for ins, _ in specs for s in ins],
            [out for _, out in specs],
            [c.out_shape() for c in casts],
            tuple(len(c.operands()) for c in casts))


def _cast_blocks(in_refs, dst_refs, refs_per_job):
    in_refs = list(in_refs)
    for dst, n_in in zip(dst_refs, refs_per_job):
        src, gains = in_refs[0], in_refs[1:n_in]
        del in_refs[:n_in]
        w = src[...]
        for gain in gains:
            w = w * gain[...]
        n = src.shape[1]
        dst[:, 0:n] = w.astype(BF16)
        if dst.shape[1] > n:
            dst[:, n:] = jnp.zeros((dst.shape[0], dst.shape[1] - n), BF16)


def _cast_kernel(*refs, refs_per_job):
    n_in = sum(refs_per_job)
    _cast_blocks(refs[:n_in], refs[n_in:], refs_per_job)


def _cast_call(weights, layer):
    n_steps = 8
    operands, in_specs, out_specs, out_shapes, refs_per_job = _cast_plumbing(
        _cast_jobs(weights, layer, n_steps), lambda i: i)
    return pl.pallas_call(
        functools.partial(_cast_kernel, refs_per_job=refs_per_job),
        out_shape=out_shapes,
        grid=(n_steps,),
        in_specs=in_specs,
        out_specs=out_specs,
        compiler_params=_compiler_params(1),
        name="cast_weights",
    )(*operands)


def _ffn_kernel(*refs, refs_per_job, col_block, final_norm):
    n_fixed = 5 if final_norm else 4
    n_cast_in = sum(refs_per_job)
    x_ref, wg_ref, wu_ref, wd_ref = refs[:4]
    cast_src = refs[n_fixed:n_fixed + n_cast_in]
    o_ref = refs[n_fixed + n_cast_in]
    cast_dst = refs[n_fixed + n_cast_in + 1:]

    x = x_ref[...]
    d = x.shape[1]
    h = _rms_scale(x).astype(BF16)
    d_ff = wd_ref.shape[0]
    a_blocks = []
    for c in range(0, d_ff, col_block):
        g = _dot(h, wg_ref[:, c:c + col_block])
        u = _dot(h, wu_ref[:, c:c + col_block])
        a_blocks.append((g * (1.0 / (1.0 + jnp.exp(-g))) * u).astype(BF16))
    a = jnp.concatenate(a_blocks, axis=1)
    rows = x.shape[0] // DOWN_PROJ_ROW_PARTS
    for r in range(0, x.shape[0], rows):
        acc = x[r:r + rows] + _dot(a[r:r + rows], wd_ref[:, 0:d])
        if final_norm:
            acc = _rms_scale(acc) * refs[4][...]
        o_ref[r:r + rows, :] = acc
    _cast_blocks(cast_src, cast_dst, refs_per_job)


def _ffn(x2, wg, wu, wd, final_gain, next_weights, *, next_layer, tm):
    m, d = x2.shape
    n_steps = m // tm
    final_norm = final_gain is not None
    operands, cast_in, cast_out, cast_shapes, refs_per_job = _cast_plumbing(
        _cast_jobs(next_weights, next_layer, n_steps), lambda i: i)
    fixed = [x2, wg, wu, wd] + ([final_gain] if final_norm else [])
    outs = pl.pallas_call(
        functools.partial(_ffn_kernel, refs_per_job=refs_per_job, col_block=MXU_N,
                          final_norm=final_norm),
        out_shape=[jax.ShapeDtypeStruct((m, d), F32)] + cast_shapes,
        grid=(n_steps,),
        in_specs=[pl.BlockSpec((tm, d), lambda i: (i, 0))]
        + [_resident(a.shape) for a in fixed[1:]] + cast_in,
        out_specs=[pl.BlockSpec((tm, d), lambda i: (i, 0))] + cast_out,
        compiler_params=_compiler_params(1),
        name="swiglu_final" if final_norm else "swiglu",
    )(*fixed, *operands)
    return outs[0], outs[1:]


def _sgu_kernel(*refs, refs_per_job, col_block):
    n_cast_in, n_casts = sum(refs_per_job), len(refs_per_job)
    x_ref, xp_ref, win_ref, vg_ref, vb_ref, ws_ref, bs_ref, wout_ref = refs[:8]
    cast_src = refs[8:8 + n_cast_in]
    o_ref = refs[8 + n_cast_in]
    cast_dst = refs[9 + n_cast_in:9 + n_cast_in + n_casts]
    wtri_ref, gate_bias_ref, h_ref, sv_ref = refs[9 + n_cast_in + n_casts:]

    tm, d = x_ref.shape
    e_a = wout_ref.shape[0]
    n_groups = ws_ref.shape[0]
    n_chunks = tm // CHUNK
    n_blocks = e_a // col_block
    groups_per_block = col_block // LANES
    assert n_blocks % OUT_PROJ_BLOCKS == 0

    @pl.when(pl.program_id(0) == 0)
    def _():
        row = lax.broadcasted_iota(jnp.int32, (CHUNK, CHUNK), 0)
        col = lax.broadcasted_iota(jnp.int32, (CHUNK, CHUNK), 1)
        for hg in range(n_groups):
            lanes = slice(hg * LANES, (hg + 1) * LANES)
            w_causal = jnp.where(col <= row, ws_ref[hg], 0.0)
            wtri_ref[hg] = w_causal.astype(BF16)
            b_col = jnp.sum(jnp.where(col == row, bs_ref[hg:hg + 1, :], 0.0), axis=1, keepdims=True)
            gate_bias_ref[:, lanes] = (jnp.sum(w_causal, axis=1, keepdims=True) * vb_ref[:, lanes]
                                       + b_col)
        h_ref[...] = jnp.zeros(h_ref.shape, BF16)
        sv_ref[...] = jnp.zeros(sv_ref.shape, F32)

    def v_proj(b):
        return _dot(h, win_ref[:, e_a + b * col_block:e_a + (b + 1) * col_block])

    def spatial(b):
        vhat = ((v_blocks[b] - mu) * rstd).astype(BF16)
        for gi in range(groups_per_block):
            lanes = slice(gi * LANES, (gi + 1) * LANES)
            rhs = jnp.concatenate(
                [vhat[n * CHUNK:(n + 1) * CHUNK, lanes] for n in range(n_chunks)], axis=1)
            sv_ref[b * groups_per_block + gi] = _dot(wtri_ref[b * groups_per_block + gi], rhs)

    def u_proj(b):
        return _dot(h_prev, win_ref[:, b * col_block:(b + 1) * col_block])

    def gate(b, u):
        y_cols = []
        for gi in range(groups_per_block):
            hg = b * groups_per_block + gi
            lanes = slice(gi * LANES, (gi + 1) * LANES)
            gain = vg_ref[:, hg * LANES:(hg + 1) * LANES]
            bias = gate_bias_ref[:, hg * LANES:(hg + 1) * LANES]
            y_cols.append(jnp.concatenate(
                [u[n * CHUNK:(n + 1) * CHUNK, lanes]
                 * (sv_ref[hg, :, n * LANES:(n + 1) * LANES] * gain + bias)
                 for n in range(n_chunks)], axis=0))
        return jnp.concatenate(y_cols, axis=1).astype(BF16)

    h_prev = h_ref[...]
    zu = u_proj(0)
    h = _rms_scale(x_ref[...]).astype(BF16)
    zv_blocks = []
    y_blocks = []
    for b in range(n_blocks):
        zv_blocks.append(v_proj(b))
        zu_next = u_proj(b + 1) if b + 1 < n_blocks else None
        y_blocks.append(gate(b, _gelu_tanh(zu)))
        zu = zu_next
    acc = xp_ref[...]
    for k in range(0, n_blocks, OUT_PROJ_BLOCKS):
        acc = acc + _dot(jnp.concatenate(y_blocks[k:k + OUT_PROJ_BLOCKS], axis=1),
                         wout_ref[k * col_block:(k + OUT_PROJ_BLOCKS) * col_block, 0:d])
    o_ref[...] = acc
    v_blocks = []
    sum_v = jnp.zeros((tm, LANES), F32)
    sum_vv = jnp.zeros((tm, LANES), F32)
    for b in range(n_blocks):
        v_blocks.append(_gelu_tanh(zv_blocks[b]))
        for c in range(0, col_block, LANES):
            t = v_blocks[b][:, c:c + LANES]
            sum_v = sum_v + t
            sum_vv = sum_vv + t * t
    mu = jnp.sum(sum_v, axis=-1, keepdims=True) * (1.0 / e_a)
    var = jnp.maximum(jnp.sum(sum_vv, axis=-1, keepdims=True) * (1.0 / e_a) - mu * mu, 0.0)
    rstd = lax.rsqrt(var + EPS)
    for b in range(n_blocks):
        spatial(b)
    h_ref[...] = h
    _cast_blocks(cast_src, cast_dst, refs_per_job)


def _sgu(x2, win, vg, vb, ws, bs, wout, next_weights, *, next_layer, mixer, tm):
    m, d = x2.shape
    e_a = wout.shape[0]
    n_groups = ws.shape[1]
    n_tiles = m // tm
    operands, cast_in, cast_out, cast_shapes, refs_per_job = _cast_plumbing(
        _cast_jobs(next_weights, next_layer, n_tiles), lambda i: i)
    cur = lambda i: (jnp.minimum(i, n_tiles - 1), 0)
    prev = lambda i: (jnp.maximum(i - 1, 0), 0)
    outs = pl.pallas_call(
        functools.partial(_sgu_kernel, refs_per_job=refs_per_job, col_block=MXU_N),
        out_shape=[jax.ShapeDtypeStruct((m, d), F32)] + cast_shapes,
        grid=(n_tiles + 1,),
        in_specs=[
            pl.BlockSpec((tm, d), cur),
            pl.BlockSpec((tm, d), prev),
            _resident(win.shape),
            _resident_layer(vg, mixer),
            _resident_layer(vb, mixer),
            _resident_layer(ws, mixer),
            _resident_layer(bs, mixer),
            _resident(wout.shape),
        ] + cast_in,
        out_specs=[pl.BlockSpec((tm, d), prev)] + cast_out,
        scratch_shapes=[
            pltpu.VMEM((n_groups, CHUNK, CHUNK), BF16),
            pltpu.VMEM((CHUNK, e_a), F32),
            pltpu.VMEM((tm, d), BF16),
            pltpu.VMEM((n_groups, CHUNK, tm), F32),
        ],
        compiler_params=_compiler_params(1),
        name="sgu_mixer",
    )(x2, x2, win, vg, vb, ws, bs, wout, *operands)
    return outs[0], outs[1:]


def _conv_kernel(*refs, refs_per_job, halo):
    n_cast_in, n_casts = sum(refs_per_job), len(refs_per_job)
    x_ref, win_ref, cw_ref, wout_ref = refs[:4]
    cast_src = refs[4:4 + n_cast_in]
    o_ref = refs[4 + n_cast_in]
    cast_dst = refs[5 + n_cast_in:5 + n_cast_in + n_casts]
    z_ref, = refs[5 + n_cast_in + n_casts:]

    tm = x_ref.shape[1]
    e_b = wout_ref.shape[0]
    conv_w = cw_ref.shape[0]
    n_blocks, _, col_block = z_ref.shape

    @pl.when(pl.program_id(1) == 0)
    def _():
        z_ref[:, 0:halo, :] = jnp.zeros((n_blocks, halo, col_block), F32)

    @pl.when(pl.program_id(1) != 0)
    def _():
        z_ref[:, 0:halo, :] = z_ref[:, tm:tm + halo, :]

    x = x_ref[0]
    h = _rms_scale(x).astype(BF16)
    y_blocks = []
    for j in range(n_blocks):
        c = j * col_block
        cols = slice(c, c + col_block)
        z_ref[j, halo:halo + tm, :] = (_dot(h, win_ref[:, e_b + c:e_b + c + col_block])
                                       * _dot(h, win_ref[:, 2 * e_b + c:2 * e_b + c + col_block]))
        b_gate = _dot(h, win_ref[:, cols])
        conv = z_ref[j, halo:halo + tm, :] * cw_ref[conv_w - 1:conv_w, cols]
        for k in range(conv_w - 1):
            back = conv_w - 1 - k
            conv = conv + z_ref[j, halo - back:halo - back + tm, :] * cw_ref[k:k + 1, cols]
        y_blocks.append((b_gate * conv).astype(BF16))
    o_ref[0] = x + _dot(jnp.concatenate(y_blocks, axis=1), wout_ref[:, 0:x.shape[1]])
    _cast_blocks(cast_src, cast_dst, refs_per_job)


def _short_conv(x3, win, cw, wout, next_weights, *, next_layer, mixer, tm):
    b, s, d = x3.shape
    e_b = wout.shape[0]
    halo = SUBLANES
    assert cw.shape[1] - 1 <= halo
    tiles_per_seq = s // tm
    operands, cast_in, cast_out, cast_shapes, refs_per_job = _cast_plumbing(
        _cast_jobs(next_weights, next_layer, b * tiles_per_seq),
        lambda i, j: i * tiles_per_seq + j)
    outs = pl.pallas_call(
        functools.partial(_conv_kernel, refs_per_job=refs_per_job, halo=halo),
        out_shape=[jax.ShapeDtypeStruct((b, s, d), F32)] + cast_shapes,
        grid=(b, tiles_per_seq),
        in_specs=[
            pl.BlockSpec((1, tm, d), lambda i, j: (i, j, 0)),
            _resident(win.shape),
            _resident_layer(cw, mixer),
            _resident(wout.shape),
        ] + cast_in,
        out_specs=[pl.BlockSpec((1, tm, d), lambda i, j: (i, j, 0))] + cast_out,
        scratch_shapes=[pltpu.VMEM((e_b // MXU_N, halo + tm, MXU_N), F32)],
        compiler_params=_compiler_params(2),
        name="short_conv_mixer",
    )(x3, win, cw, wout, *operands)
    return outs[0], outs[1:]


def kernel(x, mix_norm, ffn_norm, a_w_in, a_v_gain, a_v_bias, a_w_s, a_b_s, a_w_out,
           b_w_in, b_conv_w, b_w_out, ffn_w_gate, ffn_w_up, ffn_w_down, final_norm):
    b, s, d = x.shape
    depth = mix_norm.shape[0]
    tm = 1024
    tm_sgu = 512
    assert s % tm == 0 and s % tm_sgu == 0 and tm_sgu % CHUNK == 0 and a_w_s.shape[-1] == CHUNK
    assert a_w_s.shape[1] * LANES == a_w_out.shape[1]

    a_v_gain3 = a_v_gain.reshape(a_v_gain.shape[0], 1, -1)
    a_v_bias3 = a_v_bias.reshape(a_v_bias.shape[0], 1, -1)

    def mixer_weights(i):
        w_in, w_out = (a_w_in, a_w_out) if i % 2 == 0 else (b_w_in, b_w_out)
        return ((w_in, mix_norm[i].reshape(d, 1)), (w_out, None))

    def ffn_weights(i):
        gain = ffn_norm[i].reshape(d, 1)
        return ((ffn_w_gate, gain), (ffn_w_up, gain), (ffn_w_down, None))

    ready = _cast_call(mixer_weights(0), 0)
    x2 = x.reshape(b * s, d)
    for i in range(depth):
        j = i // 2
        if i % 2 == 0:
            x2, ready = _sgu(x2, ready[0], a_v_gain3, a_v_bias3, a_w_s, a_b_s, ready[1],
                             ffn_weights(i), next_layer=i, mixer=j, tm=tm_sgu)
        else:
            x3, ready = _short_conv(x2.reshape(b, s, d), ready[0], b_conv_w, ready[1],
                                    ffn_weights(i), next_layer=i, mixer=j, tm=tm)
            x2 = x3.reshape(b * s, d)
        last = i == depth - 1
        x2, ready = _ffn(x2, *ready, final_norm.reshape(1, d) if last else None,
                         () if last else mixer_weights(i + 1), next_layer=(i + 1) // 2, tm=tm)
    return x2.reshape(b, s, d)
```

```python
import functools
import math
from typing import NamedTuple

import jax
import jax.numpy as jnp
from jax import lax
from jax.experimental import pallas as pl
from jax.experimental.pallas import tpu as pltpu

EPS = 1e-6
CHUNK = 128
LANES = 128
SUBLANES = 8
BF16_ROWS = 16
MXU_N = 256
OUT_PROJ_BLOCKS = 4
DOWN_PROJ_ROW_PARTS = 4
VMEM_LIMIT_BYTES = 56 * 1024 * 1024

BF16 = jnp.bfloat16
F32 = jnp.float32


def _dot(a, b):
    return jnp.dot(a, b, preferred_element_type=F32)


def _rms_norm(x, g):
    y = x * lax.rsqrt(jnp.mean(x * x, axis=-1, keepdims=True) + EPS)
    return y * g


def _gelu_tanh(x):
    k = -2.0 * math.sqrt(2.0 / math.pi) * math.log2(math.e)
    return x / (1.0 + jnp.exp2(x * (k + (k * 0.044715) * (x * x))))


def _odd_tile_pitch(n):
    assert n % LANES == 0
    return n if (n // LANES) % 2 == 1 else n + LANES


def _resident(shape):
    return pl.BlockSpec(shape, lambda *_: (0,) * len(shape), pipeline_mode=pl.Buffered(1))


def _resident_layer(stacked, layer):
    rest = stacked.shape[1:]
    return pl.BlockSpec((None,) + rest, lambda *_: (layer,) + (0,) * len(rest),
                        pipeline_mode=pl.Buffered(1))


def _compiler_params(n_grid_axes):
    return pltpu.CompilerParams(dimension_semantics=("arbitrary",) * n_grid_axes,
                                vmem_limit_bytes=VMEM_LIMIT_BYTES)


class _Cast(NamedTuple):
    stacked: jax.Array
    layer: int
    rows: int

    @property
    def n_blocks(self):
        n_rows = self.stacked.shape[1]
        assert n_rows % self.rows == 0 and self.rows % BF16_ROWS == 0
        return n_rows // self.rows

    def out_shape(self):
        _, n_rows, n_cols = self.stacked.shape
        return jax.ShapeDtypeStruct((n_rows, _odd_tile_pitch(n_cols)), BF16)

    def specs(self, linear_step):
        n_cols = self.stacked.shape[2]
        block = lambda *g: jnp.minimum(linear_step(*g), self.n_blocks - 1)
        return (pl.BlockSpec((None, self.rows, n_cols), lambda *g: (self.layer, block(*g), 0)),
                pl.BlockSpec((self.rows, _odd_tile_pitch(n_cols)), lambda *g: (block(*g), 0)))


def _cast_jobs(weights, layer, n_steps):
    jobs = []
    for w in weights:
        n_rows = w.shape[1]
        rows = next(r for r in range(BF16_ROWS, n_rows + 1, BF16_ROWS)
                    if n_rows % r == 0 and n_rows // r <= n_steps)
        jobs.append(_Cast(w, layer, rows))
    return jobs


def _cast_blocks(src_refs, dst_refs):
    for src, dst in zip(src_refs, dst_refs):
        n = src.shape[1]
        dst[:, 0:n] = src[...].astype(BF16)
        if dst.shape[1] > n:
            dst[:, n:] = jnp.zeros((dst.shape[0], dst.shape[1] - n), BF16)


def _cast_kernel(*refs):
    n = len(refs) // 2
    _cast_blocks(refs[:n], refs[n:])


def _cast_call(weights, layer):
    n_steps = 8
    casts = _cast_jobs(weights, layer, n_steps)
    specs = [c.specs(lambda i: i) for c in casts]
    return pl.pallas_call(
        _cast_kernel,
        out_shape=[c.out_shape() for c in casts],
        grid=(n_steps,),
        in_specs=[s[0] for s in specs],
        out_specs=[s[1] for s in specs],
        compiler_params=_compiler_params(1),
        name="cast_weights",
    )(*[c.stacked for c in casts])


def _rms_rstd(x):
    return lax.rsqrt(jnp.mean(x * x, axis=-1, keepdims=True) + EPS)


def _zero_after(v):
    bits = lax.bitcast_convert_type(jnp.sum(v, axis=0, keepdims=True), jnp.uint32)
    bits = lax.shift_right_logical(lax.shift_right_logical(bits, jnp.uint32(16)), jnp.uint32(16))
    return lax.bitcast_convert_type(bits, F32)


def _ffn_kernel(*refs, layer, n_casts, col_block, final_norm):
    x_ref, xn_ref, g_ref, wg_ref, wu_ref, wd_ref, fg_ref = refs[:7]
    cast_src = refs[7:7 + n_casts]
    o_ref = refs[7 + n_casts]
    cast_dst = refs[8 + n_casts:8 + 2 * n_casts]
    rstd_ref, = refs[8 + 2 * n_casts:]

    @pl.when(pl.program_id(0) == 0)
    def _():
        rstd_ref[...] = _rms_rstd(x_ref[...])

    x = x_ref[...]
    d = x.shape[1]
    h = (x * rstd_ref[...] * g_ref[layer:layer + 1, :]).astype(BF16)
    d_ff = wd_ref.shape[0]
    a_blocks = []
    for c in range(0, d_ff, col_block):
        g = _dot(h, wg_ref[:, c:c + col_block])
        u = _dot(h, wu_ref[:, c:c + col_block])
        a_blocks.append((g * (1.0 / (1.0 + jnp.exp(-g))) * u).astype(BF16))
    a = jnp.concatenate(a_blocks, axis=1)
    rows = x.shape[0] // DOWN_PROJ_ROW_PARTS
    rstd_next = _rms_rstd(xn_ref[...])
    for r in range(0, x.shape[0], rows):
        a_part = a[r:r + rows]
        if r + rows == x.shape[0]:
            a_part = a_part + _zero_after(rstd_next).astype(BF16)
        acc = x[r:r + rows] + _dot(a_part, wd_ref[:, 0:d])
        if final_norm:
            acc = _rms_norm(acc, fg_ref[...])
        o_ref[r:r + rows, :] = acc
    rstd_ref[...] = rstd_next
    _cast_blocks(cast_src, cast_dst)


def _ffn(x2, norms, wg, wu, wd, fg, next_weights, *, layer, next_mixer, tm, final_norm):
    m, d = x2.shape
    n_steps = m // tm
    casts = _cast_jobs(next_weights, next_mixer, n_steps)
    cast_specs = [c.specs(lambda i: i) for c in casts]
    outs = pl.pallas_call(
        functools.partial(_ffn_kernel, layer=layer, n_casts=len(casts), col_block=MXU_N,
                          final_norm=final_norm),
        out_shape=[jax.ShapeDtypeStruct((m, d), F32)] + [c.out_shape() for c in casts],
        grid=(n_steps,),
        in_specs=[
            pl.BlockSpec((tm, d), lambda i: (i, 0)),
            pl.BlockSpec((tm, d), lambda i: (jnp.minimum(i + 1, n_steps - 1), 0)),
            _resident(norms.shape),
            _resident(wg.shape),
            _resident(wu.shape),
            _resident(wd.shape),
            _resident(fg.shape),
        ] + [s[0] for s in cast_specs],
        out_specs=[pl.BlockSpec((tm, d), lambda i: (i, 0))] + [s[1] for s in cast_specs],
        scratch_shapes=[pltpu.VMEM((tm, 1), F32)],
        compiler_params=_compiler_params(1),
        name="swiglu_final" if final_norm else "swiglu",
    )(x2, x2, norms, wg, wu, wd, fg, *[c.stacked for c in casts])
    return outs[0], outs[1:]


def _sgu_kernel(*refs, layer, n_casts, col_block):
    x_ref, xp_ref, g_ref, win_ref, vg_ref, vb_ref, ws_ref, bs_ref, wout_ref = refs[:9]
    cast_src = refs[9:9 + n_casts]
    o_ref = refs[9 + n_casts]
    cast_dst = refs[10 + n_casts:10 + 2 * n_casts]
    wtri_ref, gate_bias_ref, h_ref, sv_ref = refs[10 + 2 * n_casts:]

    tm, d = x_ref.shape
    e_a = wout_ref.shape[0]
    n_groups = ws_ref.shape[0]
    n_chunks = tm // CHUNK
    n_blocks = e_a // col_block
    groups_per_block = col_block // LANES
    assert n_blocks % OUT_PROJ_BLOCKS == 0

    @pl.when(pl.program_id(0) == 0)
    def _():
        row = lax.broadcasted_iota(jnp.int32, (CHUNK, CHUNK), 0)
        col = lax.broadcasted_iota(jnp.int32, (CHUNK, CHUNK), 1)
        for hg in range(n_groups):
            lanes = slice(hg * LANES, (hg + 1) * LANES)
            w_causal = jnp.where(col <= row, ws_ref[hg], 0.0)
            wtri_ref[hg] = w_causal.astype(BF16)
            b_col = jnp.sum(jnp.where(col == row, bs_ref[hg:hg + 1, :], 0.0), axis=1, keepdims=True)
            gate_bias_ref[:, lanes] = (jnp.sum(w_causal, axis=1, keepdims=True) * vb_ref[:, lanes]
                                       + b_col)
        h_ref[...] = jnp.zeros(h_ref.shape, BF16)
        sv_ref[...] = jnp.zeros(sv_ref.shape, F32)

    def v_proj(b):
        return _dot(h, win_ref[:, e_a + b * col_block:e_a + (b + 1) * col_block])

    def spatial(b):
        vhat = ((v_blocks[b] - mu) * rstd).astype(BF16)
        for gi in range(groups_per_block):
            lanes = slice(gi * LANES, (gi + 1) * LANES)
            rhs = jnp.concatenate(
                [vhat[n * CHUNK:(n + 1) * CHUNK, lanes] for n in range(n_chunks)], axis=1)
            sv_ref[b * groups_per_block + gi] = _dot(wtri_ref[b * groups_per_block + gi], rhs)

    def u_proj(b):
        return _dot(h_prev, win_ref[:, b * col_block:(b + 1) * col_block])

    def gate(b, u):
        y_cols = []
        for gi in range(groups_per_block):
            hg = b * groups_per_block + gi
            lanes = slice(gi * LANES, (gi + 1) * LANES)
            gain = vg_ref[:, hg * LANES:(hg + 1) * LANES]
            bias = gate_bias_ref[:, hg * LANES:(hg + 1) * LANES]
            y_cols.append(jnp.concatenate(
                [u[n * CHUNK:(n + 1) * CHUNK, lanes]
                 * (sv_ref[hg, :, n * LANES:(n + 1) * LANES] * gain + bias)
                 for n in range(n_chunks)], axis=0))
        return jnp.concatenate(y_cols, axis=1).astype(BF16)

    h_prev = h_ref[...]
    zu = u_proj(0)
    h = _rms_norm(x_ref[...], g_ref[layer:layer + 1, :]).astype(BF16)
    zv_blocks = []
    y_blocks = []
    for b in range(n_blocks):
        zv_blocks.append(v_proj(b))
        zu_next = u_proj(b + 1) if b + 1 < n_blocks else None
        y_blocks.append(gate(b, _gelu_tanh(zu)))
        zu = zu_next
    acc = xp_ref[...]
    for k in range(0, n_blocks, OUT_PROJ_BLOCKS):
        acc = acc + _dot(jnp.concatenate(y_blocks[k:k + OUT_PROJ_BLOCKS], axis=1),
                         wout_ref[k * col_block:(k + OUT_PROJ_BLOCKS) * col_block, 0:d])
    o_ref[...] = acc
    v_blocks = []
    sum_v = jnp.zeros((tm, LANES), F32)
    sum_vv = jnp.zeros((tm, LANES), F32)
    for b in range(n_blocks):
        v_blocks.append(_gelu_tanh(zv_blocks[b]))
        for c in range(0, col_block, LANES):
            t = v_blocks[b][:, c:c + LANES]
            sum_v = sum_v + t
            sum_vv = sum_vv + t * t
    mu = jnp.sum(sum_v, axis=-1, keepdims=True) * (1.0 / e_a)
    var = jnp.maximum(jnp.sum(sum_vv, axis=-1, keepdims=True) * (1.0 / e_a) - mu * mu, 0.0)
    rstd = lax.rsqrt(var + EPS)
    for b in range(n_blocks):
        spatial(b)
    h_ref[...] = h
    _cast_blocks(cast_src, cast_dst)


def _sgu(x2, norms, win, vg, vb, ws, bs, wout, next_weights, *, layer, mixer, tm):
    m, d = x2.shape
    e_a = wout.shape[0]
    n_groups = ws.shape[1]
    n_tiles = m // tm
    casts = _cast_jobs(next_weights, layer, n_tiles)
    cast_specs = [c.specs(lambda i: i) for c in casts]
    cur = lambda i: (jnp.minimum(i, n_tiles - 1), 0)
    prev = lambda i: (jnp.maximum(i - 1, 0), 0)
    outs = pl.pallas_call(
        functools.partial(_sgu_kernel, layer=layer, n_casts=len(casts), col_block=MXU_N),
        out_shape=[jax.ShapeDtypeStruct((m, d), F32)] + [c.out_shape() for c in casts],
        grid=(n_tiles + 1,),
        in_specs=[
            pl.BlockSpec((tm, d), cur),
            pl.BlockSpec((tm, d), prev),
            _resident(norms.shape),
            _resident(win.shape),
            _resident_layer(vg, mixer),
            _resident_layer(vb, mixer),
            _resident_layer(ws, mixer),
            _resident_layer(bs, mixer),
            _resident(wout.shape),
        ] + [s[0] for s in cast_specs],
        out_specs=[pl.BlockSpec((tm, d), prev)] + [s[1] for s in cast_specs],
        scratch_shapes=[
            pltpu.VMEM((n_groups, CHUNK, CHUNK), BF16),
            pltpu.VMEM((CHUNK, e_a), F32),
            pltpu.VMEM((tm, d), BF16),
            pltpu.VMEM((n_groups, CHUNK, tm), F32),
        ],
        compiler_params=_compiler_params(1),
        name="sgu_mixer",
    )(x2, x2, norms, win, vg, vb, ws, bs, wout, *[c.stacked for c in casts])
    return outs[0], outs[1:]


def _conv_kernel(*refs, layer, n_casts, halo):
    x_ref, g_ref, win_ref, cw_ref, wout_ref = refs[:5]
    cast_src = refs[5:5 + n_casts]
    o_ref = refs[5 + n_casts]
    cast_dst = refs[6 + n_casts:6 + 2 * n_casts]
    z_ref, = refs[6 + 2 * n_casts:]

    tm = x_ref.shape[1]
    e_b = wout_ref.shape[0]
    conv_w = cw_ref.shape[0]
    n_blocks, _, col_block = z_ref.shape

    @pl.when(pl.program_id(1) == 0)
    def _():
        z_ref[:, 0:halo, :] = jnp.zeros((n_blocks, halo, col_block), F32)

    @pl.when(pl.program_id(1) != 0)
    def _():
        z_ref[:, 0:halo, :] = z_ref[:, tm:tm + halo, :]

    x = x_ref[0]
    h = _rms_norm(x, g_ref[layer:layer + 1, :]).astype(BF16)
    y_blocks = []
    for j in range(n_blocks):
        c = j * col_block
        cols = slice(c, c + col_block)
        z_ref[j, halo:halo + tm, :] = (_dot(h, win_ref[:, e_b + c:e_b + c + col_block])
                                       * _dot(h, win_ref[:, 2 * e_b + c:2 * e_b + c + col_block]))
        b_gate = _dot(h, win_ref[:, cols])
        conv = z_ref[j, halo:halo + tm, :] * cw_ref[conv_w - 1:conv_w, cols]
        for k in range(conv_w - 1):
            back = conv_w - 1 - k
            conv = conv + z_ref[j, halo - back:halo - back + tm, :] * cw_ref[k:k + 1, cols]
        y_blocks.append((b_gate * conv).astype(BF16))
    o_ref[0] = x + _dot(jnp.concatenate(y_blocks, axis=1), wout_ref[:, 0:x.shape[1]])
    _cast_blocks(cast_src, cast_dst)


def _short_conv(x3, norms, win, cw, wout, next_weights, *, layer, mixer, tm):
    b, s, d = x3.shape
    e_b = wout.shape[0]
    halo = SUBLANES
    assert cw.shape[1] - 1 <= halo
    tiles_per_seq = s // tm
    casts = _cast_jobs(next_weights, layer, b * tiles_per_seq)
    cast_specs = [c.specs(lambda i, j: i * tiles_per_seq + j) for c in casts]
    outs = pl.pallas_call(
        functools.partial(_conv_kernel, layer=layer, n_casts=len(casts), halo=halo),
        out_shape=[jax.ShapeDtypeStruct((b, s, d), F32)] + [c.out_shape() for c in casts],
        grid=(b, tiles_per_seq),
        in_specs=[
            pl.BlockSpec((1, tm, d), lambda i, j: (i, j, 0)),
            _resident(norms.shape),
            _resident(win.shape),
            _resident_layer(cw, mixer),
            _resident(wout.shape),
        ] + [s[0] for s in cast_specs],
        out_specs=[pl.BlockSpec((1, tm, d), lambda i, j: (i, j, 0))] + [s[1] for s in cast_specs],
        scratch_shapes=[pltpu.VMEM((e_b // MXU_N, halo + tm, MXU_N), F32)],
        compiler_params=_compiler_params(2),
        name="short_conv_mixer",
    )(x3, norms, win, cw, wout, *[c.stacked for c in casts])
    return outs[0], outs[1:]


def kernel(x, mix_norm, ffn_norm, a_w_in, a_v_gain, a_v_bias, a_w_s, a_b_s, a_w_out,
           b_w_in, b_conv_w, b_w_out, ffn_w_gate, ffn_w_up, ffn_w_down, final_norm):
    b, s, d = x.shape
    depth = mix_norm.shape[0]
    tm = 1024
    tm_sgu = 512
    assert s % tm == 0 and s % tm_sgu == 0 and tm_sgu % CHUNK == 0 and a_w_s.shape[-1] == CHUNK
    assert a_w_s.shape[1] * LANES == a_w_out.shape[1]

    fg = final_norm.reshape(1, d)
    a_v_gain3 = a_v_gain.reshape(a_v_gain.shape[0], 1, -1)
    a_v_bias3 = a_v_bias.reshape(a_v_bias.shape[0], 1, -1)
    ffn_weights = (ffn_w_gate, ffn_w_up, ffn_w_down)

    def mixer_weights(i):
        return (a_w_in, a_w_out) if i % 2 == 0 else (b_w_in, b_w_out)

    ready = _cast_call(mixer_weights(0), 0)
    x2 = x.reshape(b * s, d)
    for i in range(depth):
        j = i // 2
        if i % 2 == 0:
            x2, ready = _sgu(x2, mix_norm, ready[0], a_v_gain3, a_v_bias3, a_w_s, a_b_s, ready[1],
                             ffn_weights, layer=i, mixer=j, tm=tm_sgu)
        else:
            x3, ready = _short_conv(x2.reshape(b, s, d), mix_norm, ready[0], b_conv_w, ready[1],
                                    ffn_weights, layer=i, mixer=j, tm=tm)
            x2 = x3.reshape(b * s, d)
        last = i == depth - 1
        x2, ready = _ffn(x2, ffn_norm, *ready, fg, () if last else mixer_weights(i + 1),
                         layer=i, next_mixer=(i + 1) // 2, tm=tm, final_norm=last)
    return x2.reshape(b, s, d)
```

```python
import functools
import math
from typing import NamedTuple

import jax
import jax.numpy as jnp
from jax import lax
from jax.experimental import pallas as pl
from jax.experimental.pallas import tpu as pltpu

EPS = 1e-6
CHUNK = 128
LANES = 128
SUBLANES = 8
BF16_ROWS = 16
MXU_N = 256
OUT_PROJ_BLOCKS = 4
DOWN_PROJ_ROW_PARTS = 4
VMEM_LIMIT_BYTES = 56 * 1024 * 1024

BF16 = jnp.bfloat16
F32 = jnp.float32


def _dot(a, b):
    return jnp.dot(a, b, preferred_element_type=F32)


def _rms_norm(x, g):
    y = x * lax.rsqrt(jnp.mean(x * x, axis=-1, keepdims=True) + EPS)
    return y * g


def _gelu_tanh(x):
    k = -2.0 * math.sqrt(2.0 / math.pi) * math.log2(math.e)
    return x / (1.0 + jnp.exp2(x * (k + (k * 0.044715) * (x * x))))


def _odd_tile_pitch(n):
    assert n % LANES == 0
    return n if (n // LANES) % 2 == 1 else n + LANES


def _resident(shape):
    return pl.BlockSpec(shape, lambda *_: (0,) * len(shape), pipeline_mode=pl.Buffered(1))


def _resident_layer(stacked, layer):
    rest = stacked.shape[1:]
    return pl.BlockSpec((None,) + rest, lambda *_: (layer,) + (0,) * len(rest),
                        pipeline_mode=pl.Buffered(1))


def _compiler_params(n_grid_axes):
    return pltpu.CompilerParams(dimension_semantics=("arbitrary",) * n_grid_axes,
                                vmem_limit_bytes=VMEM_LIMIT_BYTES)


class _Cast(NamedTuple):
    stacked: jax.Array
    layer: int
    rows: int

    @property
    def n_blocks(self):
        n_rows = self.stacked.shape[1]
        assert n_rows % self.rows == 0 and self.rows % BF16_ROWS == 0
        return n_rows // self.rows

    def out_shape(self):
        _, n_rows, n_cols = self.stacked.shape
        return jax.ShapeDtypeStruct((n_rows, _odd_tile_pitch(n_cols)), BF16)

    def specs(self, linear_step):
        n_cols = self.stacked.shape[2]
        block = lambda *g: jnp.minimum(linear_step(*g), self.n_blocks - 1)
        return (pl.BlockSpec((None, self.rows, n_cols), lambda *g: (self.layer, block(*g), 0)),
                pl.BlockSpec((self.rows, _odd_tile_pitch(n_cols)), lambda *g: (block(*g), 0)))


def _cast_jobs(weights, layer, n_steps):
    jobs = []
    for w in weights:
        n_rows = w.shape[1]
        rows = next(r for r in range(BF16_ROWS, n_rows + 1, BF16_ROWS)
                    if n_rows % r == 0 and n_rows // r <= n_steps)
        jobs.append(_Cast(w, layer, rows))
    return jobs


def _cast_blocks(src_refs, dst_refs):
    for src, dst in zip(src_refs, dst_refs):
        n = src.shape[1]
        dst[:, 0:n] = src[...].astype(BF16)
        if dst.shape[1] > n:
            dst[:, n:] = jnp.zeros((dst.shape[0], dst.shape[1] - n), BF16)


def _cast_kernel(*refs):
    n = len(refs) // 2
    _cast_blocks(refs[:n], refs[n:])


def _cast_call(weights, layer):
    n_steps = 8
    casts = _cast_jobs(weights, layer, n_steps)
    specs = [c.specs(lambda i: i) for c in casts]
    return pl.pallas_call(
        _cast_kernel,
        out_shape=[c.out_shape() for c in casts],
        grid=(n_steps,),
        in_specs=[s[0] for s in specs],
        out_specs=[s[1] for s in specs],
        compiler_params=_compiler_params(1),
        name="cast_weights",
    )(*[c.stacked for c in casts])


def _ffn_kernel(*refs, layer, n_casts, col_block, final_norm):
    x_ref, g_ref, wg_ref, wu_ref, wd_ref, fg_ref = refs[:6]
    cast_src = refs[6:6 + n_casts]
    o_ref = refs[6 + n_casts]
    cast_dst = refs[7 + n_casts:7 + 2 * n_casts]
    h_ref, = refs[7 + 2 * n_casts:]

    x = x_ref[...]
    d = x.shape[1]
    h_ref[:, 0:d] = _rms_norm(x, g_ref[layer:layer + 1, :]).astype(BF16)
    d_ff = wd_ref.shape[0]
    a_blocks = []
    for c in range(0, d_ff, col_block):
        g = _dot(h_ref[:, 0:d], wg_ref[:, c:c + col_block])
        u = _dot(h_ref[:, 0:d], wu_ref[:, c:c + col_block])
        a_blocks.append((g * (1.0 / (1.0 + jnp.exp(-g))) * u).astype(BF16))
    a = jnp.concatenate(a_blocks, axis=1)
    rows = x.shape[0] // DOWN_PROJ_ROW_PARTS
    for r in range(0, x.shape[0], rows):
        acc = x[r:r + rows] + _dot(a[r:r + rows], wd_ref[:, 0:d])
        if final_norm:
            acc = _rms_norm(acc, fg_ref[...])
        o_ref[r:r + rows, :] = acc
    _cast_blocks(cast_src, cast_dst)


def _ffn(x2, norms, wg, wu, wd, fg, next_weights, *, layer, next_mixer, tm, final_norm):
    m, d = x2.shape
    n_steps = m // tm
    casts = _cast_jobs(next_weights, next_mixer, n_steps)
    cast_specs = [c.specs(lambda i: i) for c in casts]
    outs = pl.pallas_call(
        functools.partial(_ffn_kernel, layer=layer, n_casts=len(casts), col_block=MXU_N,
                          final_norm=final_norm),
        out_shape=[jax.ShapeDtypeStruct((m, d), F32)] + [c.out_shape() for c in casts],
        grid=(n_steps,),
        in_specs=[
            pl.BlockSpec((tm, d), lambda i: (i, 0)),
            _resident(norms.shape),
            _resident(wg.shape),
            _resident(wu.shape),
            _resident(wd.shape),
            _resident(fg.shape),
        ] + [s[0] for s in cast_specs],
        out_specs=[pl.BlockSpec((tm, d), lambda i: (i, 0))] + [s[1] for s in cast_specs],
        scratch_shapes=[pltpu.VMEM((tm, _odd_tile_pitch(d)), BF16)],
        compiler_params=_compiler_params(1),
        name="swiglu_final" if final_norm else "swiglu",
    )(x2, norms, wg, wu, wd, fg, *[c.stacked for c in casts])
    return outs[0], outs[1:]


def _sgu_kernel(*refs, layer, n_casts, col_block):
    x_ref, xp_ref, g_ref, win_ref, vg_ref, vb_ref, ws_ref, bs_ref, wout_ref = refs[:9]
    cast_src = refs[9:9 + n_casts]
    o_ref = refs[9 + n_casts]
    cast_dst = refs[10 + n_casts:10 + 2 * n_casts]
    wtri_ref, gate_bias_ref, h_ref, sv_ref = refs[10 + 2 * n_casts:]

    tm, d = x_ref.shape
    e_a = wout_ref.shape[0]
    n_groups = ws_ref.shape[0]
    n_chunks = tm // CHUNK
    n_blocks = e_a // col_block
    groups_per_block = col_block // LANES
    assert n_blocks % OUT_PROJ_BLOCKS == 0

    @pl.when(pl.program_id(0) == 0)
    def _():
        row = lax.broadcasted_iota(jnp.int32, (CHUNK, CHUNK), 0)
        col = lax.broadcasted_iota(jnp.int32, (CHUNK, CHUNK), 1)
        for hg in range(n_groups):
            lanes = slice(hg * LANES, (hg + 1) * LANES)
            w_causal = jnp.where(col <= row, ws_ref[hg], 0.0)
            wtri_ref[hg] = w_causal.astype(BF16)
            b_col = jnp.sum(jnp.where(col == row, bs_ref[hg:hg + 1, :], 0.0), axis=1, keepdims=True)
            gate_bias_ref[:, lanes] = (jnp.sum(w_causal, axis=1, keepdims=True) * vb_ref[:, lanes]
                                       + b_col)
        h_ref[...] = jnp.zeros(h_ref.shape, BF16)
        sv_ref[...] = jnp.zeros(sv_ref.shape, F32)

    def v_proj(b):
        return _dot(h, win_ref[:, e_a + b * col_block:e_a + (b + 1) * col_block])

    def spatial(b):
        vhat = ((v_blocks[b] - mu) * rstd).astype(BF16)
        for gi in range(groups_per_block):
            lanes = slice(gi * LANES, (gi + 1) * LANES)
            rhs = jnp.concatenate(
                [vhat[n * CHUNK:(n + 1) * CHUNK, lanes] for n in range(n_chunks)], axis=1)
            sv_ref[b * groups_per_block + gi] = _dot(wtri_ref[b * groups_per_block + gi], rhs)

    def u_proj(b):
        return _dot(h_prev, win_ref[:, b * col_block:(b + 1) * col_block])

    def gate(b, u):
        y_cols = []
        for gi in range(groups_per_block):
            hg = b * groups_per_block + gi
            lanes = slice(gi * LANES, (gi + 1) * LANES)
            gain = vg_ref[:, hg * LANES:(hg + 1) * LANES]
            bias = gate_bias_ref[:, hg * LANES:(hg + 1) * LANES]
            y_cols.append(jnp.concatenate(
                [u[n * CHUNK:(n + 1) * CHUNK, lanes]
                 * (sv_ref[hg, :, n * LANES:(n + 1) * LANES] * gain + bias)
                 for n in range(n_chunks)], axis=0))
        return jnp.concatenate(y_cols, axis=1).astype(BF16)

    h_prev = h_ref[...]
    zu = u_proj(0)
    h = _rms_norm(x_ref[...], g_ref[layer:layer + 1, :]).astype(BF16)
    zv_blocks = []
    y_blocks = []
    for b in range(n_blocks):
        zv_blocks.append(v_proj(b))
        zu_next = u_proj(b + 1) if b + 1 < n_blocks else None
        y_blocks.append(gate(b, _gelu_tanh(zu)))
        zu = zu_next
    acc = xp_ref[...]
    for k in range(0, n_blocks, OUT_PROJ_BLOCKS):
        acc = acc + _dot(jnp.concatenate(y_blocks[k:k + OUT_PROJ_BLOCKS], axis=1),
                         wout_ref[k * col_block:(k + OUT_PROJ_BLOCKS) * col_block, 0:d])
    o_ref[...] = acc
    v_blocks = []
    sum_v = jnp.zeros((tm, LANES), F32)
    sum_vv = jnp.zeros((tm, LANES), F32)
    for b in range(n_blocks):
        v_blocks.append(_gelu_tanh(zv_blocks[b]))
        for c in range(0, col_block, LANES):
            t = v_blocks[b][:, c:c + LANES]
            sum_v = sum_v + t
            sum_vv = sum_vv + t * t
    mu = jnp.sum(sum_v, axis=-1, keepdims=True) * (1.0 / e_a)
    var = jnp.maximum(jnp.sum(sum_vv, axis=-1, keepdims=True) * (1.0 / e_a) - mu * mu, 0.0)
    rstd = lax.rsqrt(var + EPS)
    for b in range(n_blocks):
        spatial(b)
    h_ref[...] = h
    _cast_blocks(cast_src, cast_dst)


def _sgu(x2, norms, win, vg, vb, ws, bs, wout, next_weights, *, layer, mixer, tm):
    m, d = x2.shape
    e_a = wout.shape[0]
    n_groups = ws.shape[1]
    n_tiles = m // tm
    casts = _cast_jobs(next_weights, layer, n_tiles)
    cast_specs = [c.specs(lambda i: i) for c in casts]
    cur = lambda i: (jnp.minimum(i, n_tiles - 1), 0)
    prev = lambda i: (jnp.maximum(i - 1, 0), 0)
    outs = pl.pallas_call(
        functools.partial(_sgu_kernel, layer=layer, n_casts=len(casts), col_block=MXU_N),
        out_shape=[jax.ShapeDtypeStruct((m, d), F32)] + [c.out_shape() for c in casts],
        grid=(n_tiles + 1,),
        in_specs=[
            pl.BlockSpec((tm, d), cur),
            pl.BlockSpec((tm, d), prev),
            _resident(norms.shape),
            _resident(win.shape),
            _resident_layer(vg, mixer),
            _resident_layer(vb, mixer),
            _resident_layer(ws, mixer),
            _resident_layer(bs, mixer),
            _resident(wout.shape),
        ] + [s[0] for s in cast_specs],
        out_specs=[pl.BlockSpec((tm, d), prev)] + [s[1] for s in cast_specs],
        scratch_shapes=[
            pltpu.VMEM((n_groups, CHUNK, CHUNK), BF16),
            pltpu.VMEM((CHUNK, e_a), F32),
            pltpu.VMEM((tm, d), BF16),
            pltpu.VMEM((n_groups, CHUNK, tm), F32),
        ],
        compiler_params=_compiler_params(1),
        name="sgu_mixer",
    )(x2, x2, norms, win, vg, vb, ws, bs, wout, *[c.stacked for c in casts])
    return outs[0], outs[1:]


def _conv_kernel(*refs, layer, n_casts, halo):
    x_ref, g_ref, win_ref, cw_ref, wout_ref = refs[:5]
    cast_src = refs[5:5 + n_casts]
    o_ref = refs[5 + n_casts]
    cast_dst = refs[6 + n_casts:6 + 2 * n_casts]
    z_ref, = refs[6 + 2 * n_casts:]

    tm = x_ref.shape[1]
    e_b = wout_ref.shape[0]
    conv_w = cw_ref.shape[0]
    n_blocks, _, col_block = z_ref.shape

    @pl.when(pl.program_id(1) == 0)
    def _():
        z_ref[:, 0:halo, :] = jnp.zeros((n_blocks, halo, col_block), F32)

    @pl.when(pl.program_id(1) != 0)
    def _():
        z_ref[:, 0:halo, :] = z_ref[:, tm:tm + halo, :]

    x = x_ref[0]
    h = _rms_norm(x, g_ref[layer:layer + 1, :]).astype(BF16)
    y_blocks = []
    for j in range(n_blocks):
        c = j * col_block
        cols = slice(c, c + col_block)
        z_ref[j, halo:halo + tm, :] = (_dot(h, win_ref[:, e_b + c:e_b + c + col_block])
                                       * _dot(h, win_ref[:, 2 * e_b + c:2 * e_b + c + col_block]))
        b_gate = _dot(h, win_ref[:, cols])
        conv = z_ref[j, halo:halo + tm, :] * cw_ref[conv_w - 1:conv_w, cols]
        for k in range(conv_w - 1):
            back = conv_w - 1 - k
            conv = conv + z_ref[j, halo - back:halo - back + tm, :] * cw_ref[k:k + 1, cols]
        y_blocks.append((b_gate * conv).astype(BF16))
    o_ref[0] = x + _dot(jnp.concatenate(y_blocks, axis=1), wout_ref[:, 0:x.shape[1]])
    _cast_blocks(cast_src, cast_dst)


def _short_conv(x3, norms, win, cw, wout, next_weights, *, layer, mixer, tm):
    b, s, d = x3.shape
    e_b = wout.shape[0]
    halo = SUBLANES
    assert cw.shape[1] - 1 <= halo
    tiles_per_seq = s // tm
    casts = _cast_jobs(next_weights, layer, b * tiles_per_seq)
    cast_specs = [c.specs(lambda i, j: i * tiles_per_seq + j) for c in casts]
    outs = pl.pallas_call(
        functools.partial(_conv_kernel, layer=layer, n_casts=len(casts), halo=halo),
        out_shape=[jax.ShapeDtypeStruct((b, s, d), F32)] + [c.out_shape() for c in casts],
        grid=(b, tiles_per_seq),
        in_specs=[
            pl.BlockSpec((1, tm, d), lambda i, j: (i, j, 0)),
            _resident(norms.shape),
            _resident(win.shape),
            _resident_layer(cw, mixer),
            _resident(wout.shape),
        ] + [s[0] for s in cast_specs],
        out_specs=[pl.BlockSpec((1, tm, d), lambda i, j: (i, j, 0))] + [s[1] for s in cast_specs],
        scratch_shapes=[pltpu.VMEM((e_b // MXU_N, halo + tm, MXU_N), F32)],
        compiler_params=_compiler_params(2),
        name="short_conv_mixer",
    )(x3, norms, win, cw, wout, *[c.stacked for c in casts])
    return outs[0], outs[1:]


def kernel(x, mix_norm, ffn_norm, a_w_in, a_v_gain, a_v_bias, a_w_s, a_b_s, a_w_out,
           b_w_in, b_conv_w, b_w_out, ffn_w_gate, ffn_w_up, ffn_w_down, final_norm):
    b, s, d = x.shape
    depth = mix_norm.shape[0]
    tm = 1024
    tm_sgu = 512
    assert s % tm == 0 and s % tm_sgu == 0 and tm_sgu % CHUNK == 0 and a_w_s.shape[-1] == CHUNK
    assert a_w_s.shape[1] * LANES == a_w_out.shape[1]

    fg = final_norm.reshape(1, d)
    a_v_gain3 = a_v_gain.reshape(a_v_gain.shape[0], 1, -1)
    a_v_bias3 = a_v_bias.reshape(a_v_bias.shape[0], 1, -1)
    ffn_weights = (ffn_w_gate, ffn_w_up, ffn_w_down)

    def mixer_weights(i):
        return (a_w_in, a_w_out) if i % 2 == 0 else (b_w_in, b_w_out)

    ready = _cast_call(mixer_weights(0), 0)
    x2 = x.reshape(b * s, d)
    for i in range(depth):
        j = i // 2
        if i % 2 == 0:
            x2, ready = _sgu(x2, mix_norm, ready[0], a_v_gain3, a_v_bias3, a_w_s, a_b_s, ready[1],
                             ffn_weights, layer=i, mixer=j, tm=tm_sgu)
        else:
            x3, ready = _short_conv(x2.reshape(b, s, d), mix_norm, ready[0], b_conv_w, ready[1],
                                    ffn_weights, layer=i, mixer=j, tm=tm)
            x2 = x3.reshape(b * s, d)
        last = i == depth - 1
        x2, ready = _ffn(x2, ffn_norm, *ready, fg, () if last else mixer_weights(i + 1),
                         layer=i, next_mixer=(i + 1) // 2, tm=tm, final_norm=last)
    return x2.reshape(b, s, d)
```

```python
import functools
import math
from typing import NamedTuple

import jax
import jax.numpy as jnp
from jax import lax
from jax.experimental import pallas as pl
from jax.experimental.pallas import tpu as pltpu

EPS = 1e-6
CHUNK = 128
LANES = 128
SUBLANES = 8
BF16_ROWS = 16
MXU_N = 256
OUT_PROJ_BLOCKS = 4
DOWN_PROJ_ROW_PARTS = 4
VMEM_LIMIT_BYTES = 56 * 1024 * 1024

BF16 = jnp.bfloat16
F32 = jnp.float32


def _dot(a, b):
    return jnp.dot(a, b, preferred_element_type=F32)


def _rms_norm(x, g):
    y = x * lax.rsqrt(jnp.mean(x * x, axis=-1, keepdims=True) + EPS)
    return y * g


def _gelu_tanh(x):
    k = -2.0 * math.sqrt(2.0 / math.pi) * math.log2(math.e)
    return x / (1.0 + jnp.exp2(x * (k + (k * 0.044715) * (x * x))))


def _odd_tile_pitch(n):
    assert n % LANES == 0
    return n if (n // LANES) % 2 == 1 else n + LANES


def _resident(shape):
    return pl.BlockSpec(shape, lambda *_: (0,) * len(shape), pipeline_mode=pl.Buffered(1))


def _resident_layer(stacked, layer):
    rest = stacked.shape[1:]
    return pl.BlockSpec((None,) + rest, lambda *_: (layer,) + (0,) * len(rest),
                        pipeline_mode=pl.Buffered(1))


def _compiler_params(n_grid_axes):
    return pltpu.CompilerParams(dimension_semantics=("arbitrary",) * n_grid_axes,
                                vmem_limit_bytes=VMEM_LIMIT_BYTES)


class _Cast(NamedTuple):
    stacked: jax.Array
    layer: int
    rows: int

    @property
    def n_blocks(self):
        n_rows = self.stacked.shape[1]
        assert n_rows % self.rows == 0 and self.rows % BF16_ROWS == 0
        return n_rows // self.rows

    def out_shape(self):
        _, n_rows, n_cols = self.stacked.shape
        return jax.ShapeDtypeStruct((n_rows, _odd_tile_pitch(n_cols)), BF16)

    def specs(self, linear_step):
        n_cols = self.stacked.shape[2]
        block = lambda *g: jnp.minimum(linear_step(*g), self.n_blocks - 1)
        return (pl.BlockSpec((None, self.rows, n_cols), lambda *g: (self.layer, block(*g), 0)),
                pl.BlockSpec((self.rows, _odd_tile_pitch(n_cols)), lambda *g: (block(*g), 0)))


def _cast_jobs(weights, layer, n_steps):
    jobs = []
    for w in weights:
        n_rows = w.shape[1]
        rows = next(r for r in range(BF16_ROWS, n_rows + 1, BF16_ROWS)
                    if n_rows % r == 0 and n_rows // r <= n_steps)
        jobs.append(_Cast(w, layer, rows))
    return jobs


def _cast_blocks(src_refs, dst_refs):
    for src, dst in zip(src_refs, dst_refs):
        n = src.shape[1]
        dst[:, 0:n] = src[...].astype(BF16)
        if dst.shape[1] > n:
            dst[:, n:] = jnp.zeros((dst.shape[0], dst.shape[1] - n), BF16)


def _ffn_kernel(*refs, layer, n_casts, col_block, final_norm):
    x_ref, g_ref, wg_ref, wu_ref, wd_ref, fg_ref = refs[:6]
    cast_src = refs[6:6 + n_casts]
    o_ref = refs[6 + n_casts]
    cast_dst = refs[7 + n_casts:]

    x = x_ref[...]
    d = x.shape[1]
    h = _rms_norm(x, g_ref[layer:layer + 1, :]).astype(BF16)
    d_ff = wd_ref.shape[0]
    a_blocks = []
    for c in range(0, d_ff, col_block):
        g = _dot(h, wg_ref[:, c:c + col_block])
        u = _dot(h, wu_ref[:, c:c + col_block])
        a_blocks.append((g * (1.0 / (1.0 + jnp.exp(-g))) * u).astype(BF16))
    a = jnp.concatenate(a_blocks, axis=1)
    rows = x.shape[0] // DOWN_PROJ_ROW_PARTS
    for r in range(0, x.shape[0], rows):
        acc = x[r:r + rows] + _dot(a[r:r + rows], wd_ref[:, 0:d])
        if final_norm:
            acc = _rms_norm(acc, fg_ref[...])
        o_ref[r:r + rows, :] = acc
    _cast_blocks(cast_src, cast_dst)


def _ffn(x2, norms, wg, wu, wd, fg, next_weights, *, layer, next_mixer, tm, final_norm):
    m, d = x2.shape
    n_steps = m // tm
    casts = _cast_jobs(next_weights, next_mixer, n_steps)
    cast_specs = [c.specs(lambda i: i) for c in casts]
    outs = pl.pallas_call(
        functools.partial(_ffn_kernel, layer=layer, n_casts=len(casts), col_block=MXU_N,
                          final_norm=final_norm),
        out_shape=[jax.ShapeDtypeStruct((m, d), F32)] + [c.out_shape() for c in casts],
        grid=(n_steps,),
        in_specs=[
            pl.BlockSpec((tm, d), lambda i: (i, 0)),
            _resident(norms.shape),
            _resident(wg.shape),
            _resident(wu.shape),
            _resident(wd.shape),
            _resident(fg.shape),
        ] + [s[0] for s in cast_specs],
        out_specs=[pl.BlockSpec((tm, d), lambda i: (i, 0))] + [s[1] for s in cast_specs],
        compiler_params=_compiler_params(1),
        name="swiglu_final" if final_norm else "swiglu",
    )(x2, norms, wg, wu, wd, fg, *[c.stacked for c in casts])
    return outs[0], outs[1:]


def _sgu_kernel(*refs, layer, n_casts, n_prep, col_block):
    x_ref, xp_ref, g_ref, win_f32_ref, vg_ref, vb_ref, ws_ref, bs_ref, wout_f32_ref = refs[:9]
    cast_src = refs[9:9 + n_casts]
    o_ref = refs[9 + n_casts]
    cast_dst = refs[10 + n_casts:10 + 2 * n_casts]
    win_ref, wout_ref, wtri_ref, gate_bias_ref, h_ref, sv_ref = refs[10 + 2 * n_casts:]

    tm, d = x_ref.shape
    e_a = wout_ref.shape[0]
    n_groups = ws_ref.shape[0]
    n_chunks = tm // CHUNK
    n_blocks = e_a // col_block
    groups_per_block = col_block // LANES
    assert n_blocks % OUT_PROJ_BLOCKS == 0
    step = pl.program_id(0)

    @pl.when(step < n_prep)
    def _():
        for src, dst in ((win_f32_ref, win_ref), (wout_f32_ref, wout_ref)):
            rows, n = src.shape
            at = pl.ds(pl.multiple_of(step * rows, rows), rows)
            dst[at, 0:n] = src[...].astype(BF16)
            if dst.shape[1] > n:
                dst[at, n:] = jnp.zeros((rows, dst.shape[1] - n), BF16)

    @pl.when(step == n_prep)
    def _():
        row = lax.broadcasted_iota(jnp.int32, (CHUNK, CHUNK), 0)
        col = lax.broadcasted_iota(jnp.int32, (CHUNK, CHUNK), 1)
        for hg in range(n_groups):
            lanes = slice(hg * LANES, (hg + 1) * LANES)
            w_causal = jnp.where(col <= row, ws_ref[hg], 0.0)
            wtri_ref[hg] = w_causal.astype(BF16)
            b_col = jnp.sum(jnp.where(col == row, bs_ref[hg:hg + 1, :], 0.0), axis=1, keepdims=True)
            gate_bias_ref[:, lanes] = (jnp.sum(w_causal, axis=1, keepdims=True) * vb_ref[:, lanes]
                                       + b_col)
        h_ref[...] = jnp.zeros(h_ref.shape, BF16)
        sv_ref[...] = jnp.zeros(sv_ref.shape, F32)

    @pl.when(step >= n_prep)
    def _():
        def v_proj(b):
            return _dot(h, win_ref[:, e_a + b * col_block:e_a + (b + 1) * col_block])

        def spatial(b):
            vhat = ((v_blocks[b] - mu) * rstd).astype(BF16)
            for gi in range(groups_per_block):
                lanes = slice(gi * LANES, (gi + 1) * LANES)
                rhs = jnp.concatenate(
                    [vhat[n * CHUNK:(n + 1) * CHUNK, lanes] for n in range(n_chunks)], axis=1)
                sv_ref[b * groups_per_block + gi] = _dot(wtri_ref[b * groups_per_block + gi], rhs)

        def u_proj(b):
            return _dot(h_prev, win_ref[:, b * col_block:(b + 1) * col_block])

        def gate(b, u):
            y_cols = []
            for gi in range(groups_per_block):
                hg = b * groups_per_block + gi
                lanes = slice(gi * LANES, (gi + 1) * LANES)
                gain = vg_ref[:, hg * LANES:(hg + 1) * LANES]
                bias = gate_bias_ref[:, hg * LANES:(hg + 1) * LANES]
                y_cols.append(jnp.concatenate(
                    [u[n * CHUNK:(n + 1) * CHUNK, lanes]
                     * (sv_ref[hg, :, n * LANES:(n + 1) * LANES] * gain + bias)
                     for n in range(n_chunks)], axis=0))
            return jnp.concatenate(y_cols, axis=1).astype(BF16)

        h_prev = h_ref[...]
        zu = u_proj(0)
        h = _rms_norm(x_ref[...], g_ref[layer:layer + 1, :]).astype(BF16)
        zv_blocks = []
        y_blocks = []
        for b in range(n_blocks):
            zv_blocks.append(v_proj(b))
            zu_next = u_proj(b + 1) if b + 1 < n_blocks else None
            y_blocks.append(gate(b, _gelu_tanh(zu)))
            zu = zu_next
        acc = xp_ref[...]
        for k in range(0, n_blocks, OUT_PROJ_BLOCKS):
            acc = acc + _dot(jnp.concatenate(y_blocks[k:k + OUT_PROJ_BLOCKS], axis=1),
                             wout_ref[k * col_block:(k + OUT_PROJ_BLOCKS) * col_block, 0:d])
        o_ref[...] = acc
        v_blocks = []
        sum_v = jnp.zeros((tm, LANES), F32)
        sum_vv = jnp.zeros((tm, LANES), F32)
        for b in range(n_blocks):
            v_blocks.append(_gelu_tanh(zv_blocks[b]))
            for c in range(0, col_block, LANES):
                t = v_blocks[b][:, c:c + LANES]
                sum_v = sum_v + t
                sum_vv = sum_vv + t * t
        mu = jnp.sum(sum_v, axis=-1, keepdims=True) * (1.0 / e_a)
        var = jnp.maximum(jnp.sum(sum_vv, axis=-1, keepdims=True) * (1.0 / e_a) - mu * mu, 0.0)
        rstd = lax.rsqrt(var + EPS)
        for b in range(n_blocks):
            spatial(b)
        h_ref[...] = h
        _cast_blocks(cast_src, cast_dst)


def _sgu(x2, norms, win, vg, vb, ws, bs, wout, next_weights, *, layer, mixer, tm):
    m, d = x2.shape
    e_a = wout.shape[1]
    n_groups = ws.shape[1]
    n_tiles = m // tm
    n_prep = 8
    assert win.shape[1] % (n_prep * BF16_ROWS) == 0 and e_a % (n_prep * BF16_ROWS) == 0
    main = lambda i: jnp.maximum(i - n_prep, 0)
    prep_block = lambda i: jnp.minimum(i, n_prep - 1)
    casts = _cast_jobs(next_weights, layer, n_tiles)
    cast_specs = [c.specs(main) for c in casts]
    cur = lambda i: (jnp.minimum(main(i), n_tiles - 1), 0)
    prev = lambda i: (jnp.maximum(main(i) - 1, 0), 0)
    outs = pl.pallas_call(
        functools.partial(_sgu_kernel, layer=layer, n_casts=len(casts), n_prep=n_prep,
                          col_block=MXU_N),
        out_shape=[jax.ShapeDtypeStruct((m, d), F32)] + [c.out_shape() for c in casts],
        grid=(n_prep + n_tiles + 1,),
        in_specs=[
            pl.BlockSpec((tm, d), cur),
            pl.BlockSpec((tm, d), prev),
            _resident(norms.shape),
            pl.BlockSpec((None, win.shape[1] // n_prep, win.shape[2]),
                         lambda i: (mixer, prep_block(i), 0)),
            _resident_layer(vg, mixer),
            _resident_layer(vb, mixer),
            _resident_layer(ws, mixer),
            _resident_layer(bs, mixer),
            pl.BlockSpec((None, e_a // n_prep, d), lambda i: (mixer, prep_block(i), 0)),
        ] + [s[0] for s in cast_specs],
        out_specs=[pl.BlockSpec((tm, d), prev)] + [s[1] for s in cast_specs],
        scratch_shapes=[
            pltpu.VMEM((win.shape[1], _odd_tile_pitch(win.shape[2])), BF16),
            pltpu.VMEM((e_a, _odd_tile_pitch(d)), BF16),
            pltpu.VMEM((n_groups, CHUNK, CHUNK), BF16),
            pltpu.VMEM((CHUNK, e_a), F32),
            pltpu.VMEM((tm, d), BF16),
            pltpu.VMEM((n_groups, CHUNK, tm), F32),
        ],
        compiler_params=_compiler_params(1),
        name="sgu_mixer",
    )(x2, x2, norms, win, vg, vb, ws, bs, wout, *[c.stacked for c in casts])
    return outs[0], outs[1:]


def _conv_kernel(*refs, layer, n_casts, halo):
    x_ref, g_ref, win_ref, cw_ref, wout_ref = refs[:5]
    cast_src = refs[5:5 + n_casts]
    o_ref = refs[5 + n_casts]
    cast_dst = refs[6 + n_casts:6 + 2 * n_casts]
    z_ref, = refs[6 + 2 * n_casts:]

    tm = x_ref.shape[1]
    e_b = wout_ref.shape[0]
    conv_w = cw_ref.shape[0]
    n_blocks, _, col_block = z_ref.shape

    @pl.when(pl.program_id(1) == 0)
    def _():
        z_ref[:, 0:halo, :] = jnp.zeros((n_blocks, halo, col_block), F32)

    @pl.when(pl.program_id(1) != 0)
    def _():
        z_ref[:, 0:halo, :] = z_ref[:, tm:tm + halo, :]

    x = x_ref[0]
    h = _rms_norm(x, g_ref[layer:layer + 1, :]).astype(BF16)
    y_blocks = []
    for j in range(n_blocks):
        c = j * col_block
        cols = slice(c, c + col_block)
        z_ref[j, halo:halo + tm, :] = (_dot(h, win_ref[:, e_b + c:e_b + c + col_block])
                                       * _dot(h, win_ref[:, 2 * e_b + c:2 * e_b + c + col_block]))
        b_gate = _dot(h, win_ref[:, cols])
        conv = z_ref[j, halo:halo + tm, :] * cw_ref[conv_w - 1:conv_w, cols]
        for k in range(conv_w - 1):
            back = conv_w - 1 - k
            conv = conv + z_ref[j, halo - back:halo - back + tm, :] * cw_ref[k:k + 1, cols]
        y_blocks.append((b_gate * conv).astype(BF16))
    o_ref[0] = x + _dot(jnp.concatenate(y_blocks, axis=1), wout_ref[:, 0:x.shape[1]])
    _cast_blocks(cast_src, cast_dst)


def _short_conv(x3, norms, win, cw, wout, next_weights, *, layer, mixer, tm):
    b, s, d = x3.shape
    e_b = wout.shape[0]
    halo = SUBLANES
    assert cw.shape[1] - 1 <= halo
    tiles_per_seq = s // tm
    casts = _cast_jobs(next_weights, layer, b * tiles_per_seq)
    cast_specs = [c.specs(lambda i, j: i * tiles_per_seq + j) for c in casts]
    outs = pl.pallas_call(
        functools.partial(_conv_kernel, layer=layer, n_casts=len(casts), halo=halo),
        out_shape=[jax.ShapeDtypeStruct((b, s, d), F32)] + [c.out_shape() for c in casts],
        grid=(b, tiles_per_seq),
        in_specs=[
            pl.BlockSpec((1, tm, d), lambda i, j: (i, j, 0)),
            _resident(norms.shape),
            _resident(win.shape),
            _resident_layer(cw, mixer),
            _resident(wout.shape),
        ] + [s[0] for s in cast_specs],
        out_specs=[pl.BlockSpec((1, tm, d), lambda i, j: (i, j, 0))] + [s[1] for s in cast_specs],
        scratch_shapes=[pltpu.VMEM((e_b // MXU_N, halo + tm, MXU_N), F32)],
        compiler_params=_compiler_params(2),
        name="short_conv_mixer",
    )(x3, norms, win, cw, wout, *[c.stacked for c in casts])
    return outs[0], outs[1:]


def kernel(x, mix_norm, ffn_norm, a_w_in, a_v_gain, a_v_bias, a_w_s, a_b_s, a_w_out,
           b_w_in, b_conv_w, b_w_out, ffn_w_gate, ffn_w_up, ffn_w_down, final_norm):
    b, s, d = x.shape
    depth = mix_norm.shape[0]
    tm = 1024
    tm_sgu = 512
    assert s % tm == 0 and s % tm_sgu == 0 and tm_sgu % CHUNK == 0 and a_w_s.shape[-1] == CHUNK
    assert a_w_s.shape[1] * LANES == a_w_out.shape[1]

    fg = final_norm.reshape(1, d)
    a_v_gain3 = a_v_gain.reshape(a_v_gain.shape[0], 1, -1)
    a_v_bias3 = a_v_bias.reshape(a_v_bias.shape[0], 1, -1)
    ffn_weights = (ffn_w_gate, ffn_w_up, ffn_w_down)

    def mixer_weights(i):
        return (a_w_in, a_w_out) if i % 2 == 0 else (b_w_in, b_w_out)

    assert depth <= 2, "only the first layer's spatial-gating call casts its own weights"
    ready = None
    x2 = x.reshape(b * s, d)
    for i in range(depth):
        j = i // 2
        if i % 2 == 0:
            x2, ready = _sgu(x2, mix_norm, a_w_in, a_v_gain3, a_v_bias3, a_w_s, a_b_s, a_w_out,
                             ffn_weights, layer=i, mixer=j, tm=tm_sgu)
        else:
            x3, ready = _short_conv(x2.reshape(b, s, d), mix_norm, ready[0], b_conv_w, ready[1],
                                    ffn_weights, layer=i, mixer=j, tm=tm)
            x2 = x3.reshape(b * s, d)
        last = i == depth - 1
        x2, ready = _ffn(x2, ffn_norm, *ready, fg, () if last else mixer_weights(i + 1),
                         layer=i, next_mixer=(i + 1) // 2, tm=tm, final_norm=last)
    return x2.reshape(b, s, d)
```

```python
import functools
import math
from typing import NamedTuple

import jax
import jax.numpy as jnp
from jax import lax
from jax.experimental import pallas as pl
from jax.experimental.pallas import tpu as pltpu

EPS = 1e-6
CHUNK = 128
LANES = 128
SUBLANES = 8
BF16_ROWS = 16
MXU_N = 256
OUT_PROJ_BLOCKS = 4
DOWN_PROJ_ROW_PARTS = 4
VMEM_LIMIT_BYTES = 56 * 1024 * 1024

BF16 = jnp.bfloat16
F32 = jnp.float32


def _dot(a, b):
    return jnp.dot(a, b, preferred_element_type=F32)


def _rms_norm(x, g):
    y = x * lax.rsqrt(jnp.mean(x * x, axis=-1, keepdims=True) + EPS)
    return y * g


def _gelu_tanh(x):
    k = -2.0 * math.sqrt(2.0 / math.pi) * math.log2(math.e)
    return x / (1.0 + jnp.exp2(x * (k + (k * 0.044715) * (x * x))))


def _odd_tile_pitch(n):
    assert n % LANES == 0
    return n if (n // LANES) % 2 == 1 else n + LANES


def _resident(shape):
    return pl.BlockSpec(shape, lambda *_: (0,) * len(shape), pipeline_mode=pl.Buffered(1))


def _resident_layer(stacked, layer):
    rest = stacked.shape[1:]
    return pl.BlockSpec((None,) + rest, lambda *_: (layer,) + (0,) * len(rest),
                        pipeline_mode=pl.Buffered(1))


def _compiler_params(n_grid_axes):
    return pltpu.CompilerParams(dimension_semantics=("arbitrary",) * n_grid_axes,
                                vmem_limit_bytes=VMEM_LIMIT_BYTES)


class _Cast(NamedTuple):
    stacked: jax.Array
    layer: int
    rows: int

    @property
    def n_blocks(self):
        n_rows = self.stacked.shape[1]
        assert n_rows % self.rows == 0 and self.rows % BF16_ROWS == 0
        return n_rows // self.rows

    def out_shape(self):
        _, n_rows, n_cols = self.stacked.shape
        return jax.ShapeDtypeStruct((n_rows, _odd_tile_pitch(n_cols)), BF16)

    def specs(self, linear_step):
        n_cols = self.stacked.shape[2]
        block = lambda *g: jnp.minimum(linear_step(*g), self.n_blocks - 1)
        return (pl.BlockSpec((None, self.rows, n_cols), lambda *g: (self.layer, block(*g), 0)),
                pl.BlockSpec((self.rows, _odd_tile_pitch(n_cols)), lambda *g: (block(*g), 0)))


def _cast_jobs(weights, layer, n_steps):
    jobs = []
    for w in weights:
        n_rows = w.shape[1]
        rows = next(r for r in range(BF16_ROWS, n_rows + 1, BF16_ROWS)
                    if n_rows % r == 0 and n_rows // r <= n_steps)
        jobs.append(_Cast(w, layer, rows))
    return jobs


def _cast_blocks(src_refs, dst_refs):
    for src, dst in zip(src_refs, dst_refs):
        n = src.shape[1]
        dst[:, 0:n] = src[...].astype(BF16)
        if dst.shape[1] > n:
            dst[:, n:] = jnp.zeros((dst.shape[0], dst.shape[1] - n), BF16)


def _ffn_kernel(*refs, layer, n_casts, col_block, final_norm):
    x_ref, g_ref, wg_ref, wu_ref, wd_ref, fg_ref = refs[:6]
    cast_src = refs[6:6 + n_casts]
    o_ref = refs[6 + n_casts]
    cast_dst = refs[7 + n_casts:]

    x = x_ref[...]
    d = x.shape[1]
    h = _rms_norm(x, g_ref[layer:layer + 1, :]).astype(BF16)
    d_ff = wd_ref.shape[0]
    a_blocks = []
    for c in range(0, d_ff, col_block):
        g = _dot(h, wg_ref[:, c:c + col_block])
        u = _dot(h, wu_ref[:, c:c + col_block])
        a_blocks.append((g * (1.0 / (1.0 + jnp.exp(-g))) * u).astype(BF16))
    a = jnp.concatenate(a_blocks, axis=1)
    rows = x.shape[0] // DOWN_PROJ_ROW_PARTS
    for r in range(0, x.shape[0], rows):
        acc = x[r:r + rows] + _dot(a[r:r + rows], wd_ref[:, 0:d])
        if final_norm:
            acc = _rms_norm(acc, fg_ref[...])
        o_ref[r:r + rows, :] = acc
    _cast_blocks(cast_src, cast_dst)


def _ffn(x2, norms, wg, wu, wd, fg, next_weights, *, layer, next_mixer, tm, final_norm):
    m, d = x2.shape
    n_steps = m // tm
    casts = _cast_jobs(next_weights, next_mixer, n_steps)
    cast_specs = [c.specs(lambda i: i) for c in casts]
    outs = pl.pallas_call(
        functools.partial(_ffn_kernel, layer=layer, n_casts=len(casts), col_block=MXU_N,
                          final_norm=final_norm),
        out_shape=[jax.ShapeDtypeStruct((m, d), F32)] + [c.out_shape() for c in casts],
        grid=(n_steps,),
        in_specs=[
            pl.BlockSpec((tm, d), lambda i: (i, 0)),
            _resident(norms.shape),
            _resident(wg.shape),
            _resident(wu.shape),
            _resident(wd.shape),
            _resident(fg.shape),
        ] + [s[0] for s in cast_specs],
        out_specs=[pl.BlockSpec((tm, d), lambda i: (i, 0))] + [s[1] for s in cast_specs],
        compiler_params=_compiler_params(1),
        name="swiglu_final" if final_norm else "swiglu",
    )(x2, norms, wg, wu, wd, fg, *[c.stacked for c in casts])
    return outs[0], outs[1:]


def _sgu_kernel(*refs, layer, n_casts, n_prep, col_block):
    x_ref, xp_ref, g_ref, win_f32_ref, vg_ref, vb_ref, ws_ref, bs_ref, wout_f32_ref = refs[:9]
    cast_src = refs[9:9 + n_casts]
    o_ref = refs[9 + n_casts]
    cast_dst = refs[10 + n_casts:10 + 2 * n_casts]
    win_ref, wout_ref, wtri_ref, gate_bias_ref, h_ref, sv_ref = refs[10 + 2 * n_casts:]

    tm, d = x_ref.shape
    e_a = wout_ref.shape[0]
    n_groups = ws_ref.shape[0]
    n_chunks = tm // CHUNK
    n_blocks = e_a // col_block
    groups_per_block = col_block // LANES
    assert n_blocks % OUT_PROJ_BLOCKS == 0
    step = pl.program_id(0)

    @pl.when(step < n_prep)
    def _():
        for src, dst in ((win_f32_ref, win_ref), (wout_f32_ref, wout_ref)):
            rows, n = src.shape
            at = pl.ds(pl.multiple_of(step * rows, rows), rows)
            dst[at, 0:n] = src[...].astype(BF16)
            if dst.shape[1] > n:
                dst[at, n:] = jnp.zeros((rows, dst.shape[1] - n), BF16)

    @pl.when(step == n_prep)
    def _():
        row = lax.broadcasted_iota(jnp.int32, (CHUNK, CHUNK), 0)
        col = lax.broadcasted_iota(jnp.int32, (CHUNK, CHUNK), 1)
        for hg in range(n_groups):
            lanes = slice(hg * LANES, (hg + 1) * LANES)
            w_causal = jnp.where(col <= row, ws_ref[hg], 0.0)
            wtri_ref[hg] = w_causal.astype(BF16)
            b_col = jnp.sum(jnp.where(col == row, bs_ref[hg:hg + 1, :], 0.0), axis=1, keepdims=True)
            gate_bias_ref[:, lanes] = (jnp.sum(w_causal, axis=1, keepdims=True) * vb_ref[:, lanes]
                                       + b_col)

    def run_streams(run_u, run_v):
        def v_proj(b):
            return _dot(h, win_ref[:, e_a + b * col_block:e_a + (b + 1) * col_block])

        def spatial(b):
            vhat = ((v_blocks[b] - mu) * rstd).astype(BF16)
            for gi in range(groups_per_block):
                lanes = slice(gi * LANES, (gi + 1) * LANES)
                rhs = jnp.concatenate(
                    [vhat[n * CHUNK:(n + 1) * CHUNK, lanes] for n in range(n_chunks)], axis=1)
                sv_ref[b * groups_per_block + gi] = _dot(wtri_ref[b * groups_per_block + gi], rhs)

        def u_proj(b):
            return _dot(h_prev, win_ref[:, b * col_block:(b + 1) * col_block])

        def gate(b, u):
            y_cols = []
            for gi in range(groups_per_block):
                hg = b * groups_per_block + gi
                lanes = slice(gi * LANES, (gi + 1) * LANES)
                gain = vg_ref[:, hg * LANES:(hg + 1) * LANES]
                bias = gate_bias_ref[:, hg * LANES:(hg + 1) * LANES]
                y_cols.append(jnp.concatenate(
                    [u[n * CHUNK:(n + 1) * CHUNK, lanes]
                     * (sv_ref[hg, :, n * LANES:(n + 1) * LANES] * gain + bias)
                     for n in range(n_chunks)], axis=0))
            return jnp.concatenate(y_cols, axis=1).astype(BF16)

        if run_u:
            h_prev = h_ref[...]
            zu = u_proj(0)
        if run_v:
            h = _rms_norm(x_ref[...], g_ref[layer:layer + 1, :]).astype(BF16)
        zv_blocks = []
        y_blocks = []
        for b in range(n_blocks):
            if run_v:
                zv_blocks.append(v_proj(b))
            if run_u:
                zu_next = u_proj(b + 1) if b + 1 < n_blocks else None
                y_blocks.append(gate(b, _gelu_tanh(zu)))
                zu = zu_next
        if run_u:
            acc = xp_ref[...]
            for k in range(0, n_blocks, OUT_PROJ_BLOCKS):
                acc = acc + _dot(jnp.concatenate(y_blocks[k:k + OUT_PROJ_BLOCKS], axis=1),
                                 wout_ref[k * col_block:(k + OUT_PROJ_BLOCKS) * col_block, 0:d])
            o_ref[...] = acc
        if run_v:
            v_blocks = []
            sum_v = jnp.zeros((tm, LANES), F32)
            sum_vv = jnp.zeros((tm, LANES), F32)
            for b in range(n_blocks):
                v_blocks.append(_gelu_tanh(zv_blocks[b]))
                for c in range(0, col_block, LANES):
                    t = v_blocks[b][:, c:c + LANES]
                    sum_v = sum_v + t
                    sum_vv = sum_vv + t * t
            mu = jnp.sum(sum_v, axis=-1, keepdims=True) * (1.0 / e_a)
            var = jnp.maximum(jnp.sum(sum_vv, axis=-1, keepdims=True) * (1.0 / e_a) - mu * mu, 0.0)
            rstd = lax.rsqrt(var + EPS)
            for b in range(n_blocks):
                spatial(b)
            h_ref[...] = h
        _cast_blocks(cast_src, cast_dst)

    last = pl.num_programs(0) - 1
    pl.when(step == n_prep)(functools.partial(run_streams, False, True))
    pl.when((step > n_prep) & (step < last))(functools.partial(run_streams, True, True))
    pl.when(step == last)(functools.partial(run_streams, True, False))


def _sgu(x2, norms, win, vg, vb, ws, bs, wout, next_weights, *, layer, mixer, tm):
    m, d = x2.shape
    e_a = wout.shape[1]
    n_groups = ws.shape[1]
    n_tiles = m // tm
    n_prep = 8
    assert win.shape[1] % (n_prep * BF16_ROWS) == 0 and e_a % (n_prep * BF16_ROWS) == 0
    main = lambda i: jnp.maximum(i - n_prep, 0)
    prep_block = lambda i: jnp.minimum(i, n_prep - 1)
    casts = _cast_jobs(next_weights, layer, n_tiles)
    cast_specs = [c.specs(main) for c in casts]
    cur = lambda i: (jnp.minimum(main(i), n_tiles - 1), 0)
    prev = lambda i: (jnp.maximum(main(i) - 1, 0), 0)
    outs = pl.pallas_call(
        functools.partial(_sgu_kernel, layer=layer, n_casts=len(casts), n_prep=n_prep,
                          col_block=MXU_N),
        out_shape=[jax.ShapeDtypeStruct((m, d), F32)] + [c.out_shape() for c in casts],
        grid=(n_prep + n_tiles + 1,),
        in_specs=[
            pl.BlockSpec((tm, d), cur),
            pl.BlockSpec((tm, d), prev),
            _resident(norms.shape),
            pl.BlockSpec((None, win.shape[1] // n_prep, win.shape[2]),
                         lambda i: (mixer, prep_block(i), 0)),
            _resident_layer(vg, mixer),
            _resident_layer(vb, mixer),
            _resident_layer(ws, mixer),
            _resident_layer(bs, mixer),
            pl.BlockSpec((None, e_a // n_prep, d), lambda i: (mixer, prep_block(i), 0)),
        ] + [s[0] for s in cast_specs],
        out_specs=[pl.BlockSpec((tm, d), prev)] + [s[1] for s in cast_specs],
        scratch_shapes=[
            pltpu.VMEM((win.shape[1], _odd_tile_pitch(win.shape[2])), BF16),
            pltpu.VMEM((e_a, _odd_tile_pitch(d)), BF16),
            pltpu.VMEM((n_groups, CHUNK, CHUNK), BF16),
            pltpu.VMEM((CHUNK, e_a), F32),
            pltpu.VMEM((tm, d), BF16),
            pltpu.VMEM((n_groups, CHUNK, tm), F32),
        ],
        compiler_params=_compiler_params(1),
        name="sgu_mixer",
    )(x2, x2, norms, win, vg, vb, ws, bs, wout, *[c.stacked for c in casts])
    return outs[0], outs[1:]


def _conv_kernel(*refs, layer, n_casts, halo):
    x_ref, g_ref, win_ref, cw_ref, wout_ref = refs[:5]
    cast_src = refs[5:5 + n_casts]
    o_ref = refs[5 + n_casts]
    cast_dst = refs[6 + n_casts:6 + 2 * n_casts]
    z_ref, = refs[6 + 2 * n_casts:]

    tm = x_ref.shape[1]
    e_b = wout_ref.shape[0]
    conv_w = cw_ref.shape[0]
    n_blocks, _, col_block = z_ref.shape

    @pl.when(pl.program_id(1) == 0)
    def _():
        z_ref[:, 0:halo, :] = jnp.zeros((n_blocks, halo, col_block), F32)

    @pl.when(pl.program_id(1) != 0)
    def _():
        z_ref[:, 0:halo, :] = z_ref[:, tm:tm + halo, :]

    x = x_ref[0]
    h = _rms_norm(x, g_ref[layer:layer + 1, :]).astype(BF16)
    y_blocks = []
    for j in range(n_blocks):
        c = j * col_block
        cols = slice(c, c + col_block)
        z_ref[j, halo:halo + tm, :] = (_dot(h, win_ref[:, e_b + c:e_b + c + col_block])
                                       * _dot(h, win_ref[:, 2 * e_b + c:2 * e_b + c + col_block]))
        b_gate = _dot(h, win_ref[:, cols])
        conv = z_ref[j, halo:halo + tm, :] * cw_ref[conv_w - 1:conv_w, cols]
        for k in range(conv_w - 1):
            back = conv_w - 1 - k
            conv = conv + z_ref[j, halo - back:halo - back + tm, :] * cw_ref[k:k + 1, cols]
        y_blocks.append((b_gate * conv).astype(BF16))
    o_ref[0] = x + _dot(jnp.concatenate(y_blocks, axis=1), wout_ref[:, 0:x.shape[1]])
    _cast_blocks(cast_src, cast_dst)


def _short_conv(x3, norms, win, cw, wout, next_weights, *, layer, mixer, tm):
    b, s, d = x3.shape
    e_b = wout.shape[0]
    halo = SUBLANES
    assert cw.shape[1] - 1 <= halo
    tiles_per_seq = s // tm
    casts = _cast_jobs(next_weights, layer, b * tiles_per_seq)
    cast_specs = [c.specs(lambda i, j: i * tiles_per_seq + j) for c in casts]
    outs = pl.pallas_call(
        functools.partial(_conv_kernel, layer=layer, n_casts=len(casts), halo=halo),
        out_shape=[jax.ShapeDtypeStruct((b, s, d), F32)] + [c.out_shape() for c in casts],
        grid=(b, tiles_per_seq),
        in_specs=[
            pl.BlockSpec((1, tm, d), lambda i, j: (i, j, 0)),
            _resident(norms.shape),
            _resident(win.shape),
            _resident_layer(cw, mixer),
            _resident(wout.shape),
        ] + [s[0] for s in cast_specs],
        out_specs=[pl.BlockSpec((1, tm, d), lambda i, j: (i, j, 0))] + [s[1] for s in cast_specs],
        scratch_shapes=[pltpu.VMEM((e_b // MXU_N, halo + tm, MXU_N), F32)],
        compiler_params=_compiler_params(2),
        name="short_conv_mixer",
    )(x3, norms, win, cw, wout, *[c.stacked for c in casts])
    return outs[0], outs[1:]


def kernel(x, mix_norm, ffn_norm, a_w_in, a_v_gain, a_v_bias, a_w_s, a_b_s, a_w_out,
           b_w_in, b_conv_w, b_w_out, ffn_w_gate, ffn_w_up, ffn_w_down, final_norm):
    b, s, d = x.shape
    depth = mix_norm.shape[0]
    tm = 1024
    tm_sgu = 512
    assert s % tm == 0 and s % tm_sgu == 0 and tm_sgu % CHUNK == 0 and a_w_s.shape[-1] == CHUNK
    assert a_w_s.shape[1] * LANES == a_w_out.shape[1]

    fg = final_norm.reshape(1, d)
    a_v_gain3 = a_v_gain.reshape(a_v_gain.shape[0], 1, -1)
    a_v_bias3 = a_v_bias.reshape(a_v_bias.shape[0], 1, -1)
    ffn_weights = (ffn_w_gate, ffn_w_up, ffn_w_down)

    def mixer_weights(i):
        return (a_w_in, a_w_out) if i % 2 == 0 else (b_w_in, b_w_out)

    assert depth <= 2, "only the first layer's spatial-gating call casts its own weights"
    ready = None
    x2 = x.reshape(b * s, d)
    for i in range(depth):
        j = i // 2
        if i % 2 == 0:
            x2, ready = _sgu(x2, mix_norm, a_w_in, a_v_gain3, a_v_bias3, a_w_s, a_b_s, a_w_out,
                             ffn_weights, layer=i, mixer=j, tm=tm_sgu)
        else:
            x3, ready = _short_conv(x2.reshape(b, s, d), mix_norm, ready[0], b_conv_w, ready[1],
                                    ffn_weights, layer=i, mixer=j, tm=tm)
            x2 = x3.reshape(b * s, d)
        last = i == depth - 1
        x2, ready = _ffn(x2, ffn_norm, *ready, fg, () if last else mixer_weights(i + 1),
                         layer=i, next_mixer=(i + 1) // 2, tm=tm, final_norm=last)
    return x2.reshape(b, s, d)
```

```python
import functools
import math
from typing import NamedTuple

import jax
import jax.numpy as jnp
from jax import lax
from jax.experimental import pallas as pl
from jax.experimental.pallas import tpu as pltpu

EPS = 1e-6
CHUNK = 128
LANES = 128
SUBLANES = 8
BF16_ROWS = 16
MXU_N = 256
OUT_PROJ_BLOCKS = 4
DOWN_PROJ_ROW_PARTS = 4
VMEM_LIMIT_BYTES = 56 * 1024 * 1024

BF16 = jnp.bfloat16
F32 = jnp.float32


def _dot(a, b):
    return jnp.dot(a, b, preferred_element_type=F32)


def _rms_norm(x, g):
    y = x * lax.rsqrt(jnp.mean(x * x, axis=-1, keepdims=True) + EPS)
    return y * g


def _gelu_tanh(x):
    k = -2.0 * math.sqrt(2.0 / math.pi) * math.log2(math.e)
    return x / (1.0 + jnp.exp2(x * (k + (k * 0.044715) * (x * x))))


def _odd_tile_pitch(n):
    assert n % LANES == 0
    return n if (n // LANES) % 2 == 1 else n + LANES


def _resident(shape):
    return pl.BlockSpec(shape, lambda *_: (0,) * len(shape), pipeline_mode=pl.Buffered(1))


def _resident_layer(stacked, layer):
    rest = stacked.shape[1:]
    return pl.BlockSpec((None,) + rest, lambda *_: (layer,) + (0,) * len(rest),
                        pipeline_mode=pl.Buffered(1))


def _compiler_params(n_grid_axes):
    return pltpu.CompilerParams(dimension_semantics=("arbitrary",) * n_grid_axes,
                                vmem_limit_bytes=VMEM_LIMIT_BYTES)


class _Cast(NamedTuple):
    stacked: jax.Array
    layer: int
    rows: int

    @property
    def n_blocks(self):
        n_rows = self.stacked.shape[1]
        assert n_rows % self.rows == 0 and self.rows % BF16_ROWS == 0
        return n_rows // self.rows

    def out_shape(self):
        _, n_rows, n_cols = self.stacked.shape
        return jax.ShapeDtypeStruct((n_rows, _odd_tile_pitch(n_cols)), BF16)

    def specs(self, linear_step):
        n_cols = self.stacked.shape[2]
        block = lambda *g: jnp.minimum(linear_step(*g), self.n_blocks - 1)
        return (pl.BlockSpec((None, self.rows, n_cols), lambda *g: (self.layer, block(*g), 0)),
                pl.BlockSpec((self.rows, _odd_tile_pitch(n_cols)), lambda *g: (block(*g), 0)))


def _cast_jobs(weights, layer, n_steps):
    jobs = []
    for w in weights:
        n_rows = w.shape[1]
        rows = next(r for r in range(BF16_ROWS, n_rows + 1, BF16_ROWS)
                    if n_rows % r == 0 and n_rows // r <= n_steps)
        jobs.append(_Cast(w, layer, rows))
    return jobs


def _cast_blocks(src_refs, dst_refs):
    for src, dst in zip(src_refs, dst_refs):
        n = src.shape[1]
        dst[:, 0:n] = src[...].astype(BF16)
        if dst.shape[1] > n:
            dst[:, n:] = jnp.zeros((dst.shape[0], dst.shape[1] - n), BF16)


def _ffn_kernel(*refs, layer, n_casts, col_block, final_norm):
    x_ref, g_ref, wg_ref, wu_ref, wd_ref, fg_ref = refs[:6]
    cast_src = refs[6:6 + n_casts]
    o_ref = refs[6 + n_casts]
    cast_dst = refs[7 + n_casts:]

    x = x_ref[...]
    d = x.shape[1]
    h = _rms_norm(x, g_ref[layer:layer + 1, :]).astype(BF16)
    d_ff = wd_ref.shape[0]
    a_blocks = []
    for c in range(0, d_ff, col_block):
        g = _dot(h, wg_ref[:, c:c + col_block])
        u = _dot(h, wu_ref[:, c:c + col_block])
        a_blocks.append((g * (1.0 / (1.0 + jnp.exp(-g))) * u).astype(BF16))
    a = jnp.concatenate(a_blocks, axis=1)
    rows = x.shape[0] // DOWN_PROJ_ROW_PARTS
    for r in range(0, x.shape[0], rows):
        acc = x[r:r + rows] + _dot(a[r:r + rows], wd_ref[:, 0:d])
        if final_norm:
            acc = _rms_norm(acc, fg_ref[...])
        o_ref[r:r + rows, :] = acc
    _cast_blocks(cast_src, cast_dst)


def _ffn(x2, norms, wg, wu, wd, fg, next_weights, *, layer, next_mixer, tm, final_norm):
    m, d = x2.shape
    n_steps = m // tm
    casts = _cast_jobs(next_weights, next_mixer, n_steps)
    cast_specs = [c.specs(lambda i: i) for c in casts]
    outs = pl.pallas_call(
        functools.partial(_ffn_kernel, layer=layer, n_casts=len(casts), col_block=MXU_N,
                          final_norm=final_norm),
        out_shape=[jax.ShapeDtypeStruct((m, d), F32)] + [c.out_shape() for c in casts],
        grid=(n_steps,),
        in_specs=[
            pl.BlockSpec((tm, d), lambda i: (i, 0)),
            _resident(norms.shape),
            _resident(wg.shape),
            _resident(wu.shape),
            _resident(wd.shape),
            _resident(fg.shape),
        ] + [s[0] for s in cast_specs],
        out_specs=[pl.BlockSpec((tm, d), lambda i: (i, 0))] + [s[1] for s in cast_specs],
        compiler_params=_compiler_params(1),
        name="swiglu_final" if final_norm else "swiglu",
    )(x2, norms, wg, wu, wd, fg, *[c.stacked for c in casts])
    return outs[0], outs[1:]


def _sgu_kernel(*refs, layer, n_casts, n_prep, col_block):
    x_ref, xp_ref, g_ref, win_f32_ref, vg_ref, vb_ref, ws_ref, bs_ref, wout_f32_ref = refs[:9]
    cast_src = refs[9:9 + n_casts]
    o_ref = refs[9 + n_casts]
    cast_dst = refs[10 + n_casts:10 + 2 * n_casts]
    win_ref, wout_ref, wtri_ref, gate_bias_ref, h_ref, sv_ref = refs[10 + 2 * n_casts:]

    tm, d = x_ref.shape
    e_a = wout_ref.shape[0]
    n_groups = ws_ref.shape[0]
    n_chunks = tm // CHUNK
    n_blocks = e_a // col_block
    groups_per_block = col_block // LANES
    assert n_blocks % OUT_PROJ_BLOCKS == 0
    step = pl.program_id(0)

    @pl.when(step < n_prep)
    def _():
        for src, dst in ((win_f32_ref, win_ref), (wout_f32_ref, wout_ref)):
            rows, n = src.shape
            at = pl.ds(pl.multiple_of(step * rows, rows), rows)
            dst[at, 0:n] = src[...].astype(BF16)
            if dst.shape[1] > n:
                dst[at, n:] = jnp.zeros((rows, dst.shape[1] - n), BF16)

    @pl.when(step == 0)
    def _():
        row = lax.broadcasted_iota(jnp.int32, (CHUNK, CHUNK), 0)
        col = lax.broadcasted_iota(jnp.int32, (CHUNK, CHUNK), 1)
        for hg in range(n_groups):
            lanes = slice(hg * LANES, (hg + 1) * LANES)
            w_causal = jnp.where(col <= row, ws_ref[hg], 0.0)
            wtri_ref[hg] = w_causal.astype(BF16)
            b_col = jnp.sum(jnp.where(col == row, bs_ref[hg:hg + 1, :], 0.0), axis=1, keepdims=True)
            gate_bias_ref[:, lanes] = (jnp.sum(w_causal, axis=1, keepdims=True) * vb_ref[:, lanes]
                                       + b_col)

    def run_streams(run_u, run_v):
        def v_proj(b):
            return _dot(h, win_ref[:, e_a + b * col_block:e_a + (b + 1) * col_block])

        def spatial(b):
            vhat = ((v_blocks[b] - mu) * rstd).astype(BF16)
            for gi in range(groups_per_block):
                lanes = slice(gi * LANES, (gi + 1) * LANES)
                rhs = jnp.concatenate(
                    [vhat[n * CHUNK:(n + 1) * CHUNK, lanes] for n in range(n_chunks)], axis=1)
                sv_ref[b * groups_per_block + gi] = _dot(wtri_ref[b * groups_per_block + gi], rhs)

        def u_proj(b):
            return _dot(h_prev, win_ref[:, b * col_block:(b + 1) * col_block])

        def gate(b, u):
            y_cols = []
            for gi in range(groups_per_block):
                hg = b * groups_per_block + gi
                lanes = slice(gi * LANES, (gi + 1) * LANES)
                gain = vg_ref[:, hg * LANES:(hg + 1) * LANES]
                bias = gate_bias_ref[:, hg * LANES:(hg + 1) * LANES]
                y_cols.append(jnp.concatenate(
                    [u[n * CHUNK:(n + 1) * CHUNK, lanes]
                     * (sv_ref[hg, :, n * LANES:(n + 1) * LANES] * gain + bias)
                     for n in range(n_chunks)], axis=0))
            return jnp.concatenate(y_cols, axis=1).astype(BF16)

        if run_u:
            h_prev = h_ref[...]
            zu = u_proj(0)
        if run_v:
            h = _rms_norm(x_ref[...], g_ref[layer:layer + 1, :]).astype(BF16)
        zv_blocks = []
        y_blocks = []
        for b in range(n_blocks):
            if run_v:
                zv_blocks.append(v_proj(b))
            if run_u:
                zu_next = u_proj(b + 1) if b + 1 < n_blocks else None
                y_blocks.append(gate(b, _gelu_tanh(zu)))
                zu = zu_next
        if run_u:
            acc = xp_ref[...]
            for k in range(0, n_blocks, OUT_PROJ_BLOCKS):
                acc = acc + _dot(jnp.concatenate(y_blocks[k:k + OUT_PROJ_BLOCKS], axis=1),
                                 wout_ref[k * col_block:(k + OUT_PROJ_BLOCKS) * col_block, 0:d])
            o_ref[...] = acc
        if run_v:
            v_blocks = []
            sum_v = jnp.zeros((tm, LANES), F32)
            sum_vv = jnp.zeros((tm, LANES), F32)
            for b in range(n_blocks):
                v_blocks.append(_gelu_tanh(zv_blocks[b]))
                for c in range(0, col_block, LANES):
                    t = v_blocks[b][:, c:c + LANES]
                    sum_v = sum_v + t
                    sum_vv = sum_vv + t * t
            mu = jnp.sum(sum_v, axis=-1, keepdims=True) * (1.0 / e_a)
            var = jnp.maximum(jnp.sum(sum_vv, axis=-1, keepdims=True) * (1.0 / e_a) - mu * mu, 0.0)
            rstd = lax.rsqrt(var + EPS)
            for b in range(n_blocks):
                spatial(b)
            h_ref[...] = h
        _cast_blocks(cast_src, cast_dst)

    last = pl.num_programs(0) - 1
    pl.when(step == n_prep)(functools.partial(run_streams, False, True))
    pl.when((step > n_prep) & (step < last))(functools.partial(run_streams, True, True))
    pl.when(step == last)(functools.partial(run_streams, True, False))


def _sgu(x2, norms, win, vg, vb, ws, bs, wout, next_weights, *, layer, mixer, tm):
    m, d = x2.shape
    e_a = wout.shape[1]
    n_groups = ws.shape[1]
    n_tiles = m // tm
    n_prep = 4
    assert win.shape[1] % (n_prep * BF16_ROWS) == 0 and e_a % (n_prep * BF16_ROWS) == 0
    main = lambda i: jnp.maximum(i - n_prep, 0)
    prep_block = lambda i: jnp.minimum(i, n_prep - 1)
    casts = _cast_jobs(next_weights, layer, n_tiles)
    cast_specs = [c.specs(main) for c in casts]
    cur = lambda i: (jnp.minimum(main(i), n_tiles - 1), 0)
    prev = lambda i: (jnp.maximum(main(i) - 1, 0), 0)
    outs = pl.pallas_call(
        functools.partial(_sgu_kernel, layer=layer, n_casts=len(casts), n_prep=n_prep,
                          col_block=MXU_N),
        out_shape=[jax.ShapeDtypeStruct((m, d), F32)] + [c.out_shape() for c in casts],
        grid=(n_prep + n_tiles + 1,),
        in_specs=[
            pl.BlockSpec((tm, d), cur),
            pl.BlockSpec((tm, d), prev),
            _resident(norms.shape),
            pl.BlockSpec((None, win.shape[1] // n_prep, win.shape[2]),
                         lambda i: (mixer, prep_block(i), 0)),
            _resident_layer(vg, mixer),
            _resident_layer(vb, mixer),
            _resident_layer(ws, mixer),
            _resident_layer(bs, mixer),
            pl.BlockSpec((None, e_a // n_prep, d), lambda i: (mixer, prep_block(i), 0)),
        ] + [s[0] for s in cast_specs],
        out_specs=[pl.BlockSpec((tm, d), prev)] + [s[1] for s in cast_specs],
        scratch_shapes=[
            pltpu.VMEM((win.shape[1], _odd_tile_pitch(win.shape[2])), BF16),
            pltpu.VMEM((e_a, _odd_tile_pitch(d)), BF16),
            pltpu.VMEM((n_groups, CHUNK, CHUNK), BF16),
            pltpu.VMEM((CHUNK, e_a), F32),
            pltpu.VMEM((tm, d), BF16),
            pltpu.VMEM((n_groups, CHUNK, tm), F32),
        ],
        compiler_params=_compiler_params(1),
        name="sgu_mixer",
    )(x2, x2, norms, win, vg, vb, ws, bs, wout, *[c.stacked for c in casts])
    return outs[0], outs[1:]


def _conv_kernel(*refs, layer, n_casts, halo):
    x_ref, g_ref, win_ref, cw_ref, wout_ref = refs[:5]
    cast_src = refs[5:5 + n_casts]
    o_ref = refs[5 + n_casts]
    cast_dst = refs[6 + n_casts:6 + 2 * n_casts]
    z_ref, = refs[6 + 2 * n_casts:]

    tm = x_ref.shape[1]
    e_b = wout_ref.shape[0]
    conv_w = cw_ref.shape[0]
    n_blocks, _, col_block = z_ref.shape

    @pl.when(pl.program_id(1) == 0)
    def _():
        z_ref[:, 0:halo, :] = jnp.zeros((n_blocks, halo, col_block), F32)

    @pl.when(pl.program_id(1) != 0)
    def _():
        z_ref[:, 0:halo, :] = z_ref[:, tm:tm + halo, :]

    x = x_ref[0]
    h = _rms_norm(x, g_ref[layer:layer + 1, :]).astype(BF16)
    y_blocks = []
    for j in range(n_blocks):
        c = j * col_block
        cols = slice(c, c + col_block)
        z_ref[j, halo:halo + tm, :] = (_dot(h, win_ref[:, e_b + c:e_b + c + col_block])
                                       * _dot(h, win_ref[:, 2 * e_b + c:2 * e_b + c + col_block]))
        b_gate = _dot(h, win_ref[:, cols])
        conv = z_ref[j, halo:halo + tm, :] * cw_ref[conv_w - 1:conv_w, cols]
        for k in range(conv_w - 1):
            back = conv_w - 1 - k
            conv = conv + z_ref[j, halo - back:halo - back + tm, :] * cw_ref[k:k + 1, cols]
        y_blocks.append((b_gate * conv).astype(BF16))
    o_ref[0] = x + _dot(jnp.concatenate(y_blocks, axis=1), wout_ref[:, 0:x.shape[1]])
    _cast_blocks(cast_src, cast_dst)


def _short_conv(x3, norms, win, cw, wout, next_weights, *, layer, mixer, tm):
    b, s, d = x3.shape
    e_b = wout.shape[0]
    halo = SUBLANES
    assert cw.shape[1] - 1 <= halo
    tiles_per_seq = s // tm
    casts = _cast_jobs(next_weights, layer, b * tiles_per_seq)
    cast_specs = [c.specs(lambda i, j: i * tiles_per_seq + j) for c in casts]
    outs = pl.pallas_call(
        functools.partial(_conv_kernel, layer=layer, n_casts=len(casts), halo=halo),
        out_shape=[jax.ShapeDtypeStruct((b, s, d), F32)] + [c.out_shape() for c in casts],
        grid=(b, tiles_per_seq),
        in_specs=[
            pl.BlockSpec((1, tm, d), lambda i, j: (i, j, 0)),
            _resident(norms.shape),
            _resident(win.shape),
            _resident_layer(cw, mixer),
            _resident(wout.shape),
        ] + [s[0] for s in cast_specs],
        out_specs=[pl.BlockSpec((1, tm, d), lambda i, j: (i, j, 0))] + [s[1] for s in cast_specs],
        scratch_shapes=[pltpu.VMEM((e_b // MXU_N, halo + tm, MXU_N), F32)],
        compiler_params=_compiler_params(2),
        name="short_conv_mixer",
    )(x3, norms, win, cw, wout, *[c.stacked for c in casts])
    return outs[0], outs[1:]


def kernel(x, mix_norm, ffn_norm, a_w_in, a_v_gain, a_v_bias, a_w_s, a_b_s, a_w_out,
           b_w_in, b_conv_w, b_w_out, ffn_w_gate, ffn_w_up, ffn_w_down, final_norm):
    b, s, d = x.shape
    depth = mix_norm.shape[0]
    tm = 1024
    tm_sgu = 512
    assert s % tm == 0 and s % tm_sgu == 0 and tm_sgu % CHUNK == 0 and a_w_s.shape[-1] == CHUNK
    assert a_w_s.shape[1] * LANES == a_w_out.shape[1]

    fg = final_norm.reshape(1, d)
    a_v_gain3 = a_v_gain.reshape(a_v_gain.shape[0], 1, -1)
    a_v_bias3 = a_v_bias.reshape(a_v_bias.shape[0], 1, -1)
    ffn_weights = (ffn_w_gate, ffn_w_up, ffn_w_down)

    def mixer_weights(i):
        return (a_w_in, a_w_out) if i % 2 == 0 else (b_w_in, b_w_out)

    assert depth <= 2, "only the first layer's spatial-gating call casts its own weights"
    ready = None
    x2 = x.reshape(b * s, d)
    for i in range(depth):
        j = i // 2
        if i % 2 == 0:
            x2, ready = _sgu(x2, mix_norm, a_w_in, a_v_gain3, a_v_bias3, a_w_s, a_b_s, a_w_out,
                             ffn_weights, layer=i, mixer=j, tm=tm_sgu)
        else:
            x3, ready = _short_conv(x2.reshape(b, s, d), mix_norm, ready[0], b_conv_w, ready[1],
                                    ffn_weights, layer=i, mixer=j, tm=tm)
            x2 = x3.reshape(b * s, d)
        last = i == depth - 1
        x2, ready = _ffn(x2, ffn_norm, *ready, fg, () if last else mixer_weights(i + 1),
                         layer=i, next_mixer=(i + 1) // 2, tm=tm, final_norm=last)
    return x2.reshape(b, s, d)
```

```python
import functools
import math
from typing import NamedTuple

import jax
import jax.numpy as jnp
from jax import lax
from jax.experimental import pallas as pl
from jax.experimental.pallas import tpu as pltpu

EPS = 1e-6
CHUNK = 128
LANES = 128
SUBLANES = 8
BF16_ROWS = 16
MXU_N = 256
OUT_PROJ_BLOCKS = 4
DOWN_PROJ_ROW_PARTS = 4
VMEM_LIMIT_BYTES = 56 * 1024 * 1024

BF16 = jnp.bfloat16
F32 = jnp.float32


def _dot(a, b):
    return jnp.dot(a, b, preferred_element_type=F32)


def _rms_norm(x, g):
    y = x * lax.rsqrt(jnp.mean(x * x, axis=-1, keepdims=True) + EPS)
    return y * g


def _gelu_tanh(x):
    k = -2.0 * math.sqrt(2.0 / math.pi) * math.log2(math.e)
    return x / (1.0 + jnp.exp2(x * (k + (k * 0.044715) * (x * x))))


def _odd_tile_pitch(n):
    assert n % LANES == 0
    return n if (n // LANES) % 2 == 1 else n + LANES


def _resident(shape):
    return pl.BlockSpec(shape, lambda *_: (0,) * len(shape), pipeline_mode=pl.Buffered(1))


def _resident_layer(stacked, layer):
    rest = stacked.shape[1:]
    return pl.BlockSpec((None,) + rest, lambda *_: (layer,) + (0,) * len(rest),
                        pipeline_mode=pl.Buffered(1))


def _compiler_params(n_grid_axes):
    return pltpu.CompilerParams(dimension_semantics=("arbitrary",) * n_grid_axes,
                                vmem_limit_bytes=VMEM_LIMIT_BYTES)


class _Cast(NamedTuple):
    stacked: jax.Array
    layer: int
    rows: int

    @property
    def n_blocks(self):
        n_rows = self.stacked.shape[1]
        assert n_rows % self.rows == 0 and self.rows % BF16_ROWS == 0
        return n_rows // self.rows

    def out_shape(self):
        _, n_rows, n_cols = self.stacked.shape
        return jax.ShapeDtypeStruct((n_rows, _odd_tile_pitch(n_cols)), BF16)

    def specs(self, linear_step):
        n_cols = self.stacked.shape[2]
        block = lambda *g: jnp.minimum(linear_step(*g), self.n_blocks - 1)
        return (pl.BlockSpec((None, self.rows, n_cols), lambda *g: (self.layer, block(*g), 0)),
                pl.BlockSpec((self.rows, _odd_tile_pitch(n_cols)), lambda *g: (block(*g), 0)))


def _cast_jobs(weights, layer, n_steps):
    jobs = []
    for w in weights:
        n_rows = w.shape[1]
        rows = next(r for r in range(BF16_ROWS, n_rows + 1, BF16_ROWS)
                    if n_rows % r == 0 and n_rows // r <= n_steps)
        jobs.append(_Cast(w, layer, rows))
    return jobs


def _cast_blocks(src_refs, dst_refs):
    for src, dst in zip(src_refs, dst_refs):
        n = src.shape[1]
        dst[:, 0:n] = src[...].astype(BF16)
        if dst.shape[1] > n:
            dst[:, n:] = jnp.zeros((dst.shape[0], dst.shape[1] - n), BF16)


def _ffn_kernel(*refs, layer, n_casts, col_block, final_norm):
    x_ref, g_ref, wg_ref, wu_ref, wd_ref, fg_ref = refs[:6]
    cast_src = refs[6:6 + n_casts]
    o_ref = refs[6 + n_casts]
    cast_dst = refs[7 + n_casts:]

    x = x_ref[...]
    d = x.shape[1]
    h = _rms_norm(x, g_ref[layer:layer + 1, :]).astype(BF16)
    d_ff = wd_ref.shape[0]
    a_blocks = []
    for c in range(0, d_ff, col_block):
        g = _dot(h, wg_ref[:, c:c + col_block])
        u = _dot(h, wu_ref[:, c:c + col_block])
        a_blocks.append((g * (1.0 / (1.0 + jnp.exp(-g))) * u).astype(BF16))
    a = jnp.concatenate(a_blocks, axis=1)
    rows = x.shape[0] // DOWN_PROJ_ROW_PARTS
    for r in range(0, x.shape[0], rows):
        acc = x[r:r + rows] + _dot(a[r:r + rows], wd_ref[:, 0:d])
        if final_norm:
            acc = _rms_norm(acc, fg_ref[...])
        o_ref[r:r + rows, :] = acc
    _cast_blocks(cast_src, cast_dst)


def _ffn(x2, norms, wg, wu, wd, fg, next_weights, *, layer, next_mixer, tm, final_norm):
    m, d = x2.shape
    n_steps = m // tm
    casts = _cast_jobs(next_weights, next_mixer, n_steps)
    cast_specs = [c.specs(lambda i: i) for c in casts]
    outs = pl.pallas_call(
        functools.partial(_ffn_kernel, layer=layer, n_casts=len(casts), col_block=MXU_N,
                          final_norm=final_norm),
        out_shape=[jax.ShapeDtypeStruct((m, d), F32)] + [c.out_shape() for c in casts],
        grid=(n_steps,),
        in_specs=[
            pl.BlockSpec((tm, d), lambda i: (i, 0)),
            _resident(norms.shape),
            _resident(wg.shape),
            _resident(wu.shape),
            _resident(wd.shape),
            _resident(fg.shape),
        ] + [s[0] for s in cast_specs],
        out_specs=[pl.BlockSpec((tm, d), lambda i: (i, 0))] + [s[1] for s in cast_specs],
        compiler_params=_compiler_params(1),
        name="swiglu_final" if final_norm else "swiglu",
    )(x2, norms, wg, wu, wd, fg, *[c.stacked for c in casts])
    return outs[0], outs[1:]


def _sgu_kernel(*refs, layer, n_casts, n_prep, col_block):
    x_ref, xp_ref, g_ref, win_f32_ref, vg_ref, vb_ref, ws_ref, bs_ref, wout_f32_ref = refs[:9]
    cast_src = refs[9:9 + n_casts]
    o_ref = refs[9 + n_casts]
    cast_dst = refs[10 + n_casts:10 + 2 * n_casts]
    win_ref, wout_ref, wtri_ref, gate_bias_ref, h_ref, sv_ref = refs[10 + 2 * n_casts:]

    tm, d = x_ref.shape
    e_a = wout_ref.shape[0]
    n_groups = ws_ref.shape[0]
    n_chunks = tm // CHUNK
    n_blocks = e_a // col_block
    groups_per_block = col_block // LANES
    assert n_blocks % OUT_PROJ_BLOCKS == 0
    step = pl.program_id(0)

    @pl.when(step < n_prep)
    def _():
        for src, dst in ((win_f32_ref, win_ref), (wout_f32_ref, wout_ref)):
            rows, n = src.shape
            at = pl.ds(pl.multiple_of(step * rows, rows), rows)
            dst[at, 0:n] = src[...].astype(BF16)
            if dst.shape[1] > n:
                dst[at, n:] = jnp.zeros((rows, dst.shape[1] - n), BF16)

    @pl.when(step == 0)
    def _():
        row = lax.broadcasted_iota(jnp.int32, (CHUNK, CHUNK), 0)
        col = lax.broadcasted_iota(jnp.int32, (CHUNK, CHUNK), 1)
        for hg in range(n_groups):
            lanes = slice(hg * LANES, (hg + 1) * LANES)
            w_causal = jnp.where(col <= row, ws_ref[hg], 0.0)
            wtri_ref[hg] = w_causal.astype(BF16)
            b_col = jnp.sum(jnp.where(col == row, bs_ref[hg:hg + 1, :], 0.0), axis=1, keepdims=True)
            gate_bias_ref[:, lanes] = (jnp.sum(w_causal, axis=1, keepdims=True) * vb_ref[:, lanes]
                                       + b_col)

    def run_streams(run_u, run_v):
        def v_proj(b):
            return _dot(h, win_ref[:, e_a + b * col_block:e_a + (b + 1) * col_block])

        def spatial(b):
            vhat = ((v_blocks[b] - mu) * rstd).astype(BF16)
            for gi in range(groups_per_block):
                lanes = slice(gi * LANES, (gi + 1) * LANES)
                rhs = jnp.concatenate(
                    [vhat[n * CHUNK:(n + 1) * CHUNK, lanes] for n in range(n_chunks)], axis=1)
                sv_ref[b * groups_per_block + gi] = _dot(wtri_ref[b * groups_per_block + gi], rhs)

        def u_proj(b):
            return _dot(h_prev, win_ref[:, b * col_block:(b + 1) * col_block])

        def gate(b, u):
            y_cols = []
            for gi in range(groups_per_block):
                hg = b * groups_per_block + gi
                lanes = slice(gi * LANES, (gi + 1) * LANES)
                gain = vg_ref[:, hg * LANES:(hg + 1) * LANES]
                bias = gate_bias_ref[:, hg * LANES:(hg + 1) * LANES]
                y_cols.append(jnp.concatenate(
                    [u[n * CHUNK:(n + 1) * CHUNK, lanes]
                     * (sv_ref[hg, :, n * LANES:(n + 1) * LANES] * gain + bias)
                     for n in range(n_chunks)], axis=0))
            return jnp.concatenate(y_cols, axis=1).astype(BF16)

        if run_u:
            h_prev = h_ref[...]
            zu = u_proj(0)
        if run_v:
            h = _rms_norm(x_ref[...], g_ref[layer:layer + 1, :]).astype(BF16)
        zv_blocks = []
        y_blocks = []
        for b in range(n_blocks):
            if run_v:
                zv_blocks.append(v_proj(b))
            if run_u:
                zu_next = u_proj(b + 1) if b + 1 < n_blocks else None
                y_blocks.append(gate(b, _gelu_tanh(zu)))
                zu = zu_next
        if run_u:
            acc = xp_ref[...]
            for k in range(0, n_blocks, OUT_PROJ_BLOCKS):
                acc = acc + _dot(jnp.concatenate(y_blocks[k:k + OUT_PROJ_BLOCKS], axis=1),
                                 wout_ref[k * col_block:(k + OUT_PROJ_BLOCKS) * col_block, 0:d])
            o_ref[...] = acc
        if run_v:
            v_blocks = []
            sum_v = jnp.zeros((tm, LANES), F32)
            sum_vv = jnp.zeros((tm, LANES), F32)
            for b in range(n_blocks):
                v_blocks.append(_gelu_tanh(zv_blocks[b]))
                for c in range(0, col_block, LANES):
                    t = v_blocks[b][:, c:c + LANES]
                    sum_v = sum_v + t
                    sum_vv = sum_vv + t * t
            mu = jnp.sum(sum_v, axis=-1, keepdims=True) * (1.0 / e_a)
            var = jnp.maximum(jnp.sum(sum_vv, axis=-1, keepdims=True) * (1.0 / e_a) - mu * mu, 0.0)
            rstd = lax.rsqrt(var + EPS)
            for b in range(n_blocks):
                spatial(b)
            h_ref[...] = h
        _cast_blocks(cast_src, cast_dst)

    last = pl.num_programs(0) - 1
    pl.when(step == n_prep)(functools.partial(run_streams, False, True))
    pl.when((step > n_prep) & (step < last))(functools.partial(run_streams, True, True))
    pl.when(step == last)(functools.partial(run_streams, True, False))


def _sgu(x2, norms, win, vg, vb, ws, bs, wout, next_weights, *, layer, mixer, tm):
    m, d = x2.shape
    e_a = wout.shape[1]
    n_groups = ws.shape[1]
    n_tiles = m // tm
    n_prep = 8
    assert win.shape[1] % (n_prep * BF16_ROWS) == 0 and e_a % (n_prep * BF16_ROWS) == 0
    main = lambda i: jnp.maximum(i - n_prep, 0)
    prep_block = lambda i: jnp.minimum(i, n_prep - 1)
    casts = _cast_jobs(next_weights, layer, n_tiles)
    cast_specs = [c.specs(main) for c in casts]
    cur = lambda i: (jnp.minimum(main(i), n_tiles - 1), 0)
    prev = lambda i: (jnp.maximum(main(i) - 1, 0), 0)
    outs = pl.pallas_call(
        functools.partial(_sgu_kernel, layer=layer, n_casts=len(casts), n_prep=n_prep,
                          col_block=MXU_N),
        out_shape=[jax.ShapeDtypeStruct((m, d), F32)] + [c.out_shape() for c in casts],
        grid=(n_prep + n_tiles + 1,),
        in_specs=[
            pl.BlockSpec((tm, d), cur),
            pl.BlockSpec((tm, d), prev),
            _resident(norms.shape),
            pl.BlockSpec((None, win.shape[1] // n_prep, win.shape[2]),
                         lambda i: (mixer, prep_block(i), 0)),
            _resident_layer(vg, mixer),
            _resident_layer(vb, mixer),
            _resident_layer(ws, mixer),
            _resident_layer(bs, mixer),
            pl.BlockSpec((None, e_a // n_prep, d), lambda i: (mixer, prep_block(i), 0)),
        ] + [s[0] for s in cast_specs],
        out_specs=[pl.BlockSpec((tm, d), prev)] + [s[1] for s in cast_specs],
        scratch_shapes=[
            pltpu.VMEM((win.shape[1], _odd_tile_pitch(win.shape[2])), BF16),
            pltpu.VMEM((e_a, _odd_tile_pitch(d)), BF16),
            pltpu.VMEM((n_groups, CHUNK, CHUNK), BF16),
            pltpu.VMEM((CHUNK, e_a), F32),
            pltpu.VMEM((tm, d), BF16),
            pltpu.VMEM((n_groups, CHUNK, tm), F32),
        ],
        compiler_params=_compiler_params(1),
        name="sgu_mixer",
    )(x2, x2, norms, win, vg, vb, ws, bs, wout, *[c.stacked for c in casts])
    return outs[0], outs[1:]


def _conv_kernel(*refs, layer, n_casts, halo):
    x_ref, g_ref, win_ref, cw_ref, wout_ref = refs[:5]
    cast_src = refs[5:5 + n_casts]
    o_ref = refs[5 + n_casts]
    cast_dst = refs[6 + n_casts:6 + 2 * n_casts]
    z_ref, = refs[6 + 2 * n_casts:]

    tm = x_ref.shape[1]
    e_b = wout_ref.shape[0]
    conv_w = cw_ref.shape[0]
    n_blocks, _, col_block = z_ref.shape

    @pl.when(pl.program_id(1) == 0)
    def _():
        z_ref[:, 0:halo, :] = jnp.zeros((n_blocks, halo, col_block), F32)

    @pl.when(pl.program_id(1) != 0)
    def _():
        z_ref[:, 0:halo, :] = z_ref[:, tm:tm + halo, :]

    x = x_ref[0]
    h = _rms_norm(x, g_ref[layer:layer + 1, :]).astype(BF16)
    y_blocks = []
    for j in range(n_blocks):
        c = j * col_block
        cols = slice(c, c + col_block)
        z_ref[j, halo:halo + tm, :] = (_dot(h, win_ref[:, e_b + c:e_b + c + col_block])
                                       * _dot(h, win_ref[:, 2 * e_b + c:2 * e_b + c + col_block]))
        b_gate = _dot(h, win_ref[:, cols])
        conv = z_ref[j, halo:halo + tm, :] * cw_ref[conv_w - 1:conv_w, cols]
        for k in range(conv_w - 1):
            back = conv_w - 1 - k
            conv = conv + z_ref[j, halo - back:halo - back + tm, :] * cw_ref[k:k + 1, cols]
        y_blocks.append((b_gate * conv).astype(BF16))
    o_ref[0] = x + _dot(jnp.concatenate(y_blocks, axis=1), wout_ref[:, 0:x.shape[1]])
    _cast_blocks(cast_src, cast_dst)


def _short_conv(x3, norms, win, cw, wout, next_weights, *, layer, mixer, tm):
    b, s, d = x3.shape
    e_b = wout.shape[0]
    halo = SUBLANES
    assert cw.shape[1] - 1 <= halo
    tiles_per_seq = s // tm
    casts = _cast_jobs(next_weights, layer, b * tiles_per_seq)
    cast_specs = [c.specs(lambda i, j: i * tiles_per_seq + j) for c in casts]
    outs = pl.pallas_call(
        functools.partial(_conv_kernel, layer=layer, n_casts=len(casts), halo=halo),
        out_shape=[jax.ShapeDtypeStruct((b, s, d), F32)] + [c.out_shape() for c in casts],
        grid=(b, tiles_per_seq),
        in_specs=[
            pl.BlockSpec((1, tm, d), lambda i, j: (i, j, 0)),
            _resident(norms.shape),
            _resident(win.shape),
            _resident_layer(cw, mixer),
            _resident(wout.shape),
        ] + [s[0] for s in cast_specs],
        out_specs=[pl.BlockSpec((1, tm, d), lambda i, j: (i, j, 0))] + [s[1] for s in cast_specs],
        scratch_shapes=[pltpu.VMEM((e_b // MXU_N, halo + tm, MXU_N), F32)],
        compiler_params=_compiler_params(2),
        name="short_conv_mixer",
    )(x3, norms, win, cw, wout, *[c.stacked for c in casts])
    return outs[0], outs[1:]


def kernel(x, mix_norm, ffn_norm, a_w_in, a_v_gain, a_v_bias, a_w_s, a_b_s, a_w_out,
           b_w_in, b_conv_w, b_w_out, ffn_w_gate, ffn_w_up, ffn_w_down, final_norm):
    b, s, d = x.shape
    depth = mix_norm.shape[0]
    tm = 1024
    tm_sgu = 512
    assert s % tm == 0 and s % tm_sgu == 0 and tm_sgu % CHUNK == 0 and a_w_s.shape[-1] == CHUNK
    assert a_w_s.shape[1] * LANES == a_w_out.shape[1]

    fg = final_norm.reshape(1, d)
    a_v_gain3 = a_v_gain.reshape(a_v_gain.shape[0], 1, -1)
    a_v_bias3 = a_v_bias.reshape(a_v_bias.shape[0], 1, -1)
    ffn_weights = (ffn_w_gate, ffn_w_up, ffn_w_down)

    def mixer_weights(i):
        return (a_w_in, a_w_out) if i % 2 == 0 else (b_w_in, b_w_out)

    assert depth <= 2, "only the first layer's spatial-gating call casts its own weights"
    ready = None
    x2 = x.reshape(b * s, d)
    for i in range(depth):
        j = i // 2
        if i % 2 == 0:
            x2, ready = _sgu(x2, mix_norm, a_w_in, a_v_gain3, a_v_bias3, a_w_s, a_b_s, a_w_out,
                             ffn_weights, layer=i, mixer=j, tm=tm_sgu)
        else:
            x3, ready = _short_conv(x2.reshape(b, s, d), mix_norm, ready[0], b_conv_w, ready[1],
                                    ffn_weights, layer=i, mixer=j, tm=tm)
            x2 = x3.reshape(b * s, d)
        last = i == depth - 1
        x2, ready = _ffn(x2, ffn_norm, *ready, fg, () if last else mixer_weights(i + 1),
                         layer=i, next_mixer=(i + 1) // 2, tm=tm, final_norm=last)
    return x2.reshape(b, s, d)
```

```python
import functools
import math
from typing import NamedTuple

import jax
import jax.numpy as jnp
from jax import lax
from jax.experimental import pallas as pl
from jax.experimental.pallas import tpu as pltpu

EPS = 1e-6
CHUNK = 128
LANES = 128
SUBLANES = 8
BF16_ROWS = 16
MXU_N = 256
OUT_PROJ_BLOCKS = 4
DOWN_PROJ_ROW_PARTS = 4
VMEM_LIMIT_BYTES = 56 * 1024 * 1024

BF16 = jnp.bfloat16
F32 = jnp.float32


def _dot(a, b):
    return jnp.dot(a, b, preferred_element_type=F32)


def _rms_norm(x, g):
    y = x * lax.rsqrt(jnp.mean(x * x, axis=-1, keepdims=True) + EPS)
    return y * g


def _gelu_tanh(x):
    k = -2.0 * math.sqrt(2.0 / math.pi) * math.log2(math.e)
    return x / (1.0 + jnp.exp2(x * (k + (k * 0.044715) * (x * x))))


def _odd_tile_pitch(n):
    assert n % LANES == 0
    return n if (n // LANES) % 2 == 1 else n + LANES


def _resident(shape):
    return pl.BlockSpec(shape, lambda *_: (0,) * len(shape), pipeline_mode=pl.Buffered(1))


def _resident_layer(stacked, layer):
    rest = stacked.shape[1:]
    return pl.BlockSpec((None,) + rest, lambda *_: (layer,) + (0,) * len(rest),
                        pipeline_mode=pl.Buffered(1))


def _compiler_params(n_grid_axes):
    return pltpu.CompilerParams(dimension_semantics=("arbitrary",) * n_grid_axes,
                                vmem_limit_bytes=VMEM_LIMIT_BYTES)


class _Cast(NamedTuple):
    stacked: jax.Array
    layer: int
    rows: int

    @property
    def n_blocks(self):
        n_rows = self.stacked.shape[1]
        assert n_rows % self.rows == 0 and self.rows % BF16_ROWS == 0
        return n_rows // self.rows

    def out_shape(self):
        _, n_rows, n_cols = self.stacked.shape
        return jax.ShapeDtypeStruct((n_rows, _odd_tile_pitch(n_cols)), BF16)

    def specs(self, linear_step):
        n_cols = self.stacked.shape[2]
        block = lambda *g: jnp.minimum(linear_step(*g), self.n_blocks - 1)
        return (pl.BlockSpec((None, self.rows, n_cols), lambda *g: (self.layer, block(*g), 0)),
                pl.BlockSpec((self.rows, _odd_tile_pitch(n_cols)), lambda *g: (block(*g), 0)))


def _cast_jobs(weights, layer, n_steps):
    jobs = []
    for w in weights:
        n_rows = w.shape[1]
        rows = next(r for r in range(BF16_ROWS, n_rows + 1, BF16_ROWS)
                    if n_rows % r == 0 and n_rows // r <= n_steps)
        jobs.append(_Cast(w, layer, rows))
    return jobs


def _cast_blocks(src_refs, dst_refs):
    for src, dst in zip(src_refs, dst_refs):
        n = src.shape[1]
        dst[:, 0:n] = src[...].astype(BF16)
        if dst.shape[1] > n:
            dst[:, n:] = jnp.zeros((dst.shape[0], dst.shape[1] - n), BF16)


def _ffn_kernel(*refs, layer, n_casts, col_block, final_norm):
    x_ref, g_ref, wg_ref, wu_ref, wd_hbm_ref, fg_ref = refs[:6]
    cast_src = refs[6:6 + n_casts]
    o_ref = refs[6 + n_casts]
    cast_dst = refs[7 + n_casts:7 + 2 * n_casts]
    wd_ref, wd_sem = refs[7 + 2 * n_casts:]
    wd_copy = pltpu.make_async_copy(wd_hbm_ref, wd_ref, wd_sem)

    def run(first_step):
        if first_step:
            wd_copy.start()
        x = x_ref[...]
        d = x.shape[1]
        h = _rms_norm(x, g_ref[layer:layer + 1, :]).astype(BF16)
        d_ff = wd_ref.shape[0]
        a_blocks = []
        for c in range(0, d_ff, col_block):
            g = _dot(h, wg_ref[:, c:c + col_block])
            u = _dot(h, wu_ref[:, c:c + col_block])
            a_blocks.append((g * (1.0 / (1.0 + jnp.exp(-g))) * u).astype(BF16))
        a = jnp.concatenate(a_blocks, axis=1)
        if first_step:
            wd_copy.wait()
        rows = x.shape[0] // DOWN_PROJ_ROW_PARTS
        for r in range(0, x.shape[0], rows):
            acc = x[r:r + rows] + _dot(a[r:r + rows], wd_ref[:, 0:d])
            if final_norm:
                acc = _rms_norm(acc, fg_ref[...])
            o_ref[r:r + rows, :] = acc
        _cast_blocks(cast_src, cast_dst)

    step = pl.program_id(0)
    pl.when(step == 0)(functools.partial(run, True))
    pl.when(step > 0)(functools.partial(run, False))


def _ffn(x2, norms, wg, wu, wd, fg, next_weights, *, layer, next_mixer, tm, final_norm):
    m, d = x2.shape
    n_steps = m // tm
    casts = _cast_jobs(next_weights, next_mixer, n_steps)
    cast_specs = [c.specs(lambda i: i) for c in casts]
    outs = pl.pallas_call(
        functools.partial(_ffn_kernel, layer=layer, n_casts=len(casts), col_block=MXU_N,
                          final_norm=final_norm),
        out_shape=[jax.ShapeDtypeStruct((m, d), F32)] + [c.out_shape() for c in casts],
        grid=(n_steps,),
        in_specs=[
            pl.BlockSpec((tm, d), lambda i: (i, 0)),
            _resident(norms.shape),
            _resident(wg.shape),
            _resident(wu.shape),
            pl.BlockSpec(memory_space=pl.ANY),
            _resident(fg.shape),
        ] + [s[0] for s in cast_specs],
        out_specs=[pl.BlockSpec((tm, d), lambda i: (i, 0))] + [s[1] for s in cast_specs],
        scratch_shapes=[pltpu.VMEM(wd.shape, BF16), pltpu.SemaphoreType.DMA(())],
        compiler_params=_compiler_params(1),
        name="swiglu_final" if final_norm else "swiglu",
    )(x2, norms, wg, wu, wd, fg, *[c.stacked for c in casts])
    return outs[0], outs[1:]


def _sgu_kernel(*refs, layer, n_casts, n_prep, col_block):
    x_ref, xp_ref, g_ref, win_f32_ref, vg_ref, vb_ref, ws_ref, bs_ref, wout_f32_ref = refs[:9]
    cast_src = refs[9:9 + n_casts]
    o_ref = refs[9 + n_casts]
    cast_dst = refs[10 + n_casts:10 + 2 * n_casts]
    win_ref, wout_ref, wtri_ref, gate_bias_ref, h_ref, sv_ref = refs[10 + 2 * n_casts:]

    tm, d = x_ref.shape
    e_a = wout_ref.shape[0]
    n_groups = ws_ref.shape[0]
    n_chunks = tm // CHUNK
    n_blocks = e_a // col_block
    groups_per_block = col_block // LANES
    assert n_blocks % OUT_PROJ_BLOCKS == 0
    step = pl.program_id(0)

    @pl.when(step < n_prep)
    def _():
        for src, dst in ((win_f32_ref, win_ref), (wout_f32_ref, wout_ref)):
            rows, n = src.shape
            at = pl.ds(pl.multiple_of(step * rows, rows), rows)
            dst[at, 0:n] = src[...].astype(BF16)
            if dst.shape[1] > n:
                dst[at, n:] = jnp.zeros((rows, dst.shape[1] - n), BF16)

    @pl.when(step == n_prep)
    def _():
        row = lax.broadcasted_iota(jnp.int32, (CHUNK, CHUNK), 0)
        col = lax.broadcasted_iota(jnp.int32, (CHUNK, CHUNK), 1)
        for hg in range(n_groups):
            lanes = slice(hg * LANES, (hg + 1) * LANES)
            w_causal = jnp.where(col <= row, ws_ref[hg], 0.0)
            wtri_ref[hg] = w_causal.astype(BF16)
            b_col = jnp.sum(jnp.where(col == row, bs_ref[hg:hg + 1, :], 0.0), axis=1, keepdims=True)
            gate_bias_ref[:, lanes] = (jnp.sum(w_causal, axis=1, keepdims=True) * vb_ref[:, lanes]
                                       + b_col)

    def run_streams(run_u, run_v):
        def v_proj(b):
            return _dot(h, win_ref[:, e_a + b * col_block:e_a + (b + 1) * col_block])

        def spatial(b):
            vhat = ((v_blocks[b] - mu) * rstd).astype(BF16)
            for gi in range(groups_per_block):
                lanes = slice(gi * LANES, (gi + 1) * LANES)
                rhs = jnp.concatenate(
                    [vhat[n * CHUNK:(n + 1) * CHUNK, lanes] for n in range(n_chunks)], axis=1)
                sv_ref[b * groups_per_block + gi] = _dot(wtri_ref[b * groups_per_block + gi], rhs)

        def u_proj(b):
            return _dot(h_prev, win_ref[:, b * col_block:(b + 1) * col_block])

        def gate(b, u):
            y_cols = []
            for gi in range(groups_per_block):
                hg = b * groups_per_block + gi
                lanes = slice(gi * LANES, (gi + 1) * LANES)
                gain = vg_ref[:, hg * LANES:(hg + 1) * LANES]
                bias = gate_bias_ref[:, hg * LANES:(hg + 1) * LANES]
                y_cols.append(jnp.concatenate(
                    [u[n * CHUNK:(n + 1) * CHUNK, lanes]
                     * (sv_ref[hg, :, n * LANES:(n + 1) * LANES] * gain + bias)
                     for n in range(n_chunks)], axis=0))
            return jnp.concatenate(y_cols, axis=1).astype(BF16)

        if run_u:
            h_prev = h_ref[...]
            zu = u_proj(0)
        if run_v:
            h = _rms_norm(x_ref[...], g_ref[layer:layer + 1, :]).astype(BF16)
        zv_blocks = []
        y_blocks = []
        for b in range(n_blocks):
            if run_v:
                zv_blocks.append(v_proj(b))
            if run_u:
                zu_next = u_proj(b + 1) if b + 1 < n_blocks else None
                y_blocks.append(gate(b, _gelu_tanh(zu)))
                zu = zu_next
        if run_u:
            acc = xp_ref[...]
            for k in range(0, n_blocks, OUT_PROJ_BLOCKS):
                acc = acc + _dot(jnp.concatenate(y_blocks[k:k + OUT_PROJ_BLOCKS], axis=1),
                                 wout_ref[k * col_block:(k + OUT_PROJ_BLOCKS) * col_block, 0:d])
            o_ref[...] = acc
        if run_v:
            v_blocks = []
            sum_v = jnp.zeros((tm, LANES), F32)
            sum_vv = jnp.zeros((tm, LANES), F32)
            for b in range(n_blocks):
                v_blocks.append(_gelu_tanh(zv_blocks[b]))
                for c in range(0, col_block, LANES):
                    t = v_blocks[b][:, c:c + LANES]
                    sum_v = sum_v + t
                    sum_vv = sum_vv + t * t
            mu = jnp.sum(sum_v, axis=-1, keepdims=True) * (1.0 / e_a)
            var = jnp.maximum(jnp.sum(sum_vv, axis=-1, keepdims=True) * (1.0 / e_a) - mu * mu, 0.0)
            rstd = lax.rsqrt(var + EPS)
            for b in range(n_blocks):
                spatial(b)
            h_ref[...] = h
        _cast_blocks(cast_src, cast_dst)

    last = pl.num_programs(0) - 1
    pl.when(step == n_prep)(functools.partial(run_streams, False, True))
    pl.when((step > n_prep) & (step < last))(functools.partial(run_streams, True, True))
    pl.when(step == last)(functools.partial(run_streams, True, False))


def _sgu(x2, norms, win, vg, vb, ws, bs, wout, next_weights, *, layer, mixer, tm):
    m, d = x2.shape
    e_a = wout.shape[1]
    n_groups = ws.shape[1]
    n_tiles = m // tm
    n_prep = 8
    assert win.shape[1] % (n_prep * BF16_ROWS) == 0 and e_a % (n_prep * BF16_ROWS) == 0
    main = lambda i: jnp.maximum(i - n_prep, 0)
    prep_block = lambda i: jnp.minimum(i, n_prep - 1)
    casts = _cast_jobs(next_weights, layer, n_tiles)
    cast_specs = [c.specs(main) for c in casts]
    cur = lambda i: (jnp.minimum(main(i), n_tiles - 1), 0)
    prev = lambda i: (jnp.maximum(main(i) - 1, 0), 0)
    outs = pl.pallas_call(
        functools.partial(_sgu_kernel, layer=layer, n_casts=len(casts), n_prep=n_prep,
                          col_block=MXU_N),
        out_shape=[jax.ShapeDtypeStruct((m, d), F32)] + [c.out_shape() for c in casts],
        grid=(n_prep + n_tiles + 1,),
        in_specs=[
            pl.BlockSpec((tm, d), cur),
            pl.BlockSpec((tm, d), prev),
            _resident(norms.shape),
            pl.BlockSpec((None, win.shape[1] // n_prep, win.shape[2]),
                         lambda i: (mixer, prep_block(i), 0)),
            _resident_layer(vg, mixer),
            _resident_layer(vb, mixer),
            _resident_layer(ws, mixer),
            _resident_layer(bs, mixer),
            pl.BlockSpec((None, e_a // n_prep, d), lambda i: (mixer, prep_block(i), 0)),
        ] + [s[0] for s in cast_specs],
        out_specs=[pl.BlockSpec((tm, d), prev)] + [s[1] for s in cast_specs],
        scratch_shapes=[
            pltpu.VMEM((win.shape[1], _odd_tile_pitch(win.shape[2])), BF16),
            pltpu.VMEM((e_a, _odd_tile_pitch(d)), BF16),
            pltpu.VMEM((n_groups, CHUNK, CHUNK), BF16),
            pltpu.VMEM((CHUNK, e_a), F32),
            pltpu.VMEM((tm, d), BF16),
            pltpu.VMEM((n_groups, CHUNK, tm), F32),
        ],
        compiler_params=_compiler_params(1),
        name="sgu_mixer",
    )(x2, x2, norms, win, vg, vb, ws, bs, wout, *[c.stacked for c in casts])
    return outs[0], outs[1:]


def _conv_kernel(*refs, layer, n_casts, halo):
    x_ref, g_ref, win_ref, cw_ref, wout_ref = refs[:5]
    cast_src = refs[5:5 + n_casts]
    o_ref = refs[5 + n_casts]
    cast_dst = refs[6 + n_casts:6 + 2 * n_casts]
    z_ref, = refs[6 + 2 * n_casts:]

    tm = x_ref.shape[1]
    e_b = wout_ref.shape[0]
    conv_w = cw_ref.shape[0]
    n_blocks, _, col_block = z_ref.shape

    @pl.when(pl.program_id(1) == 0)
    def _():
        z_ref[:, 0:halo, :] = jnp.zeros((n_blocks, halo, col_block), F32)

    @pl.when(pl.program_id(1) != 0)
    def _():
        z_ref[:, 0:halo, :] = z_ref[:, tm:tm + halo, :]

    x = x_ref[0]
    h = _rms_norm(x, g_ref[layer:layer + 1, :]).astype(BF16)
    y_blocks = []
    for j in range(n_blocks):
        c = j * col_block
        cols = slice(c, c + col_block)
        z_ref[j, halo:halo + tm, :] = (_dot(h, win_ref[:, e_b + c:e_b + c + col_block])
                                       * _dot(h, win_ref[:, 2 * e_b + c:2 * e_b + c + col_block]))
        b_gate = _dot(h, win_ref[:, cols])
        conv = z_ref[j, halo:halo + tm, :] * cw_ref[conv_w - 1:conv_w, cols]
        for k in range(conv_w - 1):
            back = conv_w - 1 - k
            conv = conv + z_ref[j, halo - back:halo - back + tm, :] * cw_ref[k:k + 1, cols]
        y_blocks.append((b_gate * conv).astype(BF16))
    o_ref[0] = x + _dot(jnp.concatenate(y_blocks, axis=1), wout_ref[:, 0:x.shape[1]])
    _cast_blocks(cast_src, cast_dst)


def _short_conv(x3, norms, win, cw, wout, next_weights, *, layer, mixer, tm):
    b, s, d = x3.shape
    e_b = wout.shape[0]
    halo = SUBLANES
    assert cw.shape[1] - 1 <= halo
    tiles_per_seq = s // tm
    casts = _cast_jobs(next_weights, layer, b * tiles_per_seq)
    cast_specs = [c.specs(lambda i, j: i * tiles_per_seq + j) for c in casts]
    outs = pl.pallas_call(
        functools.partial(_conv_kernel, layer=layer, n_casts=len(casts), halo=halo),
        out_shape=[jax.ShapeDtypeStruct((b, s, d), F32)] + [c.out_shape() for c in casts],
        grid=(b, tiles_per_seq),
        in_specs=[
            pl.BlockSpec((1, tm, d), lambda i, j: (i, j, 0)),
            _resident(norms.shape),
            _resident(win.shape),
            _resident_layer(cw, mixer),
            _resident(wout.shape),
        ] + [s[0] for s in cast_specs],
        out_specs=[pl.BlockSpec((1, tm, d), lambda i, j: (i, j, 0))] + [s[1] for s in cast_specs],
        scratch_shapes=[pltpu.VMEM((e_b // MXU_N, halo + tm, MXU_N), F32)],
        compiler_params=_compiler_params(2),
        name="short_conv_mixer",
    )(x3, norms, win, cw, wout, *[c.stacked for c in casts])
    return outs[0], outs[1:]


def kernel(x, mix_norm, ffn_norm, a_w_in, a_v_gain, a_v_bias, a_w_s, a_b_s, a_w_out,
           b_w_in, b_conv_w, b_w_out, ffn_w_gate, ffn_w_up, ffn_w_down, final_norm):
    b, s, d = x.shape
    depth = mix_norm.shape[0]
    tm = 1024
    tm_sgu = 512
    assert s % tm == 0 and s % tm_sgu == 0 and tm_sgu % CHUNK == 0 and a_w_s.shape[-1] == CHUNK
    assert a_w_s.shape[1] * LANES == a_w_out.shape[1]

    fg = final_norm.reshape(1, d)
    a_v_gain3 = a_v_gain.reshape(a_v_gain.shape[0], 1, -1)
    a_v_bias3 = a_v_bias.reshape(a_v_bias.shape[0], 1, -1)
    ffn_weights = (ffn_w_gate, ffn_w_up, ffn_w_down)

    def mixer_weights(i):
        return (a_w_in, a_w_out) if i % 2 == 0 else (b_w_in, b_w_out)

    assert depth <= 2, "only the first layer's spatial-gating call casts its own weights"
    ready = None
    x2 = x.reshape(b * s, d)
    for i in range(depth):
        j = i // 2
        if i % 2 == 0:
            x2, ready = _sgu(x2, mix_norm, a_w_in, a_v_gain3, a_v_bias3, a_w_s, a_b_s, a_w_out,
                             ffn_weights, layer=i, mixer=j, tm=tm_sgu)
        else:
            x3, ready = _short_conv(x2.reshape(b, s, d), mix_norm, ready[0], b_conv_w, ready[1],
                                    ffn_weights, layer=i, mixer=j, tm=tm)
            x2 = x3.reshape(b * s, d)
        last = i == depth - 1
        x2, ready = _ffn(x2, ffn_norm, *ready, fg, () if last else mixer_weights(i + 1),
                         layer=i, next_mixer=(i + 1) // 2, tm=tm, final_norm=last)
    return x2.reshape(b, s, d)
```

```python
import functools
import math
from typing import NamedTuple

import jax
import jax.numpy as jnp
from jax import lax
from jax.experimental import pallas as pl
from jax.experimental.pallas import tpu as pltpu

EPS = 1e-6
CHUNK = 128
LANES = 128
SUBLANES = 8
BF16_ROWS = 16
MXU_N = 256
OUT_PROJ_BLOCKS = 4
DOWN_PROJ_ROW_PARTS = 2
VMEM_LIMIT_BYTES = 56 * 1024 * 1024

BF16 = jnp.bfloat16
F32 = jnp.float32


def _dot(a, b):
    return jnp.dot(a, b, preferred_element_type=F32)


def _rms_norm(x, g):
    y = x * lax.rsqrt(jnp.mean(x * x, axis=-1, keepdims=True) + EPS)
    return y * g


def _gelu_tanh(x):
    k = -2.0 * math.sqrt(2.0 / math.pi) * math.log2(math.e)
    return x / (1.0 + jnp.exp2(x * (k + (k * 0.044715) * (x * x))))


def _odd_tile_pitch(n):
    assert n % LANES == 0
    return n if (n // LANES) % 2 == 1 else n + LANES


def _resident(shape):
    return pl.BlockSpec(shape, lambda *_: (0,) * len(shape), pipeline_mode=pl.Buffered(1))


def _resident_layer(stacked, layer):
    rest = stacked.shape[1:]
    return pl.BlockSpec((None,) + rest, lambda *_: (layer,) + (0,) * len(rest),
                        pipeline_mode=pl.Buffered(1))


def _compiler_params(n_grid_axes):
    return pltpu.CompilerParams(dimension_semantics=("arbitrary",) * n_grid_axes,
                                vmem_limit_bytes=VMEM_LIMIT_BYTES)


class _Cast(NamedTuple):
    stacked: jax.Array
    layer: int
    rows: int

    @property
    def n_blocks(self):
        n_rows = self.stacked.shape[1]
        assert n_rows % self.rows == 0 and self.rows % BF16_ROWS == 0
        return n_rows // self.rows

    def out_shape(self):
        _, n_rows, n_cols = self.stacked.shape
        return jax.ShapeDtypeStruct((n_rows, _odd_tile_pitch(n_cols)), BF16)

    def specs(self, linear_step):
        n_cols = self.stacked.shape[2]
        block = lambda *g: jnp.minimum(linear_step(*g), self.n_blocks - 1)
        return (pl.BlockSpec((None, self.rows, n_cols), lambda *g: (self.layer, block(*g), 0)),
                pl.BlockSpec((self.rows, _odd_tile_pitch(n_cols)), lambda *g: (block(*g), 0)))


def _cast_jobs(weights, layer, n_steps):
    jobs = []
    for w in weights:
        n_rows = w.shape[1]
        rows = next(r for r in range(BF16_ROWS, n_rows + 1, BF16_ROWS)
                    if n_rows % r == 0 and n_rows // r <= n_steps)
        jobs.append(_Cast(w, layer, rows))
    return jobs


def _cast_blocks(src_refs, dst_refs):
    for src, dst in zip(src_refs, dst_refs):
        n = src.shape[1]
        dst[:, 0:n] = src[...].astype(BF16)
        if dst.shape[1] > n:
            dst[:, n:] = jnp.zeros((dst.shape[0], dst.shape[1] - n), BF16)


def _ffn_kernel(*refs, layer, n_casts, col_block, final_norm):
    x_ref, g_ref, wg_ref, wu_ref, wd_ref, fg_ref = refs[:6]
    cast_src = refs[6:6 + n_casts]
    o_ref = refs[6 + n_casts]
    cast_dst = refs[7 + n_casts:]

    x = x_ref[...]
    d = x.shape[1]
    h = _rms_norm(x, g_ref[layer:layer + 1, :]).astype(BF16)
    d_ff = wd_ref.shape[0]
    a_blocks = []
    for c in range(0, d_ff, col_block):
        g = _dot(h, wg_ref[:, c:c + col_block])
        u = _dot(h, wu_ref[:, c:c + col_block])
        a_blocks.append((g * (1.0 / (1.0 + jnp.exp(-g))) * u).astype(BF16))
    a = jnp.concatenate(a_blocks, axis=1)
    rows = x.shape[0] // DOWN_PROJ_ROW_PARTS
    for r in range(0, x.shape[0], rows):
        acc = x[r:r + rows] + _dot(a[r:r + rows], wd_ref[:, 0:d])
        if final_norm:
            acc = _rms_norm(acc, fg_ref[...])
        o_ref[r:r + rows, :] = acc
    _cast_blocks(cast_src, cast_dst)


def _ffn(x2, norms, wg, wu, wd, fg, next_weights, *, layer, next_mixer, tm, final_norm):
    m, d = x2.shape
    n_steps = m // tm
    casts = _cast_jobs(next_weights, next_mixer, n_steps)
    cast_specs = [c.specs(lambda i: i) for c in casts]
    outs = pl.pallas_call(
        functools.partial(_ffn_kernel, layer=layer, n_casts=len(casts), col_block=MXU_N,
                          final_norm=final_norm),
        out_shape=[jax.ShapeDtypeStruct((m, d), F32)] + [c.out_shape() for c in casts],
        grid=(n_steps,),
        in_specs=[
            pl.BlockSpec((tm, d), lambda i: (i, 0)),
            _resident(norms.shape),
            _resident(wg.shape),
            _resident(wu.shape),
            _resident(wd.shape),
            _resident(fg.shape),
        ] + [s[0] for s in cast_specs],
        out_specs=[pl.BlockSpec((tm, d), lambda i: (i, 0))] + [s[1] for s in cast_specs],
        compiler_params=_compiler_params(1),
        name="swiglu_final" if final_norm else "swiglu",
    )(x2, norms, wg, wu, wd, fg, *[c.stacked for c in casts])
    return outs[0], outs[1:]


def _sgu_kernel(*refs, layer, n_casts, n_prep, col_block):
    x_ref, xp_ref, g_ref, win_f32_ref, vg_ref, vb_ref, ws_ref, bs_ref, wout_f32_ref = refs[:9]
    cast_src = refs[9:9 + n_casts]
    o_ref = refs[9 + n_casts]
    cast_dst = refs[10 + n_casts:10 + 2 * n_casts]
    win_ref, wout_ref, wtri_ref, gate_bias_ref, h_ref, sv_ref = refs[10 + 2 * n_casts:]

    tm, d = x_ref.shape
    e_a = wout_ref.shape[0]
    n_groups = ws_ref.shape[0]
    n_chunks = tm // CHUNK
    n_blocks = e_a // col_block
    groups_per_block = col_block // LANES
    assert n_blocks % OUT_PROJ_BLOCKS == 0
    step = pl.program_id(0)

    @pl.when(step < n_prep)
    def _():
        for src, dst in ((win_f32_ref, win_ref), (wout_f32_ref, wout_ref)):
            rows, n = src.shape
            at = pl.ds(pl.multiple_of(step * rows, rows), rows)
            dst[at, 0:n] = src[...].astype(BF16)
            if dst.shape[1] > n:
                dst[at, n:] = jnp.zeros((rows, dst.shape[1] - n), BF16)

    @pl.when(step == n_prep)
    def _():
        row = lax.broadcasted_iota(jnp.int32, (CHUNK, CHUNK), 0)
        col = lax.broadcasted_iota(jnp.int32, (CHUNK, CHUNK), 1)
        for hg in range(n_groups):
            lanes = slice(hg * LANES, (hg + 1) * LANES)
            w_causal = jnp.where(col <= row, ws_ref[hg], 0.0)
            wtri_ref[hg] = w_causal.astype(BF16)
            b_col = jnp.sum(jnp.where(col == row, bs_ref[hg:hg + 1, :], 0.0), axis=1, keepdims=True)
            gate_bias_ref[:, lanes] = (jnp.sum(w_causal, axis=1, keepdims=True) * vb_ref[:, lanes]
                                       + b_col)

    def run_streams(run_u, run_v):
        def v_proj(b):
            return _dot(h, win_ref[:, e_a + b * col_block:e_a + (b + 1) * col_block])

        def spatial(b):
            vhat = ((v_blocks[b] - mu) * rstd).astype(BF16)
            for gi in range(groups_per_block):
                lanes = slice(gi * LANES, (gi + 1) * LANES)
                rhs = jnp.concatenate(
                    [vhat[n * CHUNK:(n + 1) * CHUNK, lanes] for n in range(n_chunks)], axis=1)
                sv_ref[b * groups_per_block + gi] = _dot(wtri_ref[b * groups_per_block + gi], rhs)

        def u_proj(b):
            return _dot(h_prev, win_ref[:, b * col_block:(b + 1) * col_block])

        def gate(b, u):
            y_cols = []
            for gi in range(groups_per_block):
                hg = b * groups_per_block + gi
                lanes = slice(gi * LANES, (gi + 1) * LANES)
                gain = vg_ref[:, hg * LANES:(hg + 1) * LANES]
                bias = gate_bias_ref[:, hg * LANES:(hg + 1) * LANES]
                y_cols.append(jnp.concatenate(
                    [u[n * CHUNK:(n + 1) * CHUNK, lanes]
                     * (sv_ref[hg, :, n * LANES:(n + 1) * LANES] * gain + bias)
                     for n in range(n_chunks)], axis=0))
            return jnp.concatenate(y_cols, axis=1).astype(BF16)

        if run_u:
            h_prev = h_ref[...]
            zu = u_proj(0)
        if run_v:
            h = _rms_norm(x_ref[...], g_ref[layer:layer + 1, :]).astype(BF16)
        zv_blocks = []
        y_blocks = []
        for b in range(n_blocks):
            if run_v:
                zv_blocks.append(v_proj(b))
            if run_u:
                zu_next = u_proj(b + 1) if b + 1 < n_blocks else None
                y_blocks.append(gate(b, _gelu_tanh(zu)))
                zu = zu_next
        if run_u:
            acc = xp_ref[...]
            for k in range(0, n_blocks, OUT_PROJ_BLOCKS):
                acc = acc + _dot(jnp.concatenate(y_blocks[k:k + OUT_PROJ_BLOCKS], axis=1),
                                 wout_ref[k * col_block:(k + OUT_PROJ_BLOCKS) * col_block, 0:d])
            o_ref[...] = acc
        if run_v:
            v_blocks = []
            sum_v = jnp.zeros((tm, LANES), F32)
            sum_vv = jnp.zeros((tm, LANES), F32)
            for b in range(n_blocks):
                v_blocks.append(_gelu_tanh(zv_blocks[b]))
                for c in range(0, col_block, LANES):
                    t = v_blocks[b][:, c:c + LANES]
                    sum_v = sum_v + t
                    sum_vv = sum_vv + t * t
            mu = jnp.sum(sum_v, axis=-1, keepdims=True) * (1.0 / e_a)
            var = jnp.maximum(jnp.sum(sum_vv, axis=-1, keepdims=True) * (1.0 / e_a) - mu * mu, 0.0)
            rstd = lax.rsqrt(var + EPS)
            for b in range(n_blocks):
                spatial(b)
            h_ref[...] = h
        _cast_blocks(cast_src, cast_dst)

    last = pl.num_programs(0) - 1
    pl.when(step == n_prep)(functools.partial(run_streams, False, True))
    pl.when((step > n_prep) & (step < last))(functools.partial(run_streams, True, True))
    pl.when(step == last)(functools.partial(run_streams, True, False))


def _sgu(x2, norms, win, vg, vb, ws, bs, wout, next_weights, *, layer, mixer, tm):
    m, d = x2.shape
    e_a = wout.shape[1]
    n_groups = ws.shape[1]
    n_tiles = m // tm
    n_prep = 8
    assert win.shape[1] % (n_prep * BF16_ROWS) == 0 and e_a % (n_prep * BF16_ROWS) == 0
    main = lambda i: jnp.maximum(i - n_prep, 0)
    prep_block = lambda i: jnp.minimum(i, n_prep - 1)
    casts = _cast_jobs(next_weights, layer, n_tiles)
    cast_specs = [c.specs(main) for c in casts]
    cur = lambda i: (jnp.minimum(main(i), n_tiles - 1), 0)
    prev = lambda i: (jnp.maximum(main(i) - 1, 0), 0)
    outs = pl.pallas_call(
        functools.partial(_sgu_kernel, layer=layer, n_casts=len(casts), n_prep=n_prep,
                          col_block=MXU_N),
        out_shape=[jax.ShapeDtypeStruct((m, d), F32)] + [c.out_shape() for c in casts],
        grid=(n_prep + n_tiles + 1,),
        in_specs=[
            pl.BlockSpec((tm, d), cur),
            pl.BlockSpec((tm, d), prev),
            _resident(norms.shape),
            pl.BlockSpec((None, win.shape[1] // n_prep, win.shape[2]),
                         lambda i: (mixer, prep_block(i), 0)),
            _resident_layer(vg, mixer),
            _resident_layer(vb, mixer),
            _resident_layer(ws, mixer),
            _resident_layer(bs, mixer),
            pl.BlockSpec((None, e_a // n_prep, d), lambda i: (mixer, prep_block(i), 0)),
        ] + [s[0] for s in cast_specs],
        out_specs=[pl.BlockSpec((tm, d), prev)] + [s[1] for s in cast_specs],
        scratch_shapes=[
            pltpu.VMEM((win.shape[1], _odd_tile_pitch(win.shape[2])), BF16),
            pltpu.VMEM((e_a, _odd_tile_pitch(d)), BF16),
            pltpu.VMEM((n_groups, CHUNK, CHUNK), BF16),
            pltpu.VMEM((CHUNK, e_a), F32),
            pltpu.VMEM((tm, d), BF16),
            pltpu.VMEM((n_groups, CHUNK, tm), F32),
        ],
        compiler_params=_compiler_params(1),
        name="sgu_mixer",
    )(x2, x2, norms, win, vg, vb, ws, bs, wout, *[c.stacked for c in casts])
    return outs[0], outs[1:]


def _conv_kernel(*refs, layer, n_casts, halo):
    x_ref, g_ref, win_ref, cw_ref, wout_ref = refs[:5]
    cast_src = refs[5:5 + n_casts]
    o_ref = refs[5 + n_casts]
    cast_dst = refs[6 + n_casts:6 + 2 * n_casts]
    z_ref, = refs[6 + 2 * n_casts:]

    tm = x_ref.shape[1]
    e_b = wout_ref.shape[0]
    conv_w = cw_ref.shape[0]
    n_blocks, _, col_block = z_ref.shape

    @pl.when(pl.program_id(1) == 0)
    def _():
        z_ref[:, 0:halo, :] = jnp.zeros((n_blocks, halo, col_block), F32)

    @pl.when(pl.program_id(1) != 0)
    def _():
        z_ref[:, 0:halo, :] = z_ref[:, tm:tm + halo, :]

    x = x_ref[0]
    h = _rms_norm(x, g_ref[layer:layer + 1, :]).astype(BF16)
    y_blocks = []
    for j in range(n_blocks):
        c = j * col_block
        cols = slice(c, c + col_block)
        z_ref[j, halo:halo + tm, :] = (_dot(h, win_ref[:, e_b + c:e_b + c + col_block])
                                       * _dot(h, win_ref[:, 2 * e_b + c:2 * e_b + c + col_block]))
        b_gate = _dot(h, win_ref[:, cols])
        conv = z_ref[j, halo:halo + tm, :] * cw_ref[conv_w - 1:conv_w, cols]
        for k in range(conv_w - 1):
            back = conv_w - 1 - k
            conv = conv + z_ref[j, halo - back:halo - back + tm, :] * cw_ref[k:k + 1, cols]
        y_blocks.append((b_gate * conv).astype(BF16))
    o_ref[0] = x + _dot(jnp.concatenate(y_blocks, axis=1), wout_ref[:, 0:x.shape[1]])
    _cast_blocks(cast_src, cast_dst)


def _short_conv(x3, norms, win, cw, wout, next_weights, *, layer, mixer, tm):
    b, s, d = x3.shape
    e_b = wout.shape[0]
    halo = SUBLANES
    assert cw.shape[1] - 1 <= halo
    tiles_per_seq = s // tm
    casts = _cast_jobs(next_weights, layer, b * tiles_per_seq)
    cast_specs = [c.specs(lambda i, j: i * tiles_per_seq + j) for c in casts]
    outs = pl.pallas_call(
        functools.partial(_conv_kernel, layer=layer, n_casts=len(casts), halo=halo),
        out_shape=[jax.ShapeDtypeStruct((b, s, d), F32)] + [c.out_shape() for c in casts],
        grid=(b, tiles_per_seq),
        in_specs=[
            pl.BlockSpec((1, tm, d), lambda i, j: (i, j, 0)),
            _resident(norms.shape),
            _resident(win.shape),
            _resident_layer(cw, mixer),
            _resident(wout.shape),
        ] + [s[0] for s in cast_specs],
        out_specs=[pl.BlockSpec((1, tm, d), lambda i, j: (i, j, 0))] + [s[1] for s in cast_specs],
        scratch_shapes=[pltpu.VMEM((e_b // MXU_N, halo + tm, MXU_N), F32)],
        compiler_params=_compiler_params(2),
        name="short_conv_mixer",
    )(x3, norms, win, cw, wout, *[c.stacked for c in casts])
    return outs[0], outs[1:]


def kernel(x, mix_norm, ffn_norm, a_w_in, a_v_gain, a_v_bias, a_w_s, a_b_s, a_w_out,
           b_w_in, b_conv_w, b_w_out, ffn_w_gate, ffn_w_up, ffn_w_down, final_norm):
    b, s, d = x.shape
    depth = mix_norm.shape[0]
    tm = 1024
    tm_sgu = 512
    assert s % tm == 0 and s % tm_sgu == 0 and tm_sgu % CHUNK == 0 and a_w_s.shape[-1] == CHUNK
    assert a_w_s.shape[1] * LANES == a_w_out.shape[1]

    fg = final_norm.reshape(1, d)
    a_v_gain3 = a_v_gain.reshape(a_v_gain.shape[0], 1, -1)
    a_v_bias3 = a_v_bias.reshape(a_v_bias.shape[0], 1, -1)
    ffn_weights = (ffn_w_gate, ffn_w_up, ffn_w_down)

    def mixer_weights(i):
        return (a_w_in, a_w_out) if i % 2 == 0 else (b_w_in, b_w_out)

    assert depth <= 2, "only the first layer's spatial-gating call casts its own weights"
    ready = None
    x2 = x.reshape(b * s, d)
    for i in range(depth):
        j = i // 2
        if i % 2 == 0:
            x2, ready = _sgu(x2, mix_norm, a_w_in, a_v_gain3, a_v_bias3, a_w_s, a_b_s, a_w_out,
                             ffn_weights, layer=i, mixer=j, tm=tm_sgu)
        else:
            x3, ready = _short_conv(x2.reshape(b, s, d), mix_norm, ready[0], b_conv_w, ready[1],
                                    ffn_weights, layer=i, mixer=j, tm=tm)
            x2 = x3.reshape(b * s, d)
        last = i == depth - 1
        x2, ready = _ffn(x2, ffn_norm, *ready, fg, () if last else mixer_weights(i + 1),
                         layer=i, next_mixer=(i + 1) // 2, tm=tm, final_norm=last)
    return x2.reshape(b, s, d)
```
